```python
import math
import jax, jax.numpy as jnp
from jax import lax
import numpy as np

D_MODEL = 1024
BATCH = 4
SEQ = 8192
DEPTH = 1

ATTN_HEADS = 8
ATTN_HEAD_DIM = 64
ATTN_QK = ATTN_HEADS * 2 * ATTN_HEAD_DIM
ATTN_V = ATTN_HEADS * 2 * ATTN_HEAD_DIM
Q_BLOCK = 128
REL_BUCKETS = 32
REL_MAX_DIST = 128
SSM_EXPAND = 2
SSM_INNER = SSM_EXPAND * D_MODEL
SSM_HEAD_DIM = 64
SSM_HEADS = SSM_INNER // SSM_HEAD_DIM
SSM_GROUPS = 8
SSM_HEADS_PER_GROUP = SSM_HEADS // SSM_GROUPS
SSM_STATE = 128
SSM_CONV = 4
SSM_CHUNK = 128
SSM_CONV_DIM = SSM_INNER + 2 * SSM_GROUPS * SSM_STATE
FFN_DIM = ((8 * D_MODEL // 3 + 127) // 128) * 128
FFN_CONV = 3
N_BRANCHES = 2
IN_PROJ_SIZES = (ATTN_QK, ATTN_QK, ATTN_V, SSM_INNER, SSM_CONV_DIM, 2 * SSM_HEADS, N_BRANCHES * D_MODEL)
IN_PROJ_DIM = 2 * ATTN_QK + ATTN_V + SSM_INNER + SSM_CONV_DIM + 2 * SSM_HEADS + N_BRANCHES * D_MODEL
RMS_EPS = 1e-6

kernel_name = "hybrid_diffattn_bissd_convglu"


def rms_norm(x, w):
    xf = x.astype(jnp.float32)
    y = xf * lax.rsqrt(jnp.mean(xf * xf, axis=-1, keepdims=True) + RMS_EPS)
    return (y * w.astype(jnp.float32)).astype(x.dtype)


def split_columns(t, sizes):
    idx = np.cumsum(np.array(sizes[:-1])).tolist()
    return jnp.split(t, idx, axis=-1)


def dwconv_centred(u, w, b):
    width = w.shape[0]
    out = lax.conv_general_dilated(
        u, w[:, None, :].astype(u.dtype), window_strides=(1,),
        padding=[(width // 2, width - 1 - width // 2)],
        dimension_numbers=('NWC', 'WIO', 'NWC'), feature_group_count=u.shape[-1])
    return out + b.astype(u.dtype)


def t5_bucket(rel):
    half = REL_BUCKETS // 2
    max_exact = half // 2
    bucket = jnp.where(rel > 0, half, 0)
    n = jnp.abs(rel)
    nf = jnp.maximum(n, 1).astype(jnp.float32)
    large = max_exact + (jnp.log(nf / max_exact) / math.log(REL_MAX_DIST / max_exact)
                         * (half - max_exact)).astype(jnp.int32)
    large = jnp.minimum(large, half - 1)
    return bucket + jnp.where(n < max_exact, n, large)


def diff_attention(q, k, v, q_norm_w, k_norm_w, rel_bias, lam, lambda_init, subln_w):
    b, s, _ = q.shape
    q = rms_norm(q.reshape(b, s, ATTN_HEADS, 2, ATTN_HEAD_DIM), q_norm_w) * (ATTN_HEAD_DIM ** -0.5)
    k = rms_norm(k.reshape(b, s, ATTN_HEADS, 2, ATTN_HEAD_DIM), k_norm_w)
    v = v.reshape(b, s, ATTN_HEADS, 2 * ATTN_HEAD_DIM)
    nblk = s // Q_BLOCK
    q_blocks = jnp.moveaxis(q.reshape(b, nblk, Q_BLOCK, ATTN_HEADS, 2, ATTN_HEAD_DIM), 1, 0)
    kpos = jnp.arange(s, dtype=jnp.int32)
    table = rel_bias.astype(jnp.float32)

    def block(args):
        qb, start = args
        logits = jnp.einsum('bqhmd,bkhmd->bmhqk', qb, k, preferred_element_type=jnp.float32)
        qpos = start + jnp.arange(Q_BLOCK, dtype=jnp.int32)
        bias = table[t5_bucket(kpos[None, :] - qpos[:, None])]
        logits = logits + jnp.transpose(bias, (2, 0, 1))[None, None]
        p = jax.nn.softmax(logits, axis=-1)
        a = p[:, 0] - lam * p[:, 1]
        return jnp.einsum('bhqk,bkhe->bqhe', a.astype(v.dtype), v)

    starts = jnp.arange(nblk, dtype=jnp.int32) * Q_BLOCK
    o = lax.map(block, (q_blocks, starts))
    o = jnp.moveaxis(o, 0, 1).reshape(b, s, ATTN_HEADS, 2 * ATTN_HEAD_DIM)
    o = rms_norm(o, subln_w) * (1.0 - lambda_init)
    return o.reshape(b, s, ATTN_V)


def ssd_scan(xs, dt, a, bm, cm):
    b, s = xs.shape[:2]
    nc = s // SSM_CHUNK
    da = dt * a
    xdt = xs * dt[..., None]

    def to_chunks(t):
        return jnp.moveaxis(t.reshape(b, nc, SSM_CHUNK, *t.shape[2:]), 1, 0)

    mask = jnp.tril(jnp.ones((SSM_CHUNK, SSM_CHUNK), dtype=bool))[None, :, :, None, None]

    def step(state, inp):
        xc, dac, bc, cc = inp
        acs = jnp.cumsum(dac, axis=1)
        seg = acs[:, :, None] - acs[:, None, :]
        decay = jnp.exp(jnp.where(mask, seg, -jnp.inf))
        cb = jnp.einsum('blgn,bsgn->bgls', cc, bc)
        y = jnp.einsum('bgls,blsgr,bsgrp->blgrp', cb, decay, xc)
        y = y + jnp.einsum('blgn,bgrpn,blgr->blgrp', cc, state, jnp.exp(acs))
        to_end = jnp.exp(acs[:, -1:] - acs)
        state = state * jnp.exp(acs[:, -1])[..., None, None] + jnp.einsum('blgn,blgr,blgrp->bgrpn', bc, to_end, xc)
        return state, y

    init = jnp.zeros((b, SSM_GROUPS, SSM_HEADS_PER_GROUP, SSM_HEAD_DIM, SSM_STATE), jnp.float32)
    _, y = lax.scan(step, init, (to_chunks(xdt), to_chunks(da), to_chunks(bm), to_chunks(cm)))
    return jnp.moveaxis(y, 0, 1).reshape(xs.shape)


def bi_ssd(z, xbc, dt_raw, conv_w, conv_b, dt_bias_f, a_log_f, dt_bias_b, a_log_b, d_skip, norm_w):
    b, s, _ = z.shape
    gr = (SSM_GROUPS, SSM_HEADS_PER_GROUP)
    xbc = jax.nn.silu(dwconv_centred(xbc, conv_w, conv_b))
    xs, bm, cm = split_columns(xbc, (SSM_INNER, SSM_GROUPS * SSM_STATE, SSM_GROUPS * SSM_STATE))
    xs = xs.reshape(b, s, *gr, SSM_HEAD_DIM).astype(jnp.float32)
    bm = bm.reshape(b, s, SSM_GROUPS, SSM_STATE).astype(jnp.float32)
    cm = cm.reshape(b, s, SSM_GROUPS, SSM_STATE).astype(jnp.float32)
    dt_raw = dt_raw.astype(jnp.float32)
    dt_f = jax.nn.softplus(dt_raw[..., :SSM_HEADS] + dt_bias_f.astype(jnp.float32)).reshape(b, s, *gr)
    dt_b = jax.nn.softplus(dt_raw[..., SSM_HEADS:] + dt_bias_b.astype(jnp.float32)).reshape(b, s, *gr)
    a_f = -jnp.exp(a_log_f.astype(jnp.float32)).reshape(gr)
    a_b = -jnp.exp(a_log_b.astype(jnp.float32)).reshape(gr)
    y_f = ssd_scan(xs, dt_f, a_f, bm, cm)
    y_b = jnp.flip(ssd_scan(jnp.flip(xs, 1), jnp.flip(dt_b, 1), a_b, jnp.flip(bm, 1), jnp.flip(cm, 1)), 1)
    y = y_f + y_b + d_skip.astype(jnp.float32).reshape(gr)[..., None] * xs
    y = y.reshape(b, s, SSM_INNER) * jax.nn.silu(z.astype(jnp.float32))
    y = rms_norm(y.reshape(b, s, SSM_GROUPS, SSM_INNER // SSM_GROUPS), norm_w.reshape(SSM_GROUPS, -1))
    return y.reshape(b, s, SSM_INNER).astype(z.dtype)


def conv_glu(h, w_up, conv_w, conv_b, w_down):
    u = dwconv_centred(h @ w_up, conv_w, conv_b)
    gate, val = jnp.split(u, 2, axis=-1)
    return (jax.nn.silu(gate) * val) @ w_down


def setup_inputs(seed: int = 0) -> dict:
    key = jax.random.key(seed)
    ks = jax.random.split(key, 32)
    f32 = jnp.float32
    L = DEPTH

    def nrm(k, shape, scale):
        return jax.random.normal(k, shape, f32) * scale

    def gain(k, shape):
        return 1.0 + 0.02 * jax.random.normal(k, shape, f32)

    def dt_bias(k):
        u = jax.random.uniform(k, (L, SSM_HEADS), f32)
        dt = jnp.exp(u * (math.log(0.1) - math.log(0.001)) + math.log(0.001))
        return dt + jnp.log(-jnp.expm1(-dt))

    def a_log(k):
        return jnp.log(jax.random.uniform(k, (L, SSM_HEADS), f32, 1.0, 16.0))

    return {
        "x": jax.random.normal(ks[0], (BATCH, SEQ, D_MODEL), f32),
        "norm1_w": gain(ks[1], (L, D_MODEL)),
        "w_in": nrm(ks[2], (L, D_MODEL, IN_PROJ_DIM), D_MODEL ** -0.5),
        "q_norm_w": gain(ks[3], (L, ATTN_HEAD_DIM)),
        "k_norm_w": gain(ks[4], (L, ATTN_HEAD_DIM)),
        "rel_bias": nrm(ks[5], (REL_BUCKETS, ATTN_HEADS), 0.5),
        "lambda_q1": nrm(ks[6], (L, ATTN_HEAD_DIM), 0.1),
        "lambda_k1": nrm(ks[7], (L, ATTN_HEAD_DIM), 0.1),
        "lambda_q2": nrm(ks[8], (L, ATTN_HEAD_DIM), 0.1),
        "lambda_k2": nrm(ks[9], (L, ATTN_HEAD_DIM), 0.1),
        "subln_w": gain(ks[10], (L, 2 * ATTN_HEAD_DIM)),
        "w_attn_out": nrm(ks[11], (L, ATTN_V, D_MODEL), ATTN_V ** -0.5),
        "ssm_conv_w": nrm(ks[12], (L, SSM_CONV, SSM_CONV_DIM), SSM_CONV ** -0.5),
        "ssm_conv_b": nrm(ks[13], (L, SSM_CONV_DIM), 0.01),
        "dt_bias_f": dt_bias(ks[14]),
        "a_log_f": a_log(ks[15]),
        "dt_bias_b": dt_bias(ks[16]),
        "a_log_b": a_log(ks[17]),
        "d_skip": gain(ks[18], (L, SSM_HEADS)),
        "ssm_norm_w": gain(ks[19], (L, SSM_INNER)),
        "w_ssm_out": nrm(ks[20], (L, SSM_INNER, D_MODEL), SSM_INNER ** -0.5),
        "w_out": nrm(ks[21], (L, D_MODEL, D_MODEL), D_MODEL ** -0.5),
        "norm2_w": gain(ks[22], (L, D_MODEL)),
        "w_ffn_up": nrm(ks[23], (L, D_MODEL, 2 * FFN_DIM), D_MODEL ** -0.5),
        "ffn_conv_w": nrm(ks[24], (L, FFN_CONV, 2 * FFN_DIM), FFN_CONV ** -0.5),
        "ffn_conv_b": nrm(ks[25], (L, 2 * FFN_DIM), 0.01),
        "w_ffn_down": nrm(ks[26], (L, FFN_DIM, D_MODEL), FFN_DIM ** -0.5),
    }


def reference(x, norm1_w, w_in, q_norm_w, k_norm_w, rel_bias, lambda_q1, lambda_k1, lambda_q2, lambda_k2,
              subln_w, w_attn_out, ssm_conv_w, ssm_conv_b, dt_bias_f, a_log_f, dt_bias_b, a_log_b, d_skip,
              ssm_norm_w, w_ssm_out, w_out, norm2_w, w_ffn_up, ffn_conv_w, ffn_conv_b, w_ffn_down):
    for layer in range(DEPTH):
        lambda_init = 0.8 - 0.6 * math.exp(-0.3 * layer)
        h = rms_norm(x, norm1_w[layer])
        proj = h @ w_in[layer]
        q, k, v, z, xbc, dt_raw, gate_logits = split_columns(proj, IN_PROJ_SIZES)
        lam = (jnp.exp(jnp.sum(lambda_q1[layer].astype(jnp.float32) * lambda_k1[layer].astype(jnp.float32)))
               - jnp.exp(jnp.sum(lambda_q2[layer].astype(jnp.float32) * lambda_k2[layer].astype(jnp.float32)))
               + lambda_init)
        attn = diff_attention(q, k, v, q_norm_w[layer], k_norm_w[layer], rel_bias, lam, lambda_init,
                              subln_w[layer]) @ w_attn_out[layer]
        ssd = bi_ssd(z, xbc, dt_raw, ssm_conv_w[layer], ssm_conv_b[layer], dt_bias_f[layer], a_log_f[layer],
                     dt_bias_b[layer], a_log_b[layer], d_skip[layer], ssm_norm_w[layer]) @ w_ssm_out[layer]
        gate_attn, gate_ssd = jnp.split(gate_logits, N_BRANCHES, axis=-1)
        mixed = jax.nn.sigmoid(gate_attn) * attn + jax.nn.sigmoid(gate_ssd) * ssd
        x = x + mixed @ w_out[layer]
        x = x + conv_glu(rms_norm(x, norm2_w[layer]), w_ffn_up[layer], ffn_conv_w[layer], ffn_conv_b[layer],
                         w_ffn_down[layer])
    return x
```

```python
import functools
import math

import jax
import jax.numpy as jnp
from jax import lax
from jax.experimental import pallas as pl
from jax.experimental.pallas import tpu as pltpu

F32 = jnp.float32
BF16 = jnp.bfloat16

HEADS = 8
HEAD_DIM = 64
SSM_HEADS = 32
SSM_HEAD_DIM = 64
SSM_GROUPS = 8
SSM_STATE = 128
SSM_INNER = SSM_HEADS * SSM_HEAD_DIM
GROUP_W = SSM_INNER // SSM_GROUPS
CONV_DIM = SSM_INNER + 2 * SSM_GROUPS * SSM_STATE
CHUNK = 128
RMS_EPS = 1e-6
LOG2E = 1.4426950408889634
NEG_BIG = -1e30

REL_THRESHOLDS = (12, 16, 23, 32, 46, 64, 91)
REL_FAR = 129

LANES = 128
VMEM_LIMIT = 52 * 1024 * 1024

COL_XBC = 0
COL_Z = 4096
COL_GATE = 6144
COL_Q = 8192
COL_K = 9216
COL_V = 10240
PROJ_W = 11264
PROJ_TILE = 1024


def _cparams(n_axes):
    return pltpu.CompilerParams(dimension_semantics=("arbitrary",) * n_axes, vmem_limit_bytes=VMEM_LIMIT)


def _rms(x, w):
    return x * lax.rsqrt(jnp.mean(x * x, axis=-1, keepdims=True) + RMS_EPS) * w


def _split_bf16(x):
    hi = x.astype(BF16)
    lo = (x - hi.astype(F32)).astype(BF16)
    return hi, lo


def _silu(x):
    return x / (1.0 + jnp.exp(-x))


def _in_proj_kernel(x_ref, nw_ref, w_ref, wdt_ref, qkw_ref, gsum_ref, proj_ref, dt_ref, h_scr):
    j = pl.program_id(1)

    @pl.when(j == 0)
    def _():
        h = _rms(x_ref[...], nw_ref[...]).astype(BF16)
        h_scr[...] = h
        dt_ref[...] = jnp.dot(h, wdt_ref[...], preferred_element_type=F32)

    acc = jnp.dot(h_scr[...], w_ref[...], preferred_element_type=F32)
    q_tile = COL_Q // PROJ_TILE

    @pl.when(j < q_tile)
    def _():
        proj_ref[...] = acc.astype(BF16)

    @pl.when(j >= q_tile)
    def _():
        jj = j - q_tile
        is_v = jj == 2
        for c in range(PROJ_TILE // 256):
            a = acc[:, c * 256:(c + 1) * 256]
            hi, lo = _split_bf16(a * a)
            ss = (jnp.dot(hi, gsum_ref[...], preferred_element_type=F32)
                  + jnp.dot(lo, gsum_ref[...], preferred_element_type=F32))
            scale = lax.rsqrt(ss * (1.0 / HEAD_DIM) + RMS_EPS)
            scale = jnp.where(is_v, 1.0, scale)
            w = qkw_ref[pl.ds(jj, 1), c * 256:(c + 1) * 256]
            proj_ref[:, c * 256:(c + 1) * 256] = (a * scale * w).astype(BF16)


def _in_proj(x2d, norm_w, w_main, w_dt, qkw, gsum, tm):
    m = x2d.shape[0]
    d = x2d.shape[1]
    nj = PROJ_W // PROJ_TILE
    return pl.pallas_call(
        _in_proj_kernel,
        grid=(m // tm, nj),
        in_specs=[
            pl.BlockSpec((tm, d), lambda i, j: (i, 0)),
            pl.BlockSpec((1, d), lambda i, j: (0, 0)),
            pl.BlockSpec((d, PROJ_TILE), lambda i, j: (0, j)),
            pl.BlockSpec((d, LANES), lambda i, j: (0, 0)),
            pl.BlockSpec((3, PROJ_TILE), lambda i, j: (0, 0)),
            pl.BlockSpec((256, 256), lambda i, j: (0, 0)),
        ],
        out_specs=[
            pl.BlockSpec((tm, PROJ_TILE), lambda i, j: (i, j)),
            pl.BlockSpec((tm, LANES), lambda i, j: (i, 0)),
        ],
        out_shape=[
            jax.ShapeDtypeStruct((m, PROJ_W), BF16),
            jax.ShapeDtypeStruct((m, LANES), F32),
        ],
        scratch_shapes=[pltpu.VMEM((tm, d), BF16)],
        compiler_params=_cparams(2),
        name="in_proj",
    )(x2d, norm_w, w_main, w_dt, qkw, gsum)


def _bias_tiles_kernel(table_ref, out_ref):
    h = pl.program_id(0)
    r = lax.broadcasted_iota(jnp.int32, (LANES, LANES), 0)
    c = lax.broadcasted_iota(jnp.int32, (LANES, LANES), 1)
    for d in range(3):
        rel = (d - 1) * LANES + c - r
        n = jnp.abs(rel)
        large = jnp.full((LANES, LANES), 8, jnp.int32)
        for t in REL_THRESHOLDS:
            large = large + jnp.where(n >= t, 1, 0)
        bucket = jnp.where(rel > 0, 16, 0) + jnp.where(n < 8, n, large)
        val = jnp.zeros((LANES, LANES), F32)
        for bkt in range(32):
            val = jnp.where(bucket == bkt, table_ref[bkt, h], val)
        out_ref[0, d] = val * LOG2E


def _bias_tiles(rel_bias):
    return pl.pallas_call(
        _bias_tiles_kernel,
        grid=(HEADS,),
        in_specs=[pl.BlockSpec(memory_space=pltpu.SMEM)],
        out_specs=pl.BlockSpec((1, 3, LANES, LANES), lambda h: (h, 0, 0, 0)),
        out_shape=jax.ShapeDtypeStruct((HEADS, 3, LANES, LANES), F32),
        compiler_params=_cparams(1),
        name="bias_tiles",
    )(rel_bias)


def _attn_kernel(lam_ref, table_ref, q_ref, k_ref, v_ref, bsm_ref, subln_ref, o_ref,
                 qs_scr, m_scr, acc_scr, bias_scr, *, tile, n_tiles, out_scale):
    t = tile
    nb = t // LANES
    h = pl.program_id(0)
    b = pl.program_id(1)
    i = pl.program_id(2)
    c_lo = table_ref[15, h] * LOG2E
    c_hi = table_ref[31, h] * LOG2E

    @pl.when((b == 0) & (i == 0))
    def _():
        for d in range(3):
            for rb in range(nb):
                for cb in range(nb):
                    off = (d - 1) * nb + cb - rb
                    if abs(off) <= 1:
                        blk = bsm_ref[0, off + 1]
                    else:
                        blk = jnp.full((LANES, LANES), 1.0, F32) * (c_lo if off < 0 else c_hi)
                    bias_scr[d, rb * LANES:(rb + 1) * LANES, cb * LANES:(cb + 1) * LANES] = blk

    q = q_ref[...]
    lane = lax.broadcasted_iota(jnp.int32, (t, LANES), 1)
    zero = jnp.zeros((t, LANES), BF16)
    qs_scr[0:t, :] = jnp.where(lane < HEAD_DIM, q, zero)
    qs_scr[t:2 * t, :] = jnp.where(lane >= HEAD_DIM, q, zero)
    m_scr[...] = jnp.full((2 * t, LANES), NEG_BIG, F32)
    acc_scr[...] = jnp.zeros((2 * t, 2 * LANES), F32)
    ones_blk = jnp.ones((t, LANES), BF16)

    def step(j, near, shift):
        off = pl.multiple_of(j * t, t)
        k = k_ref[pl.ds(off, t), :]
        v = v_ref[pl.ds(off, t), :]
        s = lax.dot_general(qs_scr[...], k, (((1,), (1,)), ((), ())), preferred_element_type=F32)
        if near is not None:
            bt = bias_scr[near]
            s = jnp.concatenate([s[0:t] + bt, s[t:2 * t] + bt], axis=0)
        m_prev = m_scr[...]
        m_cur = jnp.max(s, axis=1, keepdims=True) + shift
        m_next = jnp.maximum(m_prev, m_cur)
        alpha = jnp.exp2(m_prev - m_next)
        p = jnp.exp2(s - pltpu.repeat(m_next - shift, nb, axis=1))
        vaug = jnp.concatenate([v, ones_blk], axis=1)
        pv = jnp.dot(p.astype(BF16), vaug, preferred_element_type=F32)
        acc_scr[...] = acc_scr[...] * pltpu.repeat(alpha, 2, axis=1) + pv
        m_scr[...] = m_next

    def far_lo(j, carry):
        step(j, None, c_lo)
        return carry

    def far_hi(j, carry):
        step(j, None, c_hi)
        return carry

    lax.fori_loop(0, jnp.maximum(i - 1, 0), far_lo, 0)

    @pl.when(i >= 1)
    def _():
        step(i - 1, 0, 0.0)

    step(i, 1, 0.0)

    @pl.when(i + 1 < n_tiles)
    def _():
        step(i + 1, 2, 0.0)

    lax.fori_loop(i + 2, n_tiles, far_hi, 0)

    acc = acc_scr[...]
    o12 = acc[:, 0:LANES] / acc[:, LANES:2 * LANES]
    o = o12[0:t] - lam_ref[0] * o12[t:2 * t]
    o_ref[...] = (_rms(o, subln_ref[...]) * out_scale).astype(BF16)


def _diff_attn(proj, bias_small, rel_bias, lam, subln_w, batch, seq, tile, out_scale):
    n_tiles = seq // tile
    qb, kb, vb = COL_Q // LANES, COL_K // LANES, COL_V // LANES
    kern = functools.partial(_attn_kernel, tile=tile, n_tiles=n_tiles, out_scale=out_scale)
    return pl.pallas_call(
        kern,
        grid=(HEADS, batch, n_tiles),
        in_specs=[
            pl.BlockSpec(memory_space=pltpu.SMEM),
            pl.BlockSpec(memory_space=pltpu.SMEM),
            pl.BlockSpec((tile, LANES), lambda h, b, i: (b * n_tiles + i, qb + h)),
            pl.BlockSpec((seq, LANES), lambda h, b, i: (b, kb + h)),
            pl.BlockSpec((seq, LANES), lambda h, b, i: (b, vb + h)),
            pl.BlockSpec((1, 3, LANES, LANES), lambda h, b, i: (h, 0, 0, 0)),
            pl.BlockSpec((1, LANES), lambda h, b, i: (0, 0)),
        ],
        out_specs=pl.BlockSpec((tile, LANES), lambda h, b, i: (b * n_tiles + i, h)),
        out_shape=jax.ShapeDtypeStruct((batch * seq, HEADS * LANES), BF16),
        scratch_shapes=[
            pltpu.VMEM((2 * tile, LANES), BF16),
            pltpu.VMEM((2 * tile, LANES), F32),
            pltpu.VMEM((2 * tile, 2 * LANES), F32),
            pltpu.VMEM((3, tile, tile), F32),
        ],
        compiler_params=_cparams(3),
        name="diff_attn",
    )(lam, rel_bias, proj, proj, proj, bias_small, subln_w)


def _dt_and_da(dt_ref, dtb_ref, alog_ref):
    raw = dt_ref[...] + dtb_ref[...]
    dt = jnp.maximum(raw, 0.0) + jnp.log1p(jnp.exp(-jnp.abs(raw)))
    a = -jnp.exp(alog_ref[...])
    lane = lax.broadcasted_iota(jnp.int32, dt.shape, 1)
    dt = jnp.where(lane < 2 * SSM_HEADS, dt, 0.0)
    return dt, dt * a


def _tri(upper):
    r = lax.broadcasted_iota(jnp.int32, (CHUNK, CHUNK), 0)
    c = lax.broadcasted_iota(jnp.int32, (CHUNK, CHUNK), 1)
    return (c >= r) if upper else (c <= r)


def _tri_dot(mask, x):
    m = jnp.where(mask, 1.0, 0.0).astype(BF16)
    hi, lo = _split_bf16(x)
    return jnp.dot(m, hi, preferred_element_type=F32) + jnp.dot(m, lo, preferred_element_type=F32)


def _expand(x, e2_ref):
    hi, lo = _split_bf16(x)
    return jnp.dot(jnp.concatenate([hi, lo], axis=1), e2_ref[...], preferred_element_type=F32)


def _shift_rows(u, halo, offset):
    n = u.shape[0]
    if offset == 0:
        return u
    rolled = pltpu.roll(u, (-offset) % n, 0)
    row8 = lax.broadcasted_iota(jnp.int32, (8, u.shape[1]), 0)
    if offset < 0:
        hfix = pltpu.roll(halo, (-offset) % 8, 0)
        first = jnp.where(row8 < -offset, hfix, rolled[0:8])
        return jnp.concatenate([first, rolled[8:n]], axis=0)
    hfix = pltpu.roll(halo, (8 - offset) % 8, 0)
    last = jnp.where(row8 >= 8 - offset, hfix, rolled[n - 8:n])
    return jnp.concatenate([rolled[0:n - 8], last], axis=0)


def _ssd_bwd_kernel(xc_ref, xp_ref, xn_ref, dt_ref, cw_ref, cb_ref, dtb_ref, alog_ref, e2b_ref,
                    xact_ref, ybi_ref, state_scr, *, n_chunks):
    c = pl.program_id(1)
    cr = n_chunks - 1 - c

    @pl.when(c == 0)
    def _():
        state_scr[...] = jnp.zeros(state_scr.shape, F32)

    slab = 512
    keep_prev = jnp.where(cr > 0, 1.0, 0.0)
    keep_next = jnp.where(cr < n_chunks - 1, 1.0, 0.0)
    for sl in range(CONV_DIM // slab):
        cs = slice(sl * slab, (sl + 1) * slab)
        u = xc_ref[:, cs].astype(F32)
        hp = xp_ref[8:16, cs].astype(F32) * keep_prev
        hn = xn_ref[0:8, cs].astype(F32) * keep_next
        acc = cb_ref[:, cs] + _shift_rows(u, hp, -2) * cw_ref[0:1, cs]
        acc = acc + _shift_rows(u, hp, -1) * cw_ref[1:2, cs]
        acc = acc + u * cw_ref[2:3, cs]
        acc = acc + _shift_rows(u, hn, 1) * cw_ref[3:4, cs]
        xact_ref[:, cs] = _silu(acc).astype(BF16)

    dt, da = _dt_and_da(dt_ref, dtb_ref, alog_ref)
    rcs = _tri_dot(_tri(True), da)
    rcs_full = _expand(rcs, e2b_ref)
    dt_full = _expand(dt, e2b_ref)
    decay = jnp.exp(rcs_full)
    rcs0 = rcs_full[0:1, :]
    chunk_decay = jnp.exp(rcs0)
    xs = xact_ref[:, 0:SSM_INNER].astype(F32)
    xw = (xs * (jnp.exp(rcs0 - rcs_full) * dt_full)).astype(BF16)
    for g in range(SSM_GROUPS):
        gs = slice(g * GROUP_W, (g + 1) * GROUP_W)
        bg = xact_ref[:, SSM_INNER + g * SSM_STATE:SSM_INNER + (g + 1) * SSM_STATE]
        cg = xact_ref[:, SSM_INNER + (SSM_GROUPS + g) * SSM_STATE:SSM_INNER + (SSM_GROUPS + g + 1) * SSM_STATE]
        st = state_scr[:, gs]
        yb = jnp.dot(cg, st.astype(BF16), preferred_element_type=F32) * decay[:, gs]
        ybi_ref[:, gs] = yb.astype(BF16)
        upd = lax.dot_general(bg, xw[:, gs], (((0,), (0,)), ((), ())), preferred_element_type=F32)
        state_scr[:, gs] = st * chunk_decay[:, gs] + upd


def _ssd_bwd(proj, dt_raw, conv_w, conv_b, dt_bias, a_log, e2b, batch, seq):
    n_chunks = seq // CHUNK
    sub = CHUNK // 16
    last16 = batch * seq // 16 - 1
    kern = functools.partial(_ssd_bwd_kernel, n_chunks=n_chunks)

    def cur(b, c):
        return b * n_chunks + (n_chunks - 1 - c)

    return pl.pallas_call(
        kern,
        grid=(batch, n_chunks),
        in_specs=[
            pl.BlockSpec((CHUNK, CONV_DIM), lambda b, c: (cur(b, c), 0)),
            pl.BlockSpec((16, CONV_DIM), lambda b, c: (jnp.maximum(cur(b, c) * sub - 1, 0), 0)),
            pl.BlockSpec((16, CONV_DIM), lambda b, c: (jnp.minimum((cur(b, c) + 1) * sub, last16), 0)),
            pl.BlockSpec((CHUNK, LANES), lambda b, c: (cur(b, c), 0)),
            pl.BlockSpec((4, CONV_DIM), lambda b, c: (0, 0)),
            pl.BlockSpec((1, CONV_DIM), lambda b, c: (0, 0)),
            pl.BlockSpec((1, LANES), lambda b, c: (0, 0)),
            pl.BlockSpec((1, LANES), lambda b, c: (0, 0)),
            pl.BlockSpec((2 * LANES, SSM_INNER), lambda b, c: (0, 0)),
        ],
        out_specs=[
            pl.BlockSpec((CHUNK, CONV_DIM), lambda b, c: (cur(b, c), 0)),
            pl.BlockSpec((CHUNK, SSM_INNER), lambda b, c: (cur(b, c), 0)),
        ],
        out_shape=[
            jax.ShapeDtypeStruct((batch * seq, CONV_DIM), BF16),
            jax.ShapeDtypeStruct((batch * seq, SSM_INNER), BF16),
        ],
        scratch_shapes=[pltpu.VMEM((SSM_STATE, SSM_INNER), F32)],
        compiler_params=_cparams(2),
        name="ssd_bwd",
    )(proj, proj, proj, dt_raw, conv_w, conv_b, dt_bias, a_log, e2b)


def _ssd_fwd_kernel(xact_ref, z_ref, dt_ref, ybi_ref, dtb_ref, alog_ref, dsk_ref, nw_ref, e2f_ref,
                    out_ref, state_scr, zt_scr):
    c = pl.program_id(1)

    @pl.when(c == 0)
    def _():
        state_scr[...] = jnp.zeros(state_scr.shape, F32)

    nh = SSM_HEADS
    dt, da = _dt_and_da(dt_ref, dtb_ref, alog_ref)
    tril = _tri(False)
    triu = _tri(True)
    acs = _tri_dot(tril, da)
    rcs = _tri_dot(triu, da)
    lane = lax.broadcasted_iota(jnp.int32, (CHUNK, LANES), 1)
    packed = jnp.where(lane < 2 * nh, dt, pltpu.roll(jnp.where(lane < nh, acs, rcs), 2 * nh, 1))
    zt_scr[...] = packed.T

    acs_full = _expand(acs, e2f_ref)
    dt_full = _expand(dt, e2f_ref)
    decay = jnp.exp(acs_full)
    acs_last = acs_full[CHUNK - 1:CHUNK, :]
    chunk_decay = jnp.exp(acs_last)
    xs_bf = xact_ref[:, 0:SSM_INNER]
    xs = xs_bf.astype(F32)
    xw = (xs * (jnp.exp(acs_last - acs_full) * dt_full)).astype(BF16)
    lane_w = lax.broadcasted_iota(jnp.int32, (CHUNK, LANES), 1)

    for g in range(SSM_GROUPS):
        gs = slice(g * GROUP_W, (g + 1) * GROUP_W)
        bg = xact_ref[:, SSM_INNER + g * SSM_STATE:SSM_INNER + (g + 1) * SSM_STATE]
        cg = xact_ref[:, SSM_INNER + (SSM_GROUPS + g) * SSM_STATE:SSM_INNER + (SSM_GROUPS + g + 1) * SSM_STATE]
        cb = lax.dot_general(cg, bg, (((1,), (1,)), ((), ())), preferred_element_type=F32)
        ys = []
        for r in range(GROUP_W // SSM_HEAD_DIM):
            h = g * (GROUP_W // SSM_HEAD_DIM) + r
            seg_f = acs[:, h:h + 1] - zt_scr[2 * nh + h:2 * nh + h + 1, :]
            w_f = jnp.exp(jnp.where(tril, seg_f, -jnp.inf)) * zt_scr[h:h + 1, :]
            seg_b = rcs[:, nh + h:nh + h + 1] - zt_scr[3 * nh + h:3 * nh + h + 1, :]
            w_b = jnp.exp(jnp.where(triu, seg_b, -jnp.inf)) * zt_scr[nh + h:nh + h + 1, :]
            mat = (cb * (w_f + w_b)).astype(BF16)
            pair = xs_bf[:, (h // 2) * LANES:(h // 2 + 1) * LANES]
            ys.append(jnp.dot(mat, pair, preferred_element_type=F32))
        y = jnp.concatenate([jnp.where(lane_w < SSM_HEAD_DIM, ys[0], ys[1]),
                             jnp.where(lane_w < SSM_HEAD_DIM, ys[2], ys[3])], axis=1)
        st = state_scr[:, gs]
        y = y + jnp.dot(cg, st.astype(BF16), preferred_element_type=F32) * decay[:, gs]
        upd = lax.dot_general(bg, xw[:, gs], (((0,), (0,)), ((), ())), preferred_element_type=F32)
        state_scr[:, gs] = st * chunk_decay[:, gs] + upd
        y = y + ybi_ref[:, gs].astype(F32) + dsk_ref[:, gs] * xs[:, gs]
        y = y * _silu(z_ref[:, gs].astype(F32))
        out_ref[:, gs] = _rms(y, nw_ref[:, gs]).astype(BF16)


def _ssd_fwd(xact, proj, dt_raw, ybi, dt_bias, a_log, d_skip, norm_w, e2f, batch, seq):
    n_chunks = seq // CHUNK
    zb = COL_Z // SSM_INNER
    return pl.pallas_call(
        _ssd_fwd_kernel,
        grid=(batch, n_chunks),
        in_specs=[
            pl.BlockSpec((CHUNK, CONV_DIM), lambda b, c: (b * n_chunks + c, 0)),
            pl.BlockSpec((CHUNK, SSM_INNER), lambda b, c: (b * n_chunks + c, zb)),
            pl.BlockSpec((CHUNK, LANES), lambda b, c: (b * n_chunks + c, 0)),
            pl.BlockSpec((CHUNK, SSM_INNER), lambda b, c: (b * n_chunks + c, 0)),
            pl.BlockSpec((1, LANES), lambda b, c: (0, 0)),
            pl.BlockSpec((1, LANES), lambda b, c: (0, 0)),
            pl.BlockSpec((1, SSM_INNER), lambda b, c: (0, 0)),
            pl.BlockSpec((1, SSM_INNER), lambda b, c: (0, 0)),
            pl.BlockSpec((2 * LANES, SSM_INNER), lambda b, c: (0, 0)),
        ],
        out_specs=pl.BlockSpec((CHUNK, SSM_INNER), lambda b, c: (b * n_chunks + c, 0)),
        out_shape=jax.ShapeDtypeStruct((batch * seq, SSM_INNER), BF16),
        scratch_shapes=[pltpu.VMEM((SSM_STATE, SSM_INNER), F32), pltpu.VMEM((LANES, CHUNK), F32)],
        compiler_params=_cparams(2),
        name="ssd_fwd",
    )(xact, proj, dt_raw, ybi, dt_bias, a_log, d_skip, norm_w, e2f)


def _mix_out_kernel(x_ref, ao_ref, so_ref, ga_ref, gs_ref, wa_ref, ws_ref, wo_ref, n2_ref, x1_ref, h2_ref):
    attn = jnp.dot(ao_ref[...], wa_ref[...], preferred_element_type=F32)
    ssd = jnp.dot(so_ref[...], ws_ref[...], preferred_element_type=F32)
    mixed = (jax.nn.sigmoid(ga_ref[...].astype(F32)) * attn
             + jax.nn.sigmoid(gs_ref[...].astype(F32)) * ssd)
    x1 = x_ref[...] + jnp.dot(mixed.astype(BF16), wo_ref[...], preferred_element_type=F32)
    x1_ref[...] = x1
    h2_ref[...] = _rms(x1, n2_ref[...]).astype(BF16)


def _mix_out(x2d, attn_o, ssd_o, proj, w_attn, w_ssm, w_out, norm2_w, tm):
    m, d = x2d.shape
    gb = COL_GATE // d
    return pl.pallas_call(
        _mix_out_kernel,
        grid=(m // tm,),
        in_specs=[
            pl.BlockSpec((tm, d), lambda i: (i, 0)),
            pl.BlockSpec((tm, attn_o.shape[1]), lambda i: (i, 0)),
            pl.BlockSpec((tm, ssd_o.shape[1]), lambda i: (i, 0)),
            pl.BlockSpec((tm, d), lambda i: (i, gb)),
            pl.BlockSpec((tm, d), lambda i: (i, gb + 1)),
            pl.BlockSpec(w_attn.shape, lambda i: (0, 0)),
            pl.BlockSpec(w_ssm.shape, lambda i: (0, 0)),
            pl.BlockSpec(w_out.shape, lambda i: (0, 0)),
            pl.BlockSpec((1, d), lambda i: (0, 0)),
        ],
        out_specs=[pl.BlockSpec((tm, d), lambda i: (i, 0)), pl.BlockSpec((tm, d), lambda i: (i, 0))],
        out_shape=[jax.ShapeDtypeStruct((m, d), F32), jax.ShapeDtypeStruct((m, d), BF16)],
        compiler_params=_cparams(1),
        name="mix_out",
    )(x2d, attn_o, ssd_o, proj, proj, w_attn, w_ssm, w_out, norm2_w)


FFN_CW = 256


def _ffn_up_kernel(h_ref, w_ref, u_ref):
    acc = jnp.dot(h_ref[...], w_ref[...], preferred_element_type=F32)
    for c in range(u_ref.shape[0]):
        u_ref[c] = acc[:, c * FFN_CW:(c + 1) * FFN_CW].astype(BF16)


def _ffn_up(h2, w_up, tm):
    m, d = h2.shape
    n = w_up.shape[1]
    nj = 2
    tn = n // nj
    cpj = tn // FFN_CW
    return pl.pallas_call(
        _ffn_up_kernel,
        grid=(m // tm, nj),
        in_specs=[pl.BlockSpec((tm, d), lambda i, j: (i, 0)), pl.BlockSpec((d, tn), lambda i, j: (0, j))],
        out_specs=pl.BlockSpec((cpj, tm, FFN_CW), lambda i, j: (j, i, 0)),
        out_shape=jax.ShapeDtypeStruct((n // FFN_CW, m, FFN_CW), BF16),
        compiler_params=_cparams(2),
        name="ffn_up",
    )(h2, w_up)


def _ffn_down_kernel(x1_ref, u_ref, up_ref, un_ref, cw_ref, cb_ref, wd_ref, o_ref, acc_scr, *, tiles_per_seq):
    i = pl.program_id(0)
    nc = wd_ref.shape[0]
    pos = i % tiles_per_seq
    keep_prev = jnp.where(pos > 0, 1.0, 0.0)
    keep_next = jnp.where(pos < tiles_per_seq - 1, 1.0, 0.0)
    acc_scr[...] = x1_ref[...]

    def conv(idx):
        u = u_ref[idx].astype(F32)
        hp = up_ref[idx, 8:16, :].astype(F32) * keep_prev
        hn = un_ref[idx, 0:8, :].astype(F32) * keep_next
        w = cw_ref[idx]
        return (cb_ref[idx] + _shift_rows(u, hp, -1) * w[0:1] + u * w[1:2] + _shift_rows(u, hn, 1) * w[2:3])

    def body(c, carry):
        act = (_silu(conv(c)) * conv(nc + c)).astype(BF16)
        acc_scr[...] += jnp.dot(act, wd_ref[c], preferred_element_type=F32)
        return carry

    lax.fori_loop(0, nc, body, 0)
    o_ref[...] = acc_scr[...]


def _ffn_down(x1, u3, conv_w3, conv_b3, w_down3, seq, tm):
    m, d = x1.shape
    nch = u3.shape[0]
    sub = tm // 16
    last16 = m // 16 - 1
    kern = functools.partial(_ffn_down_kernel, tiles_per_seq=seq // tm)
    return pl.pallas_call(
        kern,
        grid=(m // tm,),
        in_specs=[
            pl.BlockSpec((tm, d), lambda i: (i, 0)),
            pl.BlockSpec((nch, tm, FFN_CW), lambda i: (0, i, 0)),
            pl.BlockSpec((nch, 16, FFN_CW), lambda i: (0, jnp.maximum(i * sub - 1, 0), 0)),
            pl.BlockSpec((nch, 16, FFN_CW), lambda i: (0, jnp.minimum((i + 1) * sub, last16), 0)),
            pl.BlockSpec(conv_w3.shape, lambda i: (0, 0, 0)),
            pl.BlockSpec(conv_b3.shape, lambda i: (0, 0, 0)),
            pl.BlockSpec(w_down3.shape, lambda i: (0, 0, 0)),
        ],
        out_specs=pl.BlockSpec((tm, d), lambda i: (i, 0)),
        out_shape=jax.ShapeDtypeStruct((m, d), F32),
        scratch_shapes=[pltpu.VMEM((tm, d), F32)],
        compiler_params=_cparams(1),
        name="ffn_down",
    )(x1, u3, u3, u3, conv_w3, conv_b3, w_down3)


def _head_expand(first_row):
    rows = jnp.arange(2 * LANES)[:, None] % LANES
    cols = jnp.arange(SSM_INNER)[None, :] // SSM_HEAD_DIM
    return (rows == cols + first_row).astype(BF16)


def _layer(x2d, batch, seq, lambda_init, p):
    d = x2d.shape[1]
    w_in = p["w_in"]
    sizes = (HEADS * 2 * HEAD_DIM,) * 3 + (SSM_INNER, CONV_DIM, 2 * SSM_HEADS, 2 * d)
    offs = [0]
    for s in sizes:
        offs.append(offs[-1] + s)
    wq, wk, wv, wz, wxbc, wdt, wg = (w_in[:, offs[n]:offs[n + 1]] for n in range(7))
    w_main = jnp.concatenate([wxbc, wz, wg, wq, wk, wv], axis=1).astype(BF16)
    w_dt = jnp.pad(wdt, ((0, 0), (0, LANES - wdt.shape[1]))).astype(BF16)
    qkw = jnp.stack([jnp.tile(p["q_norm_w"], 2 * HEADS) * (HEAD_DIM ** -0.5 * LOG2E),
                     jnp.tile(p["k_norm_w"], 2 * HEADS),
                     jnp.ones((PROJ_TILE,), F32)])
    gi = jnp.arange(256) // HEAD_DIM
    gsum = (gi[:, None] == gi[None, :]).astype(BF16)

    proj, dt_raw = _in_proj(x2d, p["norm1_w"][None, :], w_main, w_dt, qkw, gsum, tm=1024)

    lam = (jnp.exp(jnp.sum(p["lambda_q1"] * p["lambda_k1"])) - jnp.exp(jnp.sum(p["lambda_q2"] * p["lambda_k2"]))
           + lambda_init).reshape(1).astype(F32)
    bias_small = _bias_tiles(p["rel_bias"])
    attn_o = _diff_attn(proj, bias_small, p["rel_bias"], lam, p["subln_w"][None, :], batch, seq,
                        tile=512, out_scale=1.0 - lambda_init)

    pad = LANES - 2 * SSM_HEADS
    dt_bias = jnp.pad(jnp.concatenate([p["dt_bias_f"], p["dt_bias_b"]]), (0, pad))[None, :]
    a_log = jnp.pad(jnp.concatenate([p["a_log_f"], p["a_log_b"]]), (0, pad))[None, :]
    xact, ybi = _ssd_bwd(proj, dt_raw, p["ssm_conv_w"], p["ssm_conv_b"][None, :], dt_bias, a_log,
                         _head_expand(SSM_HEADS), batch, seq)
    ssd_o = _ssd_fwd(xact, proj, dt_raw, ybi, dt_bias, a_log,
                     jnp.repeat(p["d_skip"], SSM_HEAD_DIM)[None, :], p["ssm_norm_w"][None, :],
                     _head_expand(0), batch, seq)

    x1, h2 = _mix_out(x2d, attn_o, ssd_o, proj, p["w_attn_out"].astype(BF16), p["w_ssm_out"].astype(BF16),
                      p["w_out"].astype(BF16), p["norm2_w"][None, :], tm=512)

    nch = p["w_ffn_up"].shape[1] // FFN_CW
    u3 = _ffn_up(h2, p["w_ffn_up"].astype(BF16), tm=512)
    conv_w3 = p["ffn_conv_w"].reshape(-1, nch, FFN_CW).transpose(1, 0, 2)
    conv_b3 = p["ffn_conv_b"].reshape(nch, 1, FFN_CW)
    w_down3 = p["w_ffn_down"].astype(BF16).reshape(nch // 2, FFN_CW, d)
    return _ffn_down(x1, u3, conv_w3, conv_b3, w_down3, seq, tm=512)


def kernel(x, norm1_w, w_in, q_norm_w, k_norm_w, rel_bias, lambda_q1, lambda_k1, lambda_q2, lambda_k2, subln_w, w_attn_out, ssm_conv_w, ssm_conv_b, dt_bias_f, a_log_f, dt_bias_b, a_log_b, d_skip, ssm_norm_w, w_ssm_out, w_out, norm2_w, w_ffn_up, ffn_conv_w, ffn_conv_b, w_ffn_down):
    batch, seq, d = x.shape
    layered = dict(norm1_w=norm1_w, w_in=w_in, q_norm_w=q_norm_w, k_norm_w=k_norm_w, lambda_q1=lambda_q1,
                   lambda_k1=lambda_k1, lambda_q2=lambda_q2, lambda_k2=lambda_k2, subln_w=subln_w,
                   w_attn_out=w_attn_out, ssm_conv_w=ssm_conv_w, ssm_conv_b=ssm_conv_b, dt_bias_f=dt_bias_f,
                   a_log_f=a_log_f, dt_bias_b=dt_bias_b, a_log_b=a_log_b, d_skip=d_skip, ssm_norm_w=ssm_norm_w,
                   w_ssm_out=w_ssm_out, w_out=w_out, norm2_w=norm2_w, w_ffn_up=w_ffn_up, ffn_conv_w=ffn_conv_w,
                   ffn_conv_b=ffn_conv_b, w_ffn_down=w_ffn_down)
    x2d = x.reshape(batch * seq, d)
    for layer in range(norm1_w.shape[0]):
        p = {k: v[layer] for k, v in layered.items()}
        p["rel_bias"] = rel_bias
        lambda_init = 0.8 - 0.6 * math.exp(-0.3 * layer)
        x2d = _layer(x2d, batch, seq, lambda_init, p)
    return x2d.reshape(batch, seq, d)
```

```python
import functools
import math

import jax
import jax.numpy as jnp
from jax import lax
from jax.experimental import pallas as pl
from jax.experimental.pallas import tpu as pltpu

F32 = jnp.float32
BF16 = jnp.bfloat16

HEADS = 8
HEAD_DIM = 64
SSM_HEADS = 32
SSM_HEAD_DIM = 64
SSM_GROUPS = 8
SSM_STATE = 128
SSM_INNER = SSM_HEADS * SSM_HEAD_DIM
GROUP_W = SSM_INNER // SSM_GROUPS
CONV_DIM = SSM_INNER + 2 * SSM_GROUPS * SSM_STATE
CHUNK = 128
RMS_EPS = 1e-6
LOG2E = 1.4426950408889634
NEG_BIG = -1e30

REL_THRESHOLDS = (12, 16, 23, 32, 46, 64, 91)
REL_FAR = 129

LANES = 128
VMEM_LIMIT = 52 * 1024 * 1024

COL_XBC = 0
COL_Z = 4096
COL_GATE = 6144
COL_Q = 8192
COL_K = 9216
COL_V = 10240
PROJ_W = 11264
PROJ_TILE = 1024


def _cparams(n_axes):
    return pltpu.CompilerParams(dimension_semantics=("arbitrary",) * n_axes, vmem_limit_bytes=VMEM_LIMIT)


def _rms(x, w):
    return x * lax.rsqrt(jnp.mean(x * x, axis=-1, keepdims=True) + RMS_EPS) * w


def _split_bf16(x):
    hi = x.astype(BF16)
    lo = (x - hi.astype(F32)).astype(BF16)
    return hi, lo


def _silu(x):
    return x / (1.0 + jnp.exp(-x))


def _in_proj_kernel(x_ref, nw_ref, w_ref, wdt_ref, qkw_ref, gsum_ref, proj_ref, dt_ref, h_scr):
    j = pl.program_id(1)

    @pl.when(j == 0)
    def _():
        h = _rms(x_ref[...], nw_ref[...]).astype(BF16)
        h_scr[...] = h
        dt_ref[...] = jnp.dot(h, wdt_ref[...], preferred_element_type=F32)

    acc = jnp.dot(h_scr[...], w_ref[...], preferred_element_type=F32)
    q_tile = COL_Q // PROJ_TILE

    @pl.when(j < q_tile)
    def _():
        proj_ref[...] = acc.astype(BF16)

    @pl.when(j >= q_tile)
    def _():
        jj = j - q_tile
        is_v = jj == 2
        for c in range(PROJ_TILE // 256):
            a = acc[:, c * 256:(c + 1) * 256]
            hi, lo = _split_bf16(a * a)
            ss = (jnp.dot(hi, gsum_ref[...], preferred_element_type=F32)
                  + jnp.dot(lo, gsum_ref[...], preferred_element_type=F32))
            scale = lax.rsqrt(ss * (1.0 / HEAD_DIM) + RMS_EPS)
            scale = jnp.where(is_v, 1.0, scale)
            w = qkw_ref[pl.ds(jj, 1), c * 256:(c + 1) * 256]
            proj_ref[:, c * 256:(c + 1) * 256] = (a * scale * w).astype(BF16)


def _in_proj(x2d, norm_w, w_main, w_dt, qkw, gsum, tm):
    m = x2d.shape[0]
    d = x2d.shape[1]
    nj = PROJ_W // PROJ_TILE
    return pl.pallas_call(
        _in_proj_kernel,
        grid=(m // tm, nj),
        in_specs=[
            pl.BlockSpec((tm, d), lambda i, j: (i, 0)),
            pl.BlockSpec((1, d), lambda i, j: (0, 0)),
            pl.BlockSpec((d, PROJ_TILE), lambda i, j: (0, j)),
            pl.BlockSpec((d, LANES), lambda i, j: (0, 0)),
            pl.BlockSpec((3, PROJ_TILE), lambda i, j: (0, 0)),
            pl.BlockSpec((256, 256), lambda i, j: (0, 0)),
        ],
        out_specs=[
            pl.BlockSpec((tm, PROJ_TILE), lambda i, j: (i, j)),
            pl.BlockSpec((tm, LANES), lambda i, j: (i, 0)),
        ],
        out_shape=[
            jax.ShapeDtypeStruct((m, PROJ_W), BF16),
            jax.ShapeDtypeStruct((m, LANES), F32),
        ],
        scratch_shapes=[pltpu.VMEM((tm, d), BF16)],
        compiler_params=_cparams(2),
        name="in_proj",
    )(x2d, norm_w, w_main, w_dt, qkw, gsum)


def _bias_tiles_kernel(table_ref, out_ref):
    h = pl.program_id(0)
    r = lax.broadcasted_iota(jnp.int32, (LANES, LANES), 0)
    c = lax.broadcasted_iota(jnp.int32, (LANES, LANES), 1)
    for d in range(3):
        rel = (d - 1) * LANES + c - r
        n = jnp.abs(rel)
        large = jnp.full((LANES, LANES), 8, jnp.int32)
        for t in REL_THRESHOLDS:
            large = large + jnp.where(n >= t, 1, 0)
        bucket = jnp.where(rel > 0, 16, 0) + jnp.where(n < 8, n, large)
        val = jnp.zeros((LANES, LANES), F32)
        for bkt in range(32):
            val = jnp.where(bucket == bkt, table_ref[bkt, h], val)
        out_ref[0, d] = val * LOG2E


def _bias_tiles(rel_bias):
    return pl.pallas_call(
        _bias_tiles_kernel,
        grid=(HEADS,),
        in_specs=[pl.BlockSpec(memory_space=pltpu.SMEM)],
        out_specs=pl.BlockSpec((1, 3, LANES, LANES), lambda h: (h, 0, 0, 0)),
        out_shape=jax.ShapeDtypeStruct((HEADS, 3, LANES, LANES), F32),
        compiler_params=_cparams(1),
        name="bias_tiles",
    )(rel_bias)


def _near_bias_tiles(tq, tk):
    nbq, nbk = tq // LANES, tk // LANES
    table = {}
    for a in range(tk // tq):
        for d in (-1, 0, 1):
            ob = d * nbk - a * nbq
            if ob + nbk - 1 >= -1 and ob - (nbq - 1) <= 1:
                table[(a, d)] = (2 + len(table), ob)
    return table


def _attn_kernel(lam_ref, table_ref, q_ref, k_ref, v_ref, bsm_ref, subln_ref, o_ref,
                 s_scr, p_scr, mcur_scr, alpha_scr, m_scr, acc_scr, bias_scr, *, tq, tk, seq, out_scale):
    nk = seq // tk
    n_stages = (seq // tq) * nk
    ratio = tk // tq
    near = _near_bias_tiles(tq, tk)
    h = pl.program_id(0)
    b = pl.program_id(1)

    @pl.when(b == 0)
    def _():
        c_lo = table_ref[15, h] * LOG2E
        c_hi = table_ref[31, h] * LOG2E
        ones = jnp.ones((tq, tk), F32)
        bias_scr[0] = ones * c_lo
        bias_scr[1] = ones * c_hi
        for idx, ob in near.values():
            for rb in range(tq // LANES):
                for cb in range(tk // LANES):
                    off = ob + cb - rb
                    if abs(off) <= 1:
                        blk = bsm_ref[0, off + 1]
                    else:
                        blk = jnp.ones((LANES, LANES), F32) * (c_lo if off < 0 else c_hi)
                    bias_scr[idx, rb * LANES:(rb + 1) * LANES, cb * LANES:(cb + 1) * LANES] = blk

    m_scr[...] = jnp.full((2 * tq, LANES), NEG_BIG, F32)
    acc_scr[...] = jnp.zeros((2 * tq, 2 * LANES), F32)
    lane = lax.broadcasted_iota(jnp.int32, (tq, LANES), 1)
    zero = jnp.zeros((tq, LANES), BF16)
    ones_blk = jnp.ones((tk, LANES), BF16)

    def tiles_of(stage):
        i = stage // nk
        return i, stage - i * nk

    def bias_kind(i, j):
        c = i // ratio
        a = i - c * ratio
        d = j - c
        kind = jnp.where(d < 0, 0, 1)
        for (a0, d0), (idx, _) in near.items():
            kind = jnp.where((a == a0) & (d == d0), idx, kind)
        return kind

    def scores(stage, slot):
        i, j = tiles_of(stage)
        q = q_ref[pl.ds(pl.multiple_of(i * tq, tq), tq), :]
        qs = jnp.concatenate([jnp.where(lane < HEAD_DIM, q, zero), jnp.where(lane >= HEAD_DIM, q, zero)], axis=0)
        k = k_ref[pl.ds(pl.multiple_of(j * tk, tk), tk), :]
        s = lax.dot_general(qs, k, (((1,), (1,)), ((), ())), preferred_element_type=F32)
        kind = bias_kind(i, j)
        s = jnp.concatenate([s[0:tq] + bias_scr[kind], s[tq:2 * tq] + bias_scr[kind]], axis=0)
        s_scr[slot] = s
        mcur_scr[slot] = jnp.broadcast_to(jnp.max(s, axis=1, keepdims=True), (2 * tq, LANES))

    def softmax(stage, slot):
        _, j = tiles_of(stage)
        m_prev = jnp.where(j == 0, NEG_BIG, m_scr[...])
        m_next = jnp.maximum(m_prev, mcur_scr[slot])
        alpha_scr[slot] = jnp.exp2(m_prev - m_next)
        p_scr[slot] = jnp.exp2(s_scr[slot] - pltpu.repeat(m_next, tk // LANES, axis=1)).astype(BF16)
        m_scr[...] = m_next

    def values(stage, slot):
        _, j = tiles_of(stage)
        vaug = jnp.concatenate([v_ref[pl.ds(pl.multiple_of(j * tk, tk), tk), :], ones_blk], axis=1)
        pv = jnp.dot(p_scr[slot], vaug, preferred_element_type=F32)
        acc_scr[...] = acc_scr[...] * pltpu.repeat(alpha_scr[slot], 2, axis=1) + pv

    def finalize(i):
        acc = acc_scr[...]
        o12 = acc[:, 0:LANES] / acc[:, LANES:2 * LANES]
        o = o12[0:tq] - lam_ref[0] * o12[tq:2 * tq]
        o_ref[pl.ds(pl.multiple_of(i * tq, tq), tq), :] = (_rms(o, subln_ref[...]) * out_scale).astype(BF16)

    scores(0, 0)
    softmax(0, 0)
    scores(1, 1)

    def body(pair, carry):
        s1 = 2 * pair + 1
        values(s1 - 1, 0)
        softmax(s1, 1)
        scores(s1 + 1, 0)
        values(s1, 1)
        softmax(s1 + 1, 0)
        scores(s1 + 2, 1)

        @pl.when(s1 % nk == nk - 1)
        def _():
            finalize(s1 // nk)

        return carry

    lax.fori_loop(0, n_stages // 2 - 1, body, 0)
    values(n_stages - 2, 0)
    softmax(n_stages - 1, 1)
    values(n_stages - 1, 1)
    finalize(seq // tq - 1)


def _diff_attn(proj, bias_small, rel_bias, lam, subln_w, batch, seq, tq, tk, out_scale):
    assert seq % tk == 0 and tk % tq == 0 and (seq // tk) % 2 == 0
    qb, kb, vb = COL_Q // LANES, COL_K // LANES, COL_V // LANES
    kern = functools.partial(_attn_kernel, tq=tq, tk=tk, seq=seq, out_scale=out_scale)
    n_bias = 2 + len(_near_bias_tiles(tq, tk))
    return pl.pallas_call(
        kern,
        grid=(HEADS, batch),
        in_specs=[
            pl.BlockSpec(memory_space=pltpu.SMEM),
            pl.BlockSpec(memory_space=pltpu.SMEM),
            pl.BlockSpec((seq, LANES), lambda h, b: (b, qb + h)),
            pl.BlockSpec((seq, LANES), lambda h, b: (b, kb + h)),
            pl.BlockSpec((seq, LANES), lambda h, b: (b, vb + h)),
            pl.BlockSpec((1, 3, LANES, LANES), lambda h, b: (h, 0, 0, 0)),
            pl.BlockSpec((1, LANES), lambda h, b: (0, 0)),
        ],
        out_specs=pl.BlockSpec((seq, LANES), lambda h, b: (b, h)),
        out_shape=jax.ShapeDtypeStruct((batch * seq, HEADS * LANES), BF16),
        scratch_shapes=[
            pltpu.VMEM((2, 2 * tq, tk), F32),
            pltpu.VMEM((2, 2 * tq, tk), BF16),
            pltpu.VMEM((2, 2 * tq, LANES), F32),
            pltpu.VMEM((2, 2 * tq, LANES), F32),
            pltpu.VMEM((2 * tq, LANES), F32),
            pltpu.VMEM((2 * tq, 2 * LANES), F32),
            pltpu.VMEM((n_bias, tq, tk), F32),
        ],
        compiler_params=_cparams(2),
        name="diff_attn",
    )(lam, rel_bias, proj, proj, proj, bias_small, subln_w)


def _dt_and_da(dt_ref, dtb_ref, alog_ref):
    raw = dt_ref[...] + dtb_ref[...]
    dt = jnp.maximum(raw, 0.0) + jnp.log1p(jnp.exp(-jnp.abs(raw)))
    a = -jnp.exp(alog_ref[...])
    lane = lax.broadcasted_iota(jnp.int32, dt.shape, 1)
    dt = jnp.where(lane < 2 * SSM_HEADS, dt, 0.0)
    return dt, dt * a


def _tri(upper):
    r = lax.broadcasted_iota(jnp.int32, (CHUNK, CHUNK), 0)
    c = lax.broadcasted_iota(jnp.int32, (CHUNK, CHUNK), 1)
    return (c >= r) if upper else (c <= r)


def _tri_dot(mask, x):
    m = jnp.where(mask, 1.0, 0.0).astype(BF16)
    hi, lo = _split_bf16(x)
    return jnp.dot(m, hi, preferred_element_type=F32) + jnp.dot(m, lo, preferred_element_type=F32)


def _expand(x, e2_ref):
    hi, lo = _split_bf16(x)
    return jnp.dot(jnp.concatenate([hi, lo], axis=1), e2_ref[...], preferred_element_type=F32)


def _shift_rows(u, halo, offset):
    n = u.shape[0]
    if offset == 0:
        return u
    rolled = pltpu.roll(u, (-offset) % n, 0)
    row8 = lax.broadcasted_iota(jnp.int32, (8, u.shape[1]), 0)
    if offset < 0:
        hfix = pltpu.roll(halo, (-offset) % 8, 0)
        first = jnp.where(row8 < -offset, hfix, rolled[0:8])
        return jnp.concatenate([first, rolled[8:n]], axis=0)
    hfix = pltpu.roll(halo, (8 - offset) % 8, 0)
    last = jnp.where(row8 >= 8 - offset, hfix, rolled[n - 8:n])
    return jnp.concatenate([rolled[0:n - 8], last], axis=0)


def _ssd_bwd_kernel(xc_ref, xp_ref, xn_ref, dt_ref, cw_ref, cb_ref, dtb_ref, alog_ref, e2b_ref,
                    xact_ref, ybi_ref, state_scr, *, n_chunks):
    c = pl.program_id(1)
    cr = n_chunks - 1 - c

    @pl.when(c == 0)
    def _():
        state_scr[...] = jnp.zeros(state_scr.shape, F32)

    slab = 512
    keep_prev = jnp.where(cr > 0, 1.0, 0.0)
    keep_next = jnp.where(cr < n_chunks - 1, 1.0, 0.0)
    for sl in range(CONV_DIM // slab):
        cs = slice(sl * slab, (sl + 1) * slab)
        u = xc_ref[:, cs].astype(F32)
        hp = xp_ref[8:16, cs].astype(F32) * keep_prev
        hn = xn_ref[0:8, cs].astype(F32) * keep_next
        acc = cb_ref[:, cs] + _shift_rows(u, hp, -2) * cw_ref[0:1, cs]
        acc = acc + _shift_rows(u, hp, -1) * cw_ref[1:2, cs]
        acc = acc + u * cw_ref[2:3, cs]
        acc = acc + _shift_rows(u, hn, 1) * cw_ref[3:4, cs]
        xact_ref[:, cs] = _silu(acc).astype(BF16)

    dt, da = _dt_and_da(dt_ref, dtb_ref, alog_ref)
    rcs = _tri_dot(_tri(True), da)
    rcs_full = _expand(rcs, e2b_ref)
    dt_full = _expand(dt, e2b_ref)
    decay = jnp.exp(rcs_full)
    rcs0 = rcs_full[0:1, :]
    chunk_decay = jnp.exp(rcs0)
    xs = xact_ref[:, 0:SSM_INNER].astype(F32)
    xw = (xs * (jnp.exp(rcs0 - rcs_full) * dt_full)).astype(BF16)
    for g in range(SSM_GROUPS):
        gs = slice(g * GROUP_W, (g + 1) * GROUP_W)
        bg = xact_ref[:, SSM_INNER + g * SSM_STATE:SSM_INNER + (g + 1) * SSM_STATE]
        cg = xact_ref[:, SSM_INNER + (SSM_GROUPS + g) * SSM_STATE:SSM_INNER + (SSM_GROUPS + g + 1) * SSM_STATE]
        st = state_scr[:, gs]
        yb = jnp.dot(cg, st.astype(BF16), preferred_element_type=F32) * decay[:, gs]
        ybi_ref[:, gs] = yb.astype(BF16)
        upd = lax.dot_general(bg, xw[:, gs], (((0,), (0,)), ((), ())), preferred_element_type=F32)
        state_scr[:, gs] = st * chunk_decay[:, gs] + upd


def _ssd_bwd(proj, dt_raw, conv_w, conv_b, dt_bias, a_log, e2b, batch, seq):
    n_chunks = seq // CHUNK
    sub = CHUNK // 16
    last16 = batch * seq // 16 - 1
    kern = functools.partial(_ssd_bwd_kernel, n_chunks=n_chunks)

    def cur(b, c):
        return b * n_chunks + (n_chunks - 1 - c)

    return pl.pallas_call(
        kern,
        grid=(batch, n_chunks),
        in_specs=[
            pl.BlockSpec((CHUNK, CONV_DIM), lambda b, c: (cur(b, c), 0)),
            pl.BlockSpec((16, CONV_DIM), lambda b, c: (jnp.maximum(cur(b, c) * sub - 1, 0), 0)),
            pl.BlockSpec((16, CONV_DIM), lambda b, c: (jnp.minimum((cur(b, c) + 1) * sub, last16), 0)),
            pl.BlockSpec((CHUNK, LANES), lambda b, c: (cur(b, c), 0)),
            pl.BlockSpec((4, CONV_DIM), lambda b, c: (0, 0)),
            pl.BlockSpec((1, CONV_DIM), lambda b, c: (0, 0)),
            pl.BlockSpec((1, LANES), lambda b, c: (0, 0)),
            pl.BlockSpec((1, LANES), lambda b, c: (0, 0)),
            pl.BlockSpec((2 * LANES, SSM_INNER), lambda b, c: (0, 0)),
        ],
        out_specs=[
            pl.BlockSpec((CHUNK, CONV_DIM), lambda b, c: (cur(b, c), 0)),
            pl.BlockSpec((CHUNK, SSM_INNER), lambda b, c: (cur(b, c), 0)),
        ],
        out_shape=[
            jax.ShapeDtypeStruct((batch * seq, CONV_DIM), BF16),
            jax.ShapeDtypeStruct((batch * seq, SSM_INNER), BF16),
        ],
        scratch_shapes=[pltpu.VMEM((SSM_STATE, SSM_INNER), F32)],
        compiler_params=_cparams(2),
        name="ssd_bwd",
    )(proj, proj, proj, dt_raw, conv_w, conv_b, dt_bias, a_log, e2b)


def _ssd_fwd_kernel(xact_ref, z_ref, dt_ref, ybi_ref, dtb_ref, alog_ref, dsk_ref, nw_ref, e2f_ref,
                    out_ref, state_scr, zt_scr):
    c = pl.program_id(1)

    @pl.when(c == 0)
    def _():
        state_scr[...] = jnp.zeros(state_scr.shape, F32)

    nh = SSM_HEADS
    dt, da = _dt_and_da(dt_ref, dtb_ref, alog_ref)
    tril = _tri(False)
    triu = _tri(True)
    acs = _tri_dot(tril, da)
    rcs = _tri_dot(triu, da)
    lane = lax.broadcasted_iota(jnp.int32, (CHUNK, LANES), 1)
    packed = jnp.where(lane < 2 * nh, dt, pltpu.roll(jnp.where(lane < nh, acs, rcs), 2 * nh, 1))
    zt_scr[...] = packed.T

    acs_full = _expand(acs, e2f_ref)
    dt_full = _expand(dt, e2f_ref)
    decay = jnp.exp(acs_full)
    acs_last = acs_full[CHUNK - 1:CHUNK, :]
    chunk_decay = jnp.exp(acs_last)
    xs_bf = xact_ref[:, 0:SSM_INNER]
    xs = xs_bf.astype(F32)
    xw = (xs * (jnp.exp(acs_last - acs_full) * dt_full)).astype(BF16)
    lane_w = lax.broadcasted_iota(jnp.int32, (CHUNK, LANES), 1)

    for g in range(SSM_GROUPS):
        gs = slice(g * GROUP_W, (g + 1) * GROUP_W)
        bg = xact_ref[:, SSM_INNER + g * SSM_STATE:SSM_INNER + (g + 1) * SSM_STATE]
        cg = xact_ref[:, SSM_INNER + (SSM_GROUPS + g) * SSM_STATE:SSM_INNER + (SSM_GROUPS + g + 1) * SSM_STATE]
        cb = lax.dot_general(cg, bg, (((1,), (1,)), ((), ())), preferred_element_type=F32)
        ys = []
        for r in range(GROUP_W // SSM_HEAD_DIM):
            h = g * (GROUP_W // SSM_HEAD_DIM) + r
            seg_f = acs[:, h:h + 1] - zt_scr[2 * nh + h:2 * nh + h + 1, :]
            w_f = jnp.exp(jnp.where(tril, seg_f, -jnp.inf)) * zt_scr[h:h + 1, :]
            seg_b = rcs[:, nh + h:nh + h + 1] - zt_scr[3 * nh + h:3 * nh + h + 1, :]
            w_b = jnp.exp(jnp.where(triu, seg_b, -jnp.inf)) * zt_scr[nh + h:nh + h + 1, :]
            mat = (cb * (w_f + w_b)).astype(BF16)
            pair = xs_bf[:, (h // 2) * LANES:(h // 2 + 1) * LANES]
            ys.append(jnp.dot(mat, pair, preferred_element_type=F32))
        y = jnp.concatenate([jnp.where(lane_w < SSM_HEAD_DIM, ys[0], ys[1]),
                             jnp.where(lane_w < SSM_HEAD_DIM, ys[2], ys[3])], axis=1)
        st = state_scr[:, gs]
        y = y + jnp.dot(cg, st.astype(BF16), preferred_element_type=F32) * decay[:, gs]
        upd = lax.dot_general(bg, xw[:, gs], (((0,), (0,)), ((), ())), preferred_element_type=F32)
        state_scr[:, gs] = st * chunk_decay[:, gs] + upd
        y = y + ybi_ref[:, gs].astype(F32) + dsk_ref[:, gs] * xs[:, gs]
        y = y * _silu(z_ref[:, gs].astype(F32))
        out_ref[:, gs] = _rms(y, nw_ref[:, gs]).astype(BF16)


def _ssd_fwd(xact, proj, dt_raw, ybi, dt_bias, a_log, d_skip, norm_w, e2f, batch, seq):
    n_chunks = seq // CHUNK
    zb = COL_Z // SSM_INNER
    return pl.pallas_call(
        _ssd_fwd_kernel,
        grid=(batch, n_chunks),
        in_specs=[
            pl.BlockSpec((CHUNK, CONV_DIM), lambda b, c: (b * n_chunks + c, 0)),
            pl.BlockSpec((CHUNK, SSM_INNER), lambda b, c: (b * n_chunks + c, zb)),
            pl.BlockSpec((CHUNK, LANES), lambda b, c: (b * n_chunks + c, 0)),
            pl.BlockSpec((CHUNK, SSM_INNER), lambda b, c: (b * n_chunks + c, 0)),
            pl.BlockSpec((1, LANES), lambda b, c: (0, 0)),
            pl.BlockSpec((1, LANES), lambda b, c: (0, 0)),
            pl.BlockSpec((1, SSM_INNER), lambda b, c: (0, 0)),
            pl.BlockSpec((1, SSM_INNER), lambda b, c: (0, 0)),
            pl.BlockSpec((2 * LANES, SSM_INNER), lambda b, c: (0, 0)),
        ],
        out_specs=pl.BlockSpec((CHUNK, SSM_INNER), lambda b, c: (b * n_chunks + c, 0)),
        out_shape=jax.ShapeDtypeStruct((batch * seq, SSM_INNER), BF16),
        scratch_shapes=[pltpu.VMEM((SSM_STATE, SSM_INNER), F32), pltpu.VMEM((LANES, CHUNK), F32)],
        compiler_params=_cparams(2),
        name="ssd_fwd",
    )(xact, proj, dt_raw, ybi, dt_bias, a_log, d_skip, norm_w, e2f)


def _mix_out_kernel(x_ref, ao_ref, so_ref, ga_ref, gs_ref, wa_ref, ws_ref, wo_ref, n2_ref, x1_ref, h2_ref):
    attn = jnp.dot(ao_ref[...], wa_ref[...], preferred_element_type=F32)
    ssd = jnp.dot(so_ref[...], ws_ref[...], preferred_element_type=F32)
    mixed = (jax.nn.sigmoid(ga_ref[...].astype(F32)) * attn
             + jax.nn.sigmoid(gs_ref[...].astype(F32)) * ssd)
    x1 = x_ref[...] + jnp.dot(mixed.astype(BF16), wo_ref[...], preferred_element_type=F32)
    x1_ref[...] = x1
    h2_ref[...] = _rms(x1, n2_ref[...]).astype(BF16)


def _mix_out(x2d, attn_o, ssd_o, proj, w_attn, w_ssm, w_out, norm2_w, tm):
    m, d = x2d.shape
    gb = COL_GATE // d
    return pl.pallas_call(
        _mix_out_kernel,
        grid=(m // tm,),
        in_specs=[
            pl.BlockSpec((tm, d), lambda i: (i, 0)),
            pl.BlockSpec((tm, attn_o.shape[1]), lambda i: (i, 0)),
            pl.BlockSpec((tm, ssd_o.shape[1]), lambda i: (i, 0)),
            pl.BlockSpec((tm, d), lambda i: (i, gb)),
            pl.BlockSpec((tm, d), lambda i: (i, gb + 1)),
            pl.BlockSpec(w_attn.shape, lambda i: (0, 0)),
            pl.BlockSpec(w_ssm.shape, lambda i: (0, 0)),
            pl.BlockSpec(w_out.shape, lambda i: (0, 0)),
            pl.BlockSpec((1, d), lambda i: (0, 0)),
        ],
        out_specs=[pl.BlockSpec((tm, d), lambda i: (i, 0)), pl.BlockSpec((tm, d), lambda i: (i, 0))],
        out_shape=[jax.ShapeDtypeStruct((m, d), F32), jax.ShapeDtypeStruct((m, d), BF16)],
        compiler_params=_cparams(1),
        name="mix_out",
    )(x2d, attn_o, ssd_o, proj, proj, w_attn, w_ssm, w_out, norm2_w)


FFN_CW = 256


def _ffn_up_kernel(h_ref, w_ref, u_ref):
    acc = jnp.dot(h_ref[...], w_ref[...], preferred_element_type=F32)
    for c in range(u_ref.shape[0]):
        u_ref[c] = acc[:, c * FFN_CW:(c + 1) * FFN_CW].astype(BF16)


def _ffn_up(h2, w_up, tm):
    m, d = h2.shape
    n = w_up.shape[1]
    nj = 2
    tn = n // nj
    cpj = tn // FFN_CW
    return pl.pallas_call(
        _ffn_up_kernel,
        grid=(m // tm, nj),
        in_specs=[pl.BlockSpec((tm, d), lambda i, j: (i, 0)), pl.BlockSpec((d, tn), lambda i, j: (0, j))],
        out_specs=pl.BlockSpec((cpj, tm, FFN_CW), lambda i, j: (j, i, 0)),
        out_shape=jax.ShapeDtypeStruct((n // FFN_CW, m, FFN_CW), BF16),
        compiler_params=_cparams(2),
        name="ffn_up",
    )(h2, w_up)


def _ffn_down_kernel(x1_ref, u_ref, up_ref, un_ref, cw_ref, cb_ref, wd_ref, o_ref, acc_scr, *, tiles_per_seq):
    i = pl.program_id(0)
    nc = wd_ref.shape[0]
    pos = i % tiles_per_seq
    keep_prev = jnp.where(pos > 0, 1.0, 0.0)
    keep_next = jnp.where(pos < tiles_per_seq - 1, 1.0, 0.0)
    acc_scr[...] = x1_ref[...]

    def conv(idx):
        u = u_ref[idx].astype(F32)
        hp = up_ref[idx, 8:16, :].astype(F32) * keep_prev
        hn = un_ref[idx, 0:8, :].astype(F32) * keep_next
        w = cw_ref[idx]
        return (cb_ref[idx] + _shift_rows(u, hp, -1) * w[0:1] + u * w[1:2] + _shift_rows(u, hn, 1) * w[2:3])

    def body(c, carry):
        act = (_silu(conv(c)) * conv(nc + c)).astype(BF16)
        acc_scr[...] += jnp.dot(act, wd_ref[c], preferred_element_type=F32)
        return carry

    lax.fori_loop(0, nc, body, 0)
    o_ref[...] = acc_scr[...]


def _ffn_down(x1, u3, conv_w3, conv_b3, w_down3, seq, tm):
    m, d = x1.shape
    nch = u3.shape[0]
    sub = tm // 16
    last16 = m // 16 - 1
    kern = functools.partial(_ffn_down_kernel, tiles_per_seq=seq // tm)
    return pl.pallas_call(
        kern,
        grid=(m // tm,),
        in_specs=[
            pl.BlockSpec((tm, d), lambda i: (i, 0)),
            pl.BlockSpec((nch, tm, FFN_CW), lambda i: (0, i, 0)),
            pl.BlockSpec((nch, 16, FFN_CW), lambda i: (0, jnp.maximum(i * sub - 1, 0), 0)),
            pl.BlockSpec((nch, 16, FFN_CW), lambda i: (0, jnp.minimum((i + 1) * sub, last16), 0)),
            pl.BlockSpec(conv_w3.shape, lambda i: (0, 0, 0)),
            pl.BlockSpec(conv_b3.shape, lambda i: (0, 0, 0)),
            pl.BlockSpec(w_down3.shape, lambda i: (0, 0, 0)),
        ],
        out_specs=pl.BlockSpec((tm, d), lambda i: (i, 0)),
        out_shape=jax.ShapeDtypeStruct((m, d), F32),
        scratch_shapes=[pltpu.VMEM((tm, d), F32)],
        compiler_params=_cparams(1),
        name="ffn_down",
    )(x1, u3, u3, u3, conv_w3, conv_b3, w_down3)


def _head_expand(first_row):
    rows = jnp.arange(2 * LANES)[:, None] % LANES
    cols = jnp.arange(SSM_INNER)[None, :] // SSM_HEAD_DIM
    return (rows == cols + first_row).astype(BF16)


def _layer(x2d, batch, seq, lambda_init, p):
    d = x2d.shape[1]
    w_in = p["w_in"]
    sizes = (HEADS * 2 * HEAD_DIM,) * 3 + (SSM_INNER, CONV_DIM, 2 * SSM_HEADS, 2 * d)
    offs = [0]
    for s in sizes:
        offs.append(offs[-1] + s)
    wq, wk, wv, wz, wxbc, wdt, wg = (w_in[:, offs[n]:offs[n + 1]] for n in range(7))
    w_main = jnp.concatenate([wxbc, wz, wg, wq, wk, wv], axis=1).astype(BF16)
    w_dt = jnp.pad(wdt, ((0, 0), (0, LANES - wdt.shape[1]))).astype(BF16)
    qkw = jnp.stack([jnp.tile(p["q_norm_w"], 2 * HEADS) * (HEAD_DIM ** -0.5 * LOG2E),
                     jnp.tile(p["k_norm_w"], 2 * HEADS),
                     jnp.ones((PROJ_TILE,), F32)])
    gi = jnp.arange(256) // HEAD_DIM
    gsum = (gi[:, None] == gi[None, :]).astype(BF16)

    proj, dt_raw = _in_proj(x2d, p["norm1_w"][None, :], w_main, w_dt, qkw, gsum, tm=1024)

    lam = (jnp.exp(jnp.sum(p["lambda_q1"] * p["lambda_k1"])) - jnp.exp(jnp.sum(p["lambda_q2"] * p["lambda_k2"]))
           + lambda_init).reshape(1).astype(F32)
    bias_small = _bias_tiles(p["rel_bias"])
    attn_o = _diff_attn(proj, bias_small, p["rel_bias"], lam, p["subln_w"][None, :], batch, seq,
                        tq=512, tk=1024, out_scale=1.0 - lambda_init)

    pad = LANES - 2 * SSM_HEADS
    dt_bias = jnp.pad(jnp.concatenate([p["dt_bias_f"], p["dt_bias_b"]]), (0, pad))[None, :]
    a_log = jnp.pad(jnp.concatenate([p["a_log_f"], p["a_log_b"]]), (0, pad))[None, :]
    xact, ybi = _ssd_bwd(proj, dt_raw, p["ssm_conv_w"], p["ssm_conv_b"][None, :], dt_bias, a_log,
                         _head_expand(SSM_HEADS), batch, seq)
    ssd_o = _ssd_fwd(xact, proj, dt_raw, ybi, dt_bias, a_log,
                     jnp.repeat(p["d_skip"], SSM_HEAD_DIM)[None, :], p["ssm_norm_w"][None, :],
                     _head_expand(0), batch, seq)

    x1, h2 = _mix_out(x2d, attn_o, ssd_o, proj, p["w_attn_out"].astype(BF16), p["w_ssm_out"].astype(BF16),
                      p["w_out"].astype(BF16), p["norm2_w"][None, :], tm=512)

    nch = p["w_ffn_up"].shape[1] // FFN_CW
    u3 = _ffn_up(h2, p["w_ffn_up"].astype(BF16), tm=512)
    conv_w3 = p["ffn_conv_w"].reshape(-1, nch, FFN_CW).transpose(1, 0, 2)
    conv_b3 = p["ffn_conv_b"].reshape(nch, 1, FFN_CW)
    w_down3 = p["w_ffn_down"].astype(BF16).reshape(nch // 2, FFN_CW, d)
    return _ffn_down(x1, u3, conv_w3, conv_b3, w_down3, seq, tm=512)


def kernel(x, norm1_w, w_in, q_norm_w, k_norm_w, rel_bias, lambda_q1, lambda_k1, lambda_q2, lambda_k2, subln_w, w_attn_out, ssm_conv_w, ssm_conv_b, dt_bias_f, a_log_f, dt_bias_b, a_log_b, d_skip, ssm_norm_w, w_ssm_out, w_out, norm2_w, w_ffn_up, ffn_conv_w, ffn_conv_b, w_ffn_down):
    batch, seq, d = x.shape
    layered = dict(norm1_w=norm1_w, w_in=w_in, q_norm_w=q_norm_w, k_norm_w=k_norm_w, lambda_q1=lambda_q1,
                   lambda_k1=lambda_k1, lambda_q2=lambda_q2, lambda_k2=lambda_k2, subln_w=subln_w,
                   w_attn_out=w_attn_out, ssm_conv_w=ssm_conv_w, ssm_conv_b=ssm_conv_b, dt_bias_f=dt_bias_f,
                   a_log_f=a_log_f, dt_bias_b=dt_bias_b, a_log_b=a_log_b, d_skip=d_skip, ssm_norm_w=ssm_norm_w,
                   w_ssm_out=w_ssm_out, w_out=w_out, norm2_w=norm2_w, w_ffn_up=w_ffn_up, ffn_conv_w=ffn_conv_w,
                   ffn_conv_b=ffn_conv_b, w_ffn_down=w_ffn_down)
    x2d = x.reshape(batch * seq, d)
    for layer in range(norm1_w.shape[0]):
        p = {k: v[layer] for k, v in layered.items()}
        p["rel_bias"] = rel_bias
        lambda_init = 0.8 - 0.6 * math.exp(-0.3 * layer)
        x2d = _layer(x2d, batch, seq, lambda_init, p)
    return x2d.reshape(batch, seq, d)
```

```python
import functools
import math

import jax
import jax.numpy as jnp
from jax import lax
from jax.experimental import pallas as pl
from jax.experimental.pallas import tpu as pltpu

F32 = jnp.float32
BF16 = jnp.bfloat16

HEADS = 8
HEAD_DIM = 64
SSM_HEADS = 32
SSM_HEAD_DIM = 64
SSM_GROUPS = 8
SSM_STATE = 128
SSM_INNER = SSM_HEADS * SSM_HEAD_DIM
GROUP_W = SSM_INNER // SSM_GROUPS
CONV_DIM = SSM_INNER + 2 * SSM_GROUPS * SSM_STATE
CHUNK = 128
RMS_EPS = 1e-6
LOG2E = 1.4426950408889634
NEG_BIG = -1e30

REL_THRESHOLDS = (12, 16, 23, 32, 46, 64, 91)
REL_FAR = 129

LANES = 128
VMEM_LIMIT = 52 * 1024 * 1024
VMEM_LIMIT_ATTN = 60 * 1024 * 1024

MXU_DIM = 256

COL_XBC = 0
COL_Z = 4096
COL_GATE = 6144
COL_V = 8192
PROJ_W = 9216
PROJ_TILE = 1024


def _cparams(n_axes, vmem_limit=VMEM_LIMIT):
    return pltpu.CompilerParams(dimension_semantics=("arbitrary",) * n_axes, vmem_limit_bytes=vmem_limit)


def _rms(x, w):
    return x * lax.rsqrt(jnp.mean(x * x, axis=-1, keepdims=True) + RMS_EPS) * w


def _split_bf16(x):
    hi = x.astype(BF16)
    lo = (x - hi.astype(F32)).astype(BF16)
    return hi, lo


def _silu(x):
    return x / (1.0 + jnp.exp(-x))


def _in_proj_kernel(x_ref, nw_ref, w_ref, wdt_ref, proj_ref, dt_ref, h_ref):
    @pl.when(pl.program_id(1) == 0)
    def _():
        h = _rms(x_ref[...], nw_ref[...]).astype(BF16)
        h_ref[...] = h
        dt_ref[...] = jnp.dot(h, wdt_ref[...], preferred_element_type=F32)

    proj_ref[...] = jnp.dot(h_ref[...], w_ref[...], preferred_element_type=F32).astype(BF16)


def _in_proj(x2d, norm_w, w_main, w_dt, tm):
    m, d = x2d.shape
    return pl.pallas_call(
        _in_proj_kernel,
        grid=(m // tm, PROJ_W // PROJ_TILE),
        in_specs=[
            pl.BlockSpec((tm, d), lambda i, j: (i, 0)),
            pl.BlockSpec((1, d), lambda i, j: (0, 0)),
            pl.BlockSpec((d, PROJ_TILE), lambda i, j: (0, j)),
            pl.BlockSpec((d, LANES), lambda i, j: (0, 0)),
        ],
        out_specs=[
            pl.BlockSpec((tm, PROJ_TILE), lambda i, j: (i, j)),
            pl.BlockSpec((tm, LANES), lambda i, j: (i, 0)),
            pl.BlockSpec((tm, d), lambda i, j: (i, 0)),
        ],
        out_shape=[
            jax.ShapeDtypeStruct((m, PROJ_W), BF16),
            jax.ShapeDtypeStruct((m, LANES), F32),
            jax.ShapeDtypeStruct((m, d), BF16),
        ],
        compiler_params=_cparams(2),
        name="in_proj",
    )(x2d, norm_w, w_main, w_dt)


def _qk_proj_kernel(h_ref, w_ref, qkw_ref, gsum_ref, o_ref):
    acc = jnp.dot(h_ref[...], w_ref[...], preferred_element_type=F32)
    for c in range(PROJ_TILE // MXU_DIM):
        cs = slice(c * MXU_DIM, (c + 1) * MXU_DIM)
        a = acc[:, cs]
        ss = jnp.dot((a * a).astype(BF16), gsum_ref[...], preferred_element_type=F32)
        o_ref[:, cs] = (a * lax.rsqrt(ss * (1.0 / HEAD_DIM) + RMS_EPS) * qkw_ref[0, :, cs]).astype(BF16)


def _qk_proj(h, w_qk, qkw, gsum, tm):
    m, d = h.shape
    return pl.pallas_call(
        _qk_proj_kernel,
        grid=(m // tm, 2),
        in_specs=[
            pl.BlockSpec((tm, d), lambda i, j: (i, 0)),
            pl.BlockSpec((d, PROJ_TILE), lambda i, j: (0, j)),
            pl.BlockSpec((1, 1, PROJ_TILE), lambda i, j: (j, 0, 0)),
            pl.BlockSpec((MXU_DIM, MXU_DIM), lambda i, j: (0, 0)),
        ],
        out_specs=pl.BlockSpec((tm, PROJ_TILE), lambda i, j: (i, j)),
        out_shape=jax.ShapeDtypeStruct((m, 2 * PROJ_TILE), BF16),
        compiler_params=_cparams(2),
        name="qk_proj",
    )(h, w_qk, qkw, gsum)


def _bias_tiles_kernel(table_ref, out_ref):
    h = pl.program_id(0)
    r = lax.broadcasted_iota(jnp.int32, (LANES, LANES), 0)
    c = lax.broadcasted_iota(jnp.int32, (LANES, LANES), 1)
    for d in range(3):
        rel = (d - 1) * LANES + c - r
        n = jnp.abs(rel)
        large = jnp.full((LANES, LANES), 8, jnp.int32)
        for t in REL_THRESHOLDS:
            large = large + jnp.where(n >= t, 1, 0)
        bucket = jnp.where(rel > 0, 16, 0) + jnp.where(n < 8, n, large)
        val = jnp.zeros((LANES, LANES), F32)
        for bkt in range(32):
            val = jnp.where(bucket == bkt, table_ref[bkt, h], val)
        out_ref[0, d] = val * LOG2E


def _bias_tiles(rel_bias):
    return pl.pallas_call(
        _bias_tiles_kernel,
        grid=(HEADS,),
        in_specs=[pl.BlockSpec(memory_space=pltpu.SMEM)],
        out_specs=pl.BlockSpec((1, 3, LANES, LANES), lambda h: (h, 0, 0, 0)),
        out_shape=jax.ShapeDtypeStruct((HEADS, 3, LANES, LANES), F32),
        compiler_params=_cparams(1),
        name="bias_tiles",
    )(rel_bias)


def _near_bias_tiles(tq, tk):
    nbq, nbk = tq // LANES, tk // LANES
    table = {}
    for a in range(tk // tq):
        for d in (-1, 0, 1):
            ob = d * nbk - a * nbq
            if ob + nbk - 1 >= -1 and ob - (nbq - 1) <= 1:
                table[(a, d)] = (2 + len(table), ob)
    return table


def _attn_kernel(lam_ref, table_ref, q_ref, k_ref, v_ref, bsm_ref, subln_ref, o_ref,
                 s_scr, p_scr, mcur_scr, alpha_scr, m_scr, acc_scr, bias_scr, *, tq, tk, seq, out_scale):
    nk = seq // tk
    n_stages = (seq // tq) * nk
    ratio = tk // tq
    near = _near_bias_tiles(tq, tk)
    h = pl.program_id(0)
    b = pl.program_id(1)

    @pl.when(b == 0)
    def _():
        c_lo = table_ref[15, h] * LOG2E
        c_hi = table_ref[31, h] * LOG2E
        ones = jnp.ones((tq, tk), F32)
        bias_scr[0] = ones * c_lo
        bias_scr[1] = ones * c_hi
        for idx, ob in near.values():
            for rb in range(tq // LANES):
                for cb in range(tk // LANES):
                    off = ob + cb - rb
                    if abs(off) <= 1:
                        blk = bsm_ref[0, off + 1]
                    else:
                        blk = jnp.ones((LANES, LANES), F32) * (c_lo if off < 0 else c_hi)
                    bias_scr[idx, rb * LANES:(rb + 1) * LANES, cb * LANES:(cb + 1) * LANES] = blk

    c_lo = table_ref[15, h] * LOG2E
    c_hi = table_ref[31, h] * LOG2E
    lane = lax.broadcasted_iota(jnp.int32, (tq, LANES), 1)
    zero = jnp.zeros((tq, LANES), BF16)
    ones_blk = jnp.ones((tk, LANES), BF16)

    first_near = {a: min(d for (a0, d) in near if a0 == a) for a in range(ratio)}
    n_e = max(sum(1 for (a0, _) in near if a0 == a) for a in range(ratio))

    def explicit_start(i):
        c = i // ratio
        a = i - c * ratio
        lo = c + first_near[0]
        for a0 in range(1, ratio):
            lo = jnp.where(a == a0, c + first_near[a0], lo)
        return jnp.clip(lo, 0, nk - n_e)

    def key_tile(i, pos):
        e0 = explicit_start(i)
        if pos < n_e:
            return e0 + pos, None
        k = pos - n_e
        j = k + jnp.where(k >= e0, n_e, 0)
        return j, jnp.where(j < e0, c_lo, c_hi)

    def bias_kind(i, j):
        c = i // ratio
        a = i - c * ratio
        d = j - c
        kind = jnp.where(d < 0, 0, 1)
        for (a0, d0), (idx, _) in near.items():
            kind = jnp.where((a == a0) & (d == d0), idx, kind)
        return kind

    def scores(i, pos, slot):
        j, const = key_tile(i, pos)
        q = q_ref[pl.ds(pl.multiple_of(i * tq, tq), tq), :]
        qs = jnp.concatenate([jnp.where(lane < HEAD_DIM, q, zero), jnp.where(lane >= HEAD_DIM, q, zero)], axis=0)
        k = k_ref[pl.ds(pl.multiple_of(j * tk, tk), tk), :]
        s = lax.dot_general(qs, k, (((1,), (1,)), ((), ())), preferred_element_type=F32)
        if const is None:
            kind = bias_kind(i, j)
            s = jnp.concatenate([s[0:tq] + bias_scr[kind], s[tq:2 * tq] + bias_scr[kind]], axis=0)
            row_max = jnp.max(s, axis=1, keepdims=True)
        else:
            row_max = jnp.max(s, axis=1, keepdims=True) + const
        s_scr[slot] = s
        mcur_scr[slot] = jnp.broadcast_to(row_max, (2 * tq, LANES))

    def softmax(i, pos, slot):
        _, const = key_tile(i, pos)
        if pos == 0:
            m_next = mcur_scr[slot]
        else:
            m_prev = m_scr[...]
            m_next = jnp.maximum(m_prev, mcur_scr[slot])
            alpha_scr[slot] = jnp.exp2(m_prev - m_next)
        shift = m_next if const is None else m_next - const
        p_scr[slot] = jnp.exp2(s_scr[slot] - pltpu.repeat(shift, tk // LANES, axis=1)).astype(BF16)
        m_scr[...] = m_next

    def values(i, pos, slot):
        j, _ = key_tile(i, pos)
        vaug = jnp.concatenate([v_ref[pl.ds(pl.multiple_of(j * tk, tk), tk), :], ones_blk], axis=1)
        pv = jnp.dot(p_scr[slot], vaug, preferred_element_type=F32)
        if pos == 0:
            acc_scr[...] = pv
        else:
            acc_scr[...] = acc_scr[...] * pltpu.repeat(alpha_scr[slot], 2, axis=1) + pv

    def finalize(i):
        acc = acc_scr[...]
        o12 = acc[:, 0:LANES] / acc[:, LANES:2 * LANES]
        o = o12[0:tq] - lam_ref[0] * o12[tq:2 * tq]
        o_ref[pl.ds(pl.multiple_of(i * tq, tq), tq), :] = (_rms(o, subln_ref[...]) * out_scale).astype(BF16)

    nq = seq // tq
    scores(0, 0, 0)
    softmax(0, 0, 0)
    scores(0, 1, 1)

    def body(g, carry):
        nxt = jnp.minimum(g + 1, nq - 1)
        for r in range(nk):
            values(g, r, r % 2)
            softmax(g if r + 1 < nk else nxt, (r + 1) % nk, (r + 1) % 2)
            scores(g if r + 2 < nk else nxt, (r + 2) % nk, r % 2)
        finalize(g)
        return carry

    lax.fori_loop(0, nq, body, 0)


def _diff_attn(qk, proj, bias_small, rel_bias, lam, subln_w, batch, seq, tq, tk, out_scale):
    assert seq % tk == 0 and tk % tq == 0 and (seq // tk) % 2 == 0
    qb, kb, vb = 0, HEADS, COL_V // LANES
    kern = functools.partial(_attn_kernel, tq=tq, tk=tk, seq=seq, out_scale=out_scale)
    n_bias = 2 + len(_near_bias_tiles(tq, tk))
    return pl.pallas_call(
        kern,
        grid=(HEADS, batch),
        in_specs=[
            pl.BlockSpec(memory_space=pltpu.SMEM),
            pl.BlockSpec(memory_space=pltpu.SMEM),
            pl.BlockSpec((seq, LANES), lambda h, b: (b, qb + h)),
            pl.BlockSpec((seq, LANES), lambda h, b: (b, kb + h)),
            pl.BlockSpec((seq, LANES), lambda h, b: (b, vb + h)),
            pl.BlockSpec((1, 3, LANES, LANES), lambda h, b: (h, 0, 0, 0)),
            pl.BlockSpec((1, LANES), lambda h, b: (0, 0)),
        ],
        out_specs=pl.BlockSpec((seq, LANES), lambda h, b: (b, h)),
        out_shape=jax.ShapeDtypeStruct((batch * seq, HEADS * LANES), BF16),
        scratch_shapes=[
            pltpu.VMEM((2, 2 * tq, tk), F32),
            pltpu.VMEM((2, 2 * tq, tk), BF16),
            pltpu.VMEM((2, 2 * tq, LANES), F32),
            pltpu.VMEM((2, 2 * tq, LANES), F32),
            pltpu.VMEM((2 * tq, LANES), F32),
            pltpu.VMEM((2 * tq, 2 * LANES), F32),
            pltpu.VMEM((n_bias, tq, tk), F32),
        ],
        compiler_params=_cparams(2, VMEM_LIMIT_ATTN),
        name="diff_attn",
    )(lam, rel_bias, qk, qk, proj, bias_small, subln_w)


def _dt_and_da(dt_ref, dtb_ref, alog_ref):
    raw = dt_ref[...] + dtb_ref[...]
    dt = jnp.maximum(raw, 0.0) + jnp.log1p(jnp.exp(-jnp.abs(raw)))
    a = -jnp.exp(alog_ref[...])
    lane = lax.broadcasted_iota(jnp.int32, dt.shape, 1)
    dt = jnp.where(lane < 2 * SSM_HEADS, dt, 0.0)
    return dt, dt * a


def _tri(upper):
    r = lax.broadcasted_iota(jnp.int32, (CHUNK, CHUNK), 0)
    c = lax.broadcasted_iota(jnp.int32, (CHUNK, CHUNK), 1)
    return (c >= r) if upper else (c <= r)


def _tri_dot(mask, x):
    m = jnp.where(mask, 1.0, 0.0).astype(BF16)
    hi, lo = _split_bf16(x)
    return jnp.dot(m, hi, preferred_element_type=F32) + jnp.dot(m, lo, preferred_element_type=F32)


def _expand(x, e2_ref):
    hi, lo = _split_bf16(x)
    return jnp.dot(jnp.concatenate([hi, lo], axis=1), e2_ref[...], preferred_element_type=F32)


def _shift_rows(u, halo, offset):
    n = u.shape[0]
    if offset == 0:
        return u
    rolled = pltpu.roll(u, (-offset) % n, 0)
    row8 = lax.broadcasted_iota(jnp.int32, (8, u.shape[1]), 0)
    if offset < 0:
        hfix = pltpu.roll(halo, (-offset) % 8, 0)
        first = jnp.where(row8 < -offset, hfix, rolled[0:8])
        return jnp.concatenate([first, rolled[8:n]], axis=0)
    hfix = pltpu.roll(halo, (8 - offset) % 8, 0)
    last = jnp.where(row8 >= 8 - offset, hfix, rolled[n - 8:n])
    return jnp.concatenate([rolled[0:n - 8], last], axis=0)


def _ssd_bwd_kernel(xc_ref, xp_ref, xn_ref, dt_ref, cw_ref, cb_ref, dtb_ref, alog_ref, e2b_ref,
                    xact_ref, ybi_ref, state_scr, *, n_chunks):
    c = pl.program_id(1)
    cr = n_chunks - 1 - c

    @pl.when(c == 0)
    def _():
        state_scr[...] = jnp.zeros(state_scr.shape, F32)

    slab = 512
    keep_prev = jnp.where(cr > 0, 1.0, 0.0)
    keep_next = jnp.where(cr < n_chunks - 1, 1.0, 0.0)
    for sl in range(CONV_DIM // slab):
        cs = slice(sl * slab, (sl + 1) * slab)
        u = xc_ref[:, cs].astype(F32)
        hp = xp_ref[8:16, cs].astype(F32) * keep_prev
        hn = xn_ref[0:8, cs].astype(F32) * keep_next
        acc = cb_ref[:, cs] + _shift_rows(u, hp, -2) * cw_ref[0:1, cs]
        acc = acc + _shift_rows(u, hp, -1) * cw_ref[1:2, cs]
        acc = acc + u * cw_ref[2:3, cs]
        acc = acc + _shift_rows(u, hn, 1) * cw_ref[3:4, cs]
        xact_ref[:, cs] = _silu(acc).astype(BF16)

    dt, da = _dt_and_da(dt_ref, dtb_ref, alog_ref)
    rcs = _tri_dot(_tri(True), da)
    rcs_full = _expand(rcs, e2b_ref)
    dt_full = _expand(dt, e2b_ref)
    decay = jnp.exp(rcs_full)
    rcs0 = rcs_full[0:1, :]
    chunk_decay = jnp.exp(rcs0)
    xs = xact_ref[:, 0:SSM_INNER].astype(F32)
    xw = (xs * (jnp.exp(rcs0 - rcs_full) * dt_full)).astype(BF16)
    for g in range(SSM_GROUPS):
        gs = slice(g * GROUP_W, (g + 1) * GROUP_W)
        bg = xact_ref[:, SSM_INNER + g * SSM_STATE:SSM_INNER + (g + 1) * SSM_STATE]
        cg = xact_ref[:, SSM_INNER + (SSM_GROUPS + g) * SSM_STATE:SSM_INNER + (SSM_GROUPS + g + 1) * SSM_STATE]
        st = state_scr[:, gs]
        yb = jnp.dot(cg, st.astype(BF16), preferred_element_type=F32) * decay[:, gs]
        ybi_ref[:, gs] = yb.astype(BF16)
        upd = lax.dot_general(bg, xw[:, gs], (((0,), (0,)), ((), ())), preferred_element_type=F32)
        state_scr[:, gs] = st * chunk_decay[:, gs] + upd


def _ssd_bwd(proj, dt_raw, conv_w, conv_b, dt_bias, a_log, e2b, batch, seq):
    n_chunks = seq // CHUNK
    sub = CHUNK // 16
    last16 = batch * seq // 16 - 1
    kern = functools.partial(_ssd_bwd_kernel, n_chunks=n_chunks)

    def cur(b, c):
        return b * n_chunks + (n_chunks - 1 - c)

    return pl.pallas_call(
        kern,
        grid=(batch, n_chunks),
        in_specs=[
            pl.BlockSpec((CHUNK, CONV_DIM), lambda b, c: (cur(b, c), 0)),
            pl.BlockSpec((16, CONV_DIM), lambda b, c: (jnp.maximum(cur(b, c) * sub - 1, 0), 0)),
            pl.BlockSpec((16, CONV_DIM), lambda b, c: (jnp.minimum((cur(b, c) + 1) * sub, last16), 0)),
            pl.BlockSpec((CHUNK, LANES), lambda b, c: (cur(b, c), 0)),
            pl.BlockSpec((4, CONV_DIM), lambda b, c: (0, 0)),
            pl.BlockSpec((1, CONV_DIM), lambda b, c: (0, 0)),
            pl.BlockSpec((1, LANES), lambda b, c: (0, 0)),
            pl.BlockSpec((1, LANES), lambda b, c: (0, 0)),
            pl.BlockSpec((2 * LANES, SSM_INNER), lambda b, c: (0, 0)),
        ],
        out_specs=[
            pl.BlockSpec((CHUNK, CONV_DIM), lambda b, c: (cur(b, c), 0)),
            pl.BlockSpec((CHUNK, SSM_INNER), lambda b, c: (cur(b, c), 0)),
        ],
        out_shape=[
            jax.ShapeDtypeStruct((batch * seq, CONV_DIM), BF16),
            jax.ShapeDtypeStruct((batch * seq, SSM_INNER), BF16),
        ],
        scratch_shapes=[pltpu.VMEM((SSM_STATE, SSM_INNER), F32)],
        compiler_params=_cparams(2),
        name="ssd_bwd",
    )(proj, proj, proj, dt_raw, conv_w, conv_b, dt_bias, a_log, e2b)


def _ssd_fwd_kernel(xact_ref, z_ref, dt_ref, ybi_ref, dtb_ref, alog_ref, dsk_ref, nw_ref, e2f_ref,
                    out_ref, state_scr, zt_scr):
    c = pl.program_id(1)

    @pl.when(c == 0)
    def _():
        state_scr[...] = jnp.zeros(state_scr.shape, F32)

    nh = SSM_HEADS
    dt, da = _dt_and_da(dt_ref, dtb_ref, alog_ref)
    tril = _tri(False)
    triu = _tri(True)
    acs = _tri_dot(tril, da)
    rcs = _tri_dot(triu, da)
    lane = lax.broadcasted_iota(jnp.int32, (CHUNK, LANES), 1)
    packed = jnp.where(lane < 2 * nh, dt, pltpu.roll(jnp.where(lane < nh, acs, rcs), 2 * nh, 1))
    zt_scr[...] = packed.T

    acs_full = _expand(acs, e2f_ref)
    dt_full = _expand(dt, e2f_ref)
    decay = jnp.exp(acs_full)
    acs_last = acs_full[CHUNK - 1:CHUNK, :]
    chunk_decay = jnp.exp(acs_last)
    xs_bf = xact_ref[:, 0:SSM_INNER]
    xs = xs_bf.astype(F32)
    xw = (xs * (jnp.exp(acs_last - acs_full) * dt_full)).astype(BF16)
    lane_w = lax.broadcasted_iota(jnp.int32, (CHUNK, LANES), 1)

    for g in range(SSM_GROUPS):
        gs = slice(g * GROUP_W, (g + 1) * GROUP_W)
        bg = xact_ref[:, SSM_INNER + g * SSM_STATE:SSM_INNER + (g + 1) * SSM_STATE]
        cg = xact_ref[:, SSM_INNER + (SSM_GROUPS + g) * SSM_STATE:SSM_INNER + (SSM_GROUPS + g + 1) * SSM_STATE]
        cb = lax.dot_general(cg, bg, (((1,), (1,)), ((), ())), preferred_element_type=F32)
        ys = []
        for r in range(GROUP_W // SSM_HEAD_DIM):
            h = g * (GROUP_W // SSM_HEAD_DIM) + r
            seg_f = acs[:, h:h + 1] - zt_scr[2 * nh + h:2 * nh + h + 1, :]
            w_f = jnp.exp(jnp.where(tril, seg_f, -jnp.inf)) * zt_scr[h:h + 1, :]
            seg_b = rcs[:, nh + h:nh + h + 1] - zt_scr[3 * nh + h:3 * nh + h + 1, :]
            w_b = jnp.exp(jnp.where(triu, seg_b, -jnp.inf)) * zt_scr[nh + h:nh + h + 1, :]
            mat = (cb * (w_f + w_b)).astype(BF16)
            pair = xs_bf[:, (h // 2) * LANES:(h // 2 + 1) * LANES]
            ys.append(jnp.dot(mat, pair, preferred_element_type=F32))
        y = jnp.concatenate([jnp.where(lane_w < SSM_HEAD_DIM, ys[0], ys[1]),
                             jnp.where(lane_w < SSM_HEAD_DIM, ys[2], ys[3])], axis=1)
        st = state_scr[:, gs]
        y = y + jnp.dot(cg, st.astype(BF16), preferred_element_type=F32) * decay[:, gs]
        upd = lax.dot_general(bg, xw[:, gs], (((0,), (0,)), ((), ())), preferred_element_type=F32)
        state_scr[:, gs] = st * chunk_decay[:, gs] + upd
        y = y + ybi_ref[:, gs].astype(F32) + dsk_ref[:, gs] * xs[:, gs]
        y = y * _silu(z_ref[:, gs].astype(F32))
        out_ref[:, gs] = _rms(y, nw_ref[:, gs]).astype(BF16)


def _ssd_fwd(xact, proj, dt_raw, ybi, dt_bias, a_log, d_skip, norm_w, e2f, batch, seq):
    n_chunks = seq // CHUNK
    zb = COL_Z // SSM_INNER
    return pl.pallas_call(
        _ssd_fwd_kernel,
        grid=(batch, n_chunks),
        in_specs=[
            pl.BlockSpec((CHUNK, CONV_DIM), lambda b, c: (b * n_chunks + c, 0)),
            pl.BlockSpec((CHUNK, SSM_INNER), lambda b, c: (b * n_chunks + c, zb)),
            pl.BlockSpec((CHUNK, LANES), lambda b, c: (b * n_chunks + c, 0)),
            pl.BlockSpec((CHUNK, SSM_INNER), lambda b, c: (b * n_chunks + c, 0)),
            pl.BlockSpec((1, LANES), lambda b, c: (0, 0)),
            pl.BlockSpec((1, LANES), lambda b, c: (0, 0)),
            pl.BlockSpec((1, SSM_INNER), lambda b, c: (0, 0)),
            pl.BlockSpec((1, SSM_INNER), lambda b, c: (0, 0)),
            pl.BlockSpec((2 * LANES, SSM_INNER), lambda b, c: (0, 0)),
        ],
        out_specs=pl.BlockSpec((CHUNK, SSM_INNER), lambda b, c: (b * n_chunks + c, 0)),
        out_shape=jax.ShapeDtypeStruct((batch * seq, SSM_INNER), BF16),
        scratch_shapes=[pltpu.VMEM((SSM_STATE, SSM_INNER), F32), pltpu.VMEM((LANES, CHUNK), F32)],
        compiler_params=_cparams(2),
        name="ssd_fwd",
    )(xact, proj, dt_raw, ybi, dt_bias, a_log, d_skip, norm_w, e2f)


def _mix_out_kernel(x_ref, ao_ref, so_ref, ga_ref, gs_ref, wa_ref, ws_ref, wo_ref, n2_ref, x1_ref, h2_ref):
    attn = jnp.dot(ao_ref[...], wa_ref[...], preferred_element_type=F32)
    ssd = jnp.dot(so_ref[...], ws_ref[...], preferred_element_type=F32)
    mixed = (jax.nn.sigmoid(ga_ref[...].astype(F32)) * attn
             + jax.nn.sigmoid(gs_ref[...].astype(F32)) * ssd)
    x1 = x_ref[...] + jnp.dot(mixed.astype(BF16), wo_ref[...], preferred_element_type=F32)
    x1_ref[...] = x1
    h2_ref[...] = _rms(x1, n2_ref[...]).astype(BF16)


def _mix_out(x2d, attn_o, ssd_o, proj, w_attn, w_ssm, w_out, norm2_w, tm):
    m, d = x2d.shape
    gb = COL_GATE // d
    return pl.pallas_call(
        _mix_out_kernel,
        grid=(m // tm,),
        in_specs=[
            pl.BlockSpec((tm, d), lambda i: (i, 0)),
            pl.BlockSpec((tm, attn_o.shape[1]), lambda i: (i, 0)),
            pl.BlockSpec((tm, ssd_o.shape[1]), lambda i: (i, 0)),
            pl.BlockSpec((tm, d), lambda i: (i, gb)),
            pl.BlockSpec((tm, d), lambda i: (i, gb + 1)),
            pl.BlockSpec(w_attn.shape, lambda i: (0, 0)),
            pl.BlockSpec(w_ssm.shape, lambda i: (0, 0)),
            pl.BlockSpec(w_out.shape, lambda i: (0, 0)),
            pl.BlockSpec((1, d), lambda i: (0, 0)),
        ],
        out_specs=[pl.BlockSpec((tm, d), lambda i: (i, 0)), pl.BlockSpec((tm, d), lambda i: (i, 0))],
        out_shape=[jax.ShapeDtypeStruct((m, d), F32), jax.ShapeDtypeStruct((m, d), BF16)],
        compiler_params=_cparams(1),
        name="mix_out",
    )(x2d, attn_o, ssd_o, proj, proj, w_attn, w_ssm, w_out, norm2_w)


FFN_CW = 256
HALO = 16


def _ffn_kernel(x1_ref, h_ref, hp_ref, hn_ref, wu_ref, cw_ref, cb_ref, wd_ref, o_ref, act_scr, *, tiles_per_seq):
    i = pl.program_id(0)
    tm = h_ref.shape[0]
    f = wd_ref.shape[0]
    pos = i % tiles_per_seq
    keep_prev = jnp.where(pos > 0, 1.0, 0.0)
    keep_next = jnp.where(pos < tiles_per_seq - 1, 1.0, 0.0)
    h_ext = jnp.concatenate([hp_ref[...], h_ref[...], hn_ref[...]], axis=0)

    def conv(col):
        cs = slice(col, col + FFN_CW)
        u_ext = jnp.dot(h_ext, wu_ref[:, cs], preferred_element_type=F32)
        u = u_ext[HALO:HALO + tm]
        up = u_ext[HALO - 8:HALO] * keep_prev
        un = u_ext[HALO + tm:HALO + tm + 8] * keep_next
        return (cb_ref[:, cs] + _shift_rows(u, up, -1) * cw_ref[0:1, cs] + u * cw_ref[1:2, cs]
                + _shift_rows(u, un, 1) * cw_ref[2:3, cs])

    for c in range(f // FFN_CW):
        act = _silu(conv(c * FFN_CW)) * conv(f + c * FFN_CW)
        act_scr[:, c * FFN_CW:(c + 1) * FFN_CW] = act.astype(BF16)
    o_ref[...] = x1_ref[...] + jnp.dot(act_scr[...], wd_ref[...], preferred_element_type=F32)


def _ffn(x1, h2, w_up, conv_w, conv_b, w_down, seq, tm):
    m, d = x1.shape
    f = w_down.shape[0]
    sub = tm // HALO
    last = m // HALO - 1
    kern = functools.partial(_ffn_kernel, tiles_per_seq=seq // tm)
    resident = pl.Buffered(1)
    return pl.pallas_call(
        kern,
        grid=(m // tm,),
        in_specs=[
            pl.BlockSpec((tm, d), lambda i: (i, 0)),
            pl.BlockSpec((tm, d), lambda i: (i, 0)),
            pl.BlockSpec((HALO, d), lambda i: (jnp.maximum(i * sub - 1, 0), 0)),
            pl.BlockSpec((HALO, d), lambda i: (jnp.minimum((i + 1) * sub, last), 0)),
            pl.BlockSpec(w_up.shape, lambda i: (0, 0), pipeline_mode=resident),
            pl.BlockSpec(conv_w.shape, lambda i: (0, 0)),
            pl.BlockSpec(conv_b.shape, lambda i: (0, 0)),
            pl.BlockSpec(w_down.shape, lambda i: (0, 0), pipeline_mode=resident),
        ],
        out_specs=pl.BlockSpec((tm, d), lambda i: (i, 0)),
        out_shape=jax.ShapeDtypeStruct((m, d), F32),
        scratch_shapes=[pltpu.VMEM((tm, f), BF16)],
        compiler_params=_cparams(1),
        name="ffn",
    )(x1, h2, h2, h2, w_up, conv_w, conv_b, w_down)


def _head_expand(first_row):
    rows = jnp.arange(2 * LANES)[:, None] % LANES
    cols = jnp.arange(SSM_INNER)[None, :] // SSM_HEAD_DIM
    return (rows == cols + first_row).astype(BF16)


def _layer(x2d, batch, seq, lambda_init, p):
    d = x2d.shape[1]
    w_in = p["w_in"]
    sizes = (HEADS * 2 * HEAD_DIM,) * 3 + (SSM_INNER, CONV_DIM, 2 * SSM_HEADS, 2 * d)
    offs = [0]
    for s in sizes:
        offs.append(offs[-1] + s)
    wq, wk, wv, wz, wxbc, wdt, wg = (w_in[:, offs[n]:offs[n + 1]] for n in range(7))
    w_main = jnp.concatenate([wxbc, wz, wg, wv], axis=1).astype(BF16)
    w_qk = jnp.concatenate([wq, wk], axis=1).astype(BF16)
    w_dt = jnp.pad(wdt, ((0, 0), (0, LANES - wdt.shape[1]))).astype(BF16)
    qkw = jnp.stack([jnp.tile(p["q_norm_w"], 2 * HEADS) * (HEAD_DIM ** -0.5 * LOG2E),
                     jnp.tile(p["k_norm_w"], 2 * HEADS)])[:, None, :]
    gi = jnp.arange(MXU_DIM) // HEAD_DIM
    gsum = (gi[:, None] == gi[None, :]).astype(BF16)

    proj, dt_raw, h1 = _in_proj(x2d, p["norm1_w"][None, :], w_main, w_dt, tm=1024)
    qk = _qk_proj(h1, w_qk, qkw, gsum, tm=1024)

    lam = (jnp.exp(jnp.sum(p["lambda_q1"] * p["lambda_k1"])) - jnp.exp(jnp.sum(p["lambda_q2"] * p["lambda_k2"]))
           + lambda_init).reshape(1).astype(F32)
    bias_small = _bias_tiles(p["rel_bias"])
    attn_o = _diff_attn(qk, proj, bias_small, p["rel_bias"], lam, p["subln_w"][None, :], batch, seq,
                        tq=512, tk=1024, out_scale=1.0 - lambda_init)

    pad = LANES - 2 * SSM_HEADS
    dt_bias = jnp.pad(jnp.concatenate([p["dt_bias_f"], p["dt_bias_b"]]), (0, pad))[None, :]
    a_log = jnp.pad(jnp.concatenate([p["a_log_f"], p["a_log_b"]]), (0, pad))[None, :]
    xact, ybi = _ssd_bwd(proj, dt_raw, p["ssm_conv_w"], p["ssm_conv_b"][None, :], dt_bias, a_log,
                         _head_expand(SSM_HEADS), batch, seq)
    ssd_o = _ssd_fwd(xact, proj, dt_raw, ybi, dt_bias, a_log,
                     jnp.repeat(p["d_skip"], SSM_HEAD_DIM)[None, :], p["ssm_norm_w"][None, :],
                     _head_expand(0), batch, seq)

    x1, h2 = _mix_out(x2d, attn_o, ssd_o, proj, p["w_attn_out"].astype(BF16), p["w_ssm_out"].astype(BF16),
                      p["w_out"].astype(BF16), p["norm2_w"][None, :], tm=512)

    return _ffn(x1, h2, p["w_ffn_up"].astype(BF16), p["ffn_conv_w"], p["ffn_conv_b"][None, :],
                p["w_ffn_down"].astype(BF16), seq, tm=512)


def kernel(x, norm1_w, w_in, q_norm_w, k_norm_w, rel_bias, lambda_q1, lambda_k1, lambda_q2, lambda_k2, subln_w, w_attn_out, ssm_conv_w, ssm_conv_b, dt_bias_f, a_log_f, dt_bias_b, a_log_b, d_skip, ssm_norm_w, w_ssm_out, w_out, norm2_w, w_ffn_up, ffn_conv_w, ffn_conv_b, w_ffn_down):
    batch, seq, d = x.shape
    layered = dict(norm1_w=norm1_w, w_in=w_in, q_norm_w=q_norm_w, k_norm_w=k_norm_w, lambda_q1=lambda_q1,
                   lambda_k1=lambda_k1, lambda_q2=lambda_q2, lambda_k2=lambda_k2, subln_w=subln_w,
                   w_attn_out=w_attn_out, ssm_conv_w=ssm_conv_w, ssm_conv_b=ssm_conv_b, dt_bias_f=dt_bias_f,
                   a_log_f=a_log_f, dt_bias_b=dt_bias_b, a_log_b=a_log_b, d_skip=d_skip, ssm_norm_w=ssm_norm_w,
                   w_ssm_out=w_ssm_out, w_out=w_out, norm2_w=norm2_w, w_ffn_up=w_ffn_up, ffn_conv_w=ffn_conv_w,
                   ffn_conv_b=ffn_conv_b, w_ffn_down=w_ffn_down)
    x2d = x.reshape(batch * seq, d)
    for layer in range(norm1_w.shape[0]):
        p = {k: v[layer] for k, v in layered.items()}
        p["rel_bias"] = rel_bias
        lambda_init = 0.8 - 0.6 * math.exp(-0.3 * layer)
        x2d = _layer(x2d, batch, seq, lambda_init, p)
    return x2d.reshape(batch, seq, d)
```

```python
import functools
import math

import jax
import jax.numpy as jnp
from jax import lax
from jax.experimental import pallas as pl
from jax.experimental.pallas import tpu as pltpu

F32 = jnp.float32
BF16 = jnp.bfloat16

HEADS = 8
HEAD_DIM = 64
SSM_HEADS = 32
SSM_HEAD_DIM = 64
SSM_GROUPS = 8
SSM_STATE = 128
SSM_INNER = SSM_HEADS * SSM_HEAD_DIM
GROUP_W = SSM_INNER // SSM_GROUPS
CONV_DIM = SSM_INNER + 2 * SSM_GROUPS * SSM_STATE
CHUNK = 128
SSM_SHIFTS = (-2, -1, 1)
HALO = 16
RMS_EPS = 1e-6
LOG2E = 1.4426950408889634
NEG_BIG = -1e30

REL_THRESHOLDS = (12, 16, 23, 32, 46, 64, 91)
REL_FAR = 129

LANES = 128
VMEM_LIMIT = 52 * 1024 * 1024
VMEM_LIMIT_ATTN = 60 * 1024 * 1024

MXU_DIM = 256

COL_XBC = 0
COL_Z = 4096
COL_GATE = 6144
COL_V = 8192
PROJ_W = 9216
PROJ_TILE = 1024


def _cparams(n_axes, vmem_limit=VMEM_LIMIT):
    return pltpu.CompilerParams(dimension_semantics=("arbitrary",) * n_axes, vmem_limit_bytes=vmem_limit)


def _rms(x, w):
    return x * lax.rsqrt(jnp.mean(x * x, axis=-1, keepdims=True) + RMS_EPS) * w


def _split_bf16(x):
    hi = x.astype(BF16)
    lo = (x - hi.astype(F32)).astype(BF16)
    return hi, lo


def _silu(x):
    return x / (1.0 + jnp.exp(-x))


def _in_proj_kernel(x_ref, nw_ref, w_ref, wdt_ref, proj_ref, dt_ref, h_ref):
    @pl.when(pl.program_id(1) == 0)
    def _():
        h = _rms(x_ref[...], nw_ref[...]).astype(BF16)
        h_ref[...] = h
        dt_ref[...] = jnp.dot(h, wdt_ref[...], preferred_element_type=F32)

    proj_ref[...] = jnp.dot(h_ref[...], w_ref[...], preferred_element_type=F32).astype(BF16)


def _in_proj(x2d, norm_w, w_main, w_dt, tm):
    m, d = x2d.shape
    return pl.pallas_call(
        _in_proj_kernel,
        grid=(m // tm, PROJ_W // PROJ_TILE),
        in_specs=[
            pl.BlockSpec((tm, d), lambda i, j: (i, 0)),
            pl.BlockSpec((1, d), lambda i, j: (0, 0)),
            pl.BlockSpec((d, PROJ_TILE), lambda i, j: (0, j)),
            pl.BlockSpec((d, LANES), lambda i, j: (0, 0)),
        ],
        out_specs=[
            pl.BlockSpec((tm, PROJ_TILE), lambda i, j: (i, j)),
            pl.BlockSpec((tm, LANES), lambda i, j: (i, 0)),
            pl.BlockSpec((tm, d), lambda i, j: (i, 0)),
        ],
        out_shape=[
            jax.ShapeDtypeStruct((m, PROJ_W), BF16),
            jax.ShapeDtypeStruct((m, LANES), F32),
            jax.ShapeDtypeStruct((m, d), BF16),
        ],
        compiler_params=_cparams(2),
        name="in_proj",
    )(x2d, norm_w, w_main, w_dt)


def _qk_proj_kernel(h_ref, w_ref, qkw_ref, gsum_ref, o_ref):
    acc = jnp.dot(h_ref[...], w_ref[...], preferred_element_type=F32)
    for c in range(PROJ_TILE // MXU_DIM):
        cs = slice(c * MXU_DIM, (c + 1) * MXU_DIM)
        a = acc[:, cs]
        ss = jnp.dot((a * a).astype(BF16), gsum_ref[...], preferred_element_type=F32)
        o_ref[:, cs] = (a * lax.rsqrt(ss * (1.0 / HEAD_DIM) + RMS_EPS) * qkw_ref[0, :, cs]).astype(BF16)


def _qk_proj(h, w_qk, qkw, gsum, tm):
    m, d = h.shape
    return pl.pallas_call(
        _qk_proj_kernel,
        grid=(m // tm, 2),
        in_specs=[
            pl.BlockSpec((tm, d), lambda i, j: (i, 0)),
            pl.BlockSpec((d, PROJ_TILE), lambda i, j: (0, j)),
            pl.BlockSpec((1, 1, PROJ_TILE), lambda i, j: (j, 0, 0)),
            pl.BlockSpec((MXU_DIM, MXU_DIM), lambda i, j: (0, 0)),
        ],
        out_specs=pl.BlockSpec((tm, PROJ_TILE), lambda i, j: (i, j)),
        out_shape=jax.ShapeDtypeStruct((m, 2 * PROJ_TILE), BF16),
        compiler_params=_cparams(2),
        name="qk_proj",
    )(h, w_qk, qkw, gsum)


def _bias_tiles_kernel(table_ref, out_ref):
    h = pl.program_id(0)
    r = lax.broadcasted_iota(jnp.int32, (LANES, LANES), 0)
    c = lax.broadcasted_iota(jnp.int32, (LANES, LANES), 1)
    for d in range(3):
        rel = (d - 1) * LANES + c - r
        n = jnp.abs(rel)
        large = jnp.full((LANES, LANES), 8, jnp.int32)
        for t in REL_THRESHOLDS:
            large = large + jnp.where(n >= t, 1, 0)
        bucket = jnp.where(rel > 0, 16, 0) + jnp.where(n < 8, n, large)
        val = jnp.zeros((LANES, LANES), F32)
        for bkt in range(32):
            val = jnp.where(bucket == bkt, table_ref[bkt, h], val)
        out_ref[0, d] = val * LOG2E


def _bias_tiles(rel_bias):
    return pl.pallas_call(
        _bias_tiles_kernel,
        grid=(HEADS,),
        in_specs=[pl.BlockSpec(memory_space=pltpu.SMEM)],
        out_specs=pl.BlockSpec((1, 3, LANES, LANES), lambda h: (h, 0, 0, 0)),
        out_shape=jax.ShapeDtypeStruct((HEADS, 3, LANES, LANES), F32),
        compiler_params=_cparams(1),
        name="bias_tiles",
    )(rel_bias)


def _near_bias_tiles(tq, tk):
    nbq, nbk = tq // LANES, tk // LANES
    table = {}
    for a in range(tk // tq):
        for d in (-1, 0, 1):
            ob = d * nbk - a * nbq
            if ob + nbk - 1 >= -1 and ob - (nbq - 1) <= 1:
                table[(a, d)] = (2 + len(table), ob)
    return table


def _attn_kernel(lam_ref, table_ref, q_ref, k_ref, v_ref, bsm_ref, subln_ref, o_ref,
                 s_scr, p_scr, mcur_scr, alpha_scr, m_scr, acc_scr, bias_scr, *, tq, tk, seq, out_scale):
    nk = seq // tk
    n_stages = (seq // tq) * nk
    ratio = tk // tq
    near = _near_bias_tiles(tq, tk)
    h = pl.program_id(0)
    b = pl.program_id(1)

    @pl.when(b == 0)
    def _():
        c_lo = table_ref[15, h] * LOG2E
        c_hi = table_ref[31, h] * LOG2E
        ones = jnp.ones((tq, tk), F32)
        bias_scr[0] = ones * c_lo
        bias_scr[1] = ones * c_hi
        for idx, ob in near.values():
            for rb in range(tq // LANES):
                for cb in range(tk // LANES):
                    off = ob + cb - rb
                    if abs(off) <= 1:
                        blk = bsm_ref[0, off + 1]
                    else:
                        blk = jnp.ones((LANES, LANES), F32) * (c_lo if off < 0 else c_hi)
                    bias_scr[idx, rb * LANES:(rb + 1) * LANES, cb * LANES:(cb + 1) * LANES] = blk

    c_lo = table_ref[15, h] * LOG2E
    c_hi = table_ref[31, h] * LOG2E
    lane = lax.broadcasted_iota(jnp.int32, (tq, LANES), 1)
    zero = jnp.zeros((tq, LANES), BF16)
    ones_blk = jnp.ones((tk, LANES), BF16)

    first_near = {a: min(d for (a0, d) in near if a0 == a) for a in range(ratio)}
    n_e = max(sum(1 for (a0, _) in near if a0 == a) for a in range(ratio))

    def explicit_start(i):
        c = i // ratio
        a = i - c * ratio
        lo = c + first_near[0]
        for a0 in range(1, ratio):
            lo = jnp.where(a == a0, c + first_near[a0], lo)
        return jnp.clip(lo, 0, nk - n_e)

    def key_tile(i, pos):
        e0 = explicit_start(i)
        if pos < n_e:
            return e0 + pos, None
        k = pos - n_e
        j = k + jnp.where(k >= e0, n_e, 0)
        return j, jnp.where(j < e0, c_lo, c_hi)

    def bias_kind(i, j):
        c = i // ratio
        a = i - c * ratio
        d = j - c
        kind = jnp.where(d < 0, 0, 1)
        for (a0, d0), (idx, _) in near.items():
            kind = jnp.where((a == a0) & (d == d0), idx, kind)
        return kind

    def scores(i, pos, slot):
        j, const = key_tile(i, pos)
        q = q_ref[pl.ds(pl.multiple_of(i * tq, tq), tq), :]
        qs = jnp.concatenate([jnp.where(lane < HEAD_DIM, q, zero), jnp.where(lane >= HEAD_DIM, q, zero)], axis=0)
        k = k_ref[pl.ds(pl.multiple_of(j * tk, tk), tk), :]
        s = lax.dot_general(qs, k, (((1,), (1,)), ((), ())), preferred_element_type=F32)
        if const is None:
            kind = bias_kind(i, j)
            s = jnp.concatenate([s[0:tq] + bias_scr[kind], s[tq:2 * tq] + bias_scr[kind]], axis=0)
            row_max = jnp.max(s, axis=1, keepdims=True)
        else:
            row_max = jnp.max(s, axis=1, keepdims=True) + const
        s_scr[slot] = s
        mcur_scr[slot] = jnp.broadcast_to(row_max, (2 * tq, LANES))

    def softmax(i, pos, slot):
        _, const = key_tile(i, pos)
        if pos == 0:
            m_next = mcur_scr[slot]
        else:
            m_prev = m_scr[...]
            m_next = jnp.maximum(m_prev, mcur_scr[slot])
            alpha_scr[slot] = jnp.exp2(m_prev - m_next)
        shift = m_next if const is None else m_next - const
        p_scr[slot] = jnp.exp2(s_scr[slot] - pltpu.repeat(shift, tk // LANES, axis=1)).astype(BF16)
        m_scr[...] = m_next

    def values(i, pos, slot):
        j, _ = key_tile(i, pos)
        vaug = jnp.concatenate([v_ref[pl.ds(pl.multiple_of(j * tk, tk), tk), :], ones_blk], axis=1)
        pv = jnp.dot(p_scr[slot], vaug, preferred_element_type=F32)
        if pos == 0:
            acc_scr[...] = pv
        else:
            acc_scr[...] = acc_scr[...] * pltpu.repeat(alpha_scr[slot], 2, axis=1) + pv

    def finalize(i):
        acc = acc_scr[...]
        o12 = acc[:, 0:LANES] / acc[:, LANES:2 * LANES]
        o = o12[0:tq] - lam_ref[0] * o12[tq:2 * tq]
        o_ref[pl.ds(pl.multiple_of(i * tq, tq), tq), :] = (_rms(o, subln_ref[...]) * out_scale).astype(BF16)

    nq = seq // tq
    scores(0, 0, 0)
    softmax(0, 0, 0)
    scores(0, 1, 1)

    def body(g, carry):
        nxt = jnp.minimum(g + 1, nq - 1)
        for r in range(nk):
            values(g, r, r % 2)
            softmax(g if r + 1 < nk else nxt, (r + 1) % nk, (r + 1) % 2)
            scores(g if r + 2 < nk else nxt, (r + 2) % nk, r % 2)
        finalize(g)
        return carry

    lax.fori_loop(0, nq, body, 0)


def _diff_attn(qk, proj, bias_small, rel_bias, lam, subln_w, batch, seq, tq, tk, out_scale):
    assert seq % tk == 0 and tk % tq == 0 and (seq // tk) % 2 == 0
    qb, kb, vb = 0, HEADS, COL_V // LANES
    kern = functools.partial(_attn_kernel, tq=tq, tk=tk, seq=seq, out_scale=out_scale)
    n_bias = 2 + len(_near_bias_tiles(tq, tk))
    return pl.pallas_call(
        kern,
        grid=(HEADS, batch),
        in_specs=[
            pl.BlockSpec(memory_space=pltpu.SMEM),
            pl.BlockSpec(memory_space=pltpu.SMEM),
            pl.BlockSpec((seq, LANES), lambda h, b: (b, qb + h)),
            pl.BlockSpec((seq, LANES), lambda h, b: (b, kb + h)),
            pl.BlockSpec((seq, LANES), lambda h, b: (b, vb + h)),
            pl.BlockSpec((1, 3, LANES, LANES), lambda h, b: (h, 0, 0, 0)),
            pl.BlockSpec((1, LANES), lambda h, b: (0, 0)),
        ],
        out_specs=pl.BlockSpec((seq, LANES), lambda h, b: (b, h)),
        out_shape=jax.ShapeDtypeStruct((batch * seq, HEADS * LANES), BF16),
        scratch_shapes=[
            pltpu.VMEM((2, 2 * tq, tk), F32),
            pltpu.VMEM((2, 2 * tq, tk), BF16),
            pltpu.VMEM((2, 2 * tq, LANES), F32),
            pltpu.VMEM((2, 2 * tq, LANES), F32),
            pltpu.VMEM((2 * tq, LANES), F32),
            pltpu.VMEM((2 * tq, 2 * LANES), F32),
            pltpu.VMEM((n_bias, tq, tk), F32),
        ],
        compiler_params=_cparams(2, VMEM_LIMIT_ATTN),
        name="diff_attn",
    )(lam, rel_bias, qk, qk, proj, bias_small, subln_w)


def _tri(upper):
    r = lax.broadcasted_iota(jnp.int32, (CHUNK, CHUNK), 0)
    c = lax.broadcasted_iota(jnp.int32, (CHUNK, CHUNK), 1)
    return (c >= r) if upper else (c <= r)


def _tri_dot(mask, x):
    m = jnp.where(mask, 1.0, 0.0).astype(BF16)
    hi, lo = _split_bf16(x)
    return jnp.dot(m, hi, preferred_element_type=F32) + jnp.dot(m, lo, preferred_element_type=F32)


PREP_CHUNKS = 4


def _ssd_prep_kernel(dt_ref, dtb_ref, alog_ref, zt_ref, cs_ref, dd_ref, ww_ref):
    nh = SSM_HEADS
    lane = lax.broadcasted_iota(jnp.int32, (CHUNK, LANES), 1)
    tril = _tri(False)
    triu = _tri(True)
    for k in range(PREP_CHUNKS):
        rows = slice(k * CHUNK, (k + 1) * CHUNK)
        raw = dt_ref[rows, :] + dtb_ref[...]
        dt = jnp.maximum(raw, 0.0) + jnp.log1p(jnp.exp(-jnp.abs(raw)))
        dt = jnp.where(lane < 2 * nh, dt, 0.0)
        da = dt * -jnp.exp(alog_ref[...])
        acs = _tri_dot(tril, da)
        rcs = _tri_dot(triu, da)
        cs = jnp.where(lane < nh, acs, rcs)
        end = jnp.where(lane < nh, acs[CHUNK - 1:CHUNK, :], rcs[0:1, :])
        cs_ref[rows, :] = cs
        dd_ref[rows, :] = jnp.exp(cs)
        ww_ref[rows, :] = jnp.exp(end - cs) * dt
        zt_ref[rows, :] = jnp.where(lane < 2 * nh, dt, pltpu.roll(cs, 2 * nh, 1)).T


def _ssd_prep(dt_raw, dt_bias, a_log):
    m = dt_raw.shape[0]
    rows = PREP_CHUNKS * CHUNK
    blk = pl.BlockSpec((rows, LANES), lambda i: (i, 0))
    vec = pl.BlockSpec((1, LANES), lambda i: (0, 0))
    return pl.pallas_call(
        _ssd_prep_kernel,
        grid=(m // rows,),
        in_specs=[blk, vec, vec],
        out_specs=[blk] * 4,
        out_shape=[jax.ShapeDtypeStruct((m, LANES), F32)] * 4,
        compiler_params=_cparams(1),
        name="ssd_prep",
    )(dt_raw, dt_bias, a_log)


def _expand(x, e2_ref):
    hi, lo = _split_bf16(x)
    return jnp.dot(jnp.concatenate([hi, lo], axis=1), e2_ref[...], preferred_element_type=F32)


def _shift_rows(u, halo, offset):
    n = u.shape[0]
    if offset == 0:
        return u
    rolled = pltpu.roll(u, (-offset) % n, 0)
    row8 = lax.broadcasted_iota(jnp.int32, (8, u.shape[1]), 0)
    if offset < 0:
        hfix = pltpu.roll(halo, (-offset) % 8, 0)
        first = jnp.where(row8 < -offset, hfix, rolled[0:8])
        return jnp.concatenate([first, rolled[8:n]], axis=0)
    hfix = pltpu.roll(halo, (8 - offset) % 8, 0)
    last = jnp.where(row8 >= 8 - offset, hfix, rolled[n - 8:n])
    return jnp.concatenate([rolled[0:n - 8], last], axis=0)


def _shift_select(n, shifts):
    t = jnp.arange(len(shifts) * n)
    src = t % n + HALO + jnp.repeat(jnp.asarray(shifts), n)
    return (src[:, None] == jnp.arange(n + 2 * HALO)[None, :]).astype(BF16)


def _ssd_bwd_kernel(xc_ref, xp_ref, xn_ref, dd_ref, ww_ref, cw_ref, cb_ref, e2b_ref, sel_ref,
                    xact_ref, ybi_ref, state_scr, *, n_chunks):
    c = pl.program_id(1)
    cr = n_chunks - 1 - c

    @pl.when(c == 0)
    def _():
        state_scr[...] = jnp.zeros(state_scr.shape, F32)

    keep_prev = jnp.where(cr > 0, 1.0, 0.0).astype(BF16)
    keep_next = jnp.where(cr < n_chunks - 1, 1.0, 0.0).astype(BF16)
    slab = 512
    for sl in range(CONV_DIM // slab):
        cs = slice(sl * slab, (sl + 1) * slab)
        u_ext = jnp.concatenate([xp_ref[:, cs] * keep_prev, xc_ref[:, cs], xn_ref[:, cs] * keep_next], axis=0)
        sh = jnp.dot(sel_ref[...], u_ext, preferred_element_type=F32)
        acc = cb_ref[:, cs] + xc_ref[:, cs].astype(F32) * cw_ref[2:3, cs]
        for k, off in enumerate(SSM_SHIFTS):
            acc = acc + sh[k * CHUNK:(k + 1) * CHUNK] * cw_ref[off + 2:off + 3, cs]
        xact_ref[:, cs] = _silu(acc).astype(BF16)

    decay = _expand(dd_ref[...], e2b_ref)
    chunk_decay = decay[0:1, :]
    xs = xact_ref[:, 0:SSM_INNER].astype(F32)
    xw = (xs * _expand(ww_ref[...], e2b_ref)).astype(BF16)
    for g in range(SSM_GROUPS):
        gs = slice(g * GROUP_W, (g + 1) * GROUP_W)
        bg = xact_ref[:, SSM_INNER + g * SSM_STATE:SSM_INNER + (g + 1) * SSM_STATE]
        cg = xact_ref[:, SSM_INNER + (SSM_GROUPS + g) * SSM_STATE:SSM_INNER + (SSM_GROUPS + g + 1) * SSM_STATE]
        st = state_scr[:, gs]
        yb = jnp.dot(cg, st.astype(BF16), preferred_element_type=F32) * decay[:, gs]
        ybi_ref[:, gs] = yb.astype(BF16)
        upd = lax.dot_general(bg, xw[:, gs], (((0,), (0,)), ((), ())), preferred_element_type=F32)
        state_scr[:, gs] = st * chunk_decay[:, gs] + upd


def _ssd_bwd(proj, dd, ww, conv_w, conv_b, e2b, batch, seq):
    n_chunks = seq // CHUNK
    sub = CHUNK // HALO
    last16 = batch * seq // HALO - 1
    kern = functools.partial(_ssd_bwd_kernel, n_chunks=n_chunks)
    select = _shift_select(CHUNK, SSM_SHIFTS)

    def cur(b, c):
        return b * n_chunks + (n_chunks - 1 - c)

    return pl.pallas_call(
        kern,
        grid=(batch, n_chunks),
        in_specs=[
            pl.BlockSpec((CHUNK, CONV_DIM), lambda b, c: (cur(b, c), 0)),
            pl.BlockSpec((HALO, CONV_DIM), lambda b, c: (jnp.maximum(cur(b, c) * sub - 1, 0), 0)),
            pl.BlockSpec((HALO, CONV_DIM), lambda b, c: (jnp.minimum((cur(b, c) + 1) * sub, last16), 0)),
            pl.BlockSpec((CHUNK, LANES), lambda b, c: (cur(b, c), 0)),
            pl.BlockSpec((CHUNK, LANES), lambda b, c: (cur(b, c), 0)),
            pl.BlockSpec((4, CONV_DIM), lambda b, c: (0, 0)),
            pl.BlockSpec((1, CONV_DIM), lambda b, c: (0, 0)),
            pl.BlockSpec((2 * LANES, SSM_INNER), lambda b, c: (0, 0)),
            pl.BlockSpec(select.shape, lambda b, c: (0, 0)),
        ],
        out_specs=[
            pl.BlockSpec((CHUNK, CONV_DIM), lambda b, c: (cur(b, c), 0)),
            pl.BlockSpec((CHUNK, SSM_INNER), lambda b, c: (cur(b, c), 0)),
        ],
        out_shape=[
            jax.ShapeDtypeStruct((batch * seq, CONV_DIM), BF16),
            jax.ShapeDtypeStruct((batch * seq, SSM_INNER), BF16),
        ],
        scratch_shapes=[pltpu.VMEM((SSM_STATE, SSM_INNER), F32)],
        compiler_params=_cparams(2),
        name="ssd_bwd",
    )(proj, proj, proj, dd, ww, conv_w, conv_b, e2b, select)


def _ssd_fwd_kernel(xact_ref, z_ref, ybi_ref, zt_ref, cs_ref, dd_ref, ww_ref, dsk_ref, nw_ref, e2f_ref,
                    out_ref, state_scr):
    c = pl.program_id(1)

    @pl.when(c == 0)
    def _():
        state_scr[...] = jnp.zeros(state_scr.shape, F32)

    nh = SSM_HEADS
    cs = cs_ref[...]
    decay = _expand(dd_ref[...], e2f_ref)
    chunk_decay = decay[CHUNK - 1:CHUNK, :]
    xs_bf = xact_ref[:, 0:SSM_INNER]
    xs = xs_bf.astype(F32)
    xw = (xs * _expand(ww_ref[...], e2f_ref)).astype(BF16)
    lane_w = lax.broadcasted_iota(jnp.int32, (CHUNK, LANES), 1)
    row_i = lax.broadcasted_iota(jnp.int32, (CHUNK, CHUNK), 0)
    col_i = lax.broadcasted_iota(jnp.int32, (CHUNK, CHUNK), 1)
    lower = col_i < row_i
    upper = col_i > row_i

    for g in range(SSM_GROUPS):
        gs = slice(g * GROUP_W, (g + 1) * GROUP_W)
        bg = xact_ref[:, SSM_INNER + g * SSM_STATE:SSM_INNER + (g + 1) * SSM_STATE]
        cg = xact_ref[:, SSM_INNER + (SSM_GROUPS + g) * SSM_STATE:SSM_INNER + (SSM_GROUPS + g + 1) * SSM_STATE]
        cb = lax.dot_general(cg, bg, (((1,), (1,)), ((), ())), preferred_element_type=F32)
        ys = []
        for r in range(GROUP_W // SSM_HEAD_DIM):
            h = g * (GROUP_W // SSM_HEAD_DIM) + r
            seg_f = cs[:, h:h + 1] - zt_ref[2 * nh + h:2 * nh + h + 1, :]
            seg_b = cs[:, nh + h:nh + h + 1] - zt_ref[3 * nh + h:3 * nh + h + 1, :]
            dt_f = zt_ref[h:h + 1, :]
            dt_b = zt_ref[nh + h:nh + h + 1, :]
            dt_sel = jnp.where(lower, dt_f, jnp.where(upper, dt_b, dt_f + dt_b))
            mat = (cb * (jnp.exp(jnp.where(upper, seg_b, seg_f)) * dt_sel)).astype(BF16)
            pair = xs_bf[:, (h // 2) * LANES:(h // 2 + 1) * LANES]
            ys.append(jnp.dot(mat, pair, preferred_element_type=F32))
        y = jnp.concatenate([jnp.where(lane_w < SSM_HEAD_DIM, ys[0], ys[1]),
                             jnp.where(lane_w < SSM_HEAD_DIM, ys[2], ys[3])], axis=1)
        st = state_scr[:, gs]
        y = y + jnp.dot(cg, st.astype(BF16), preferred_element_type=F32) * decay[:, gs]
        upd = lax.dot_general(bg, xw[:, gs], (((0,), (0,)), ((), ())), preferred_element_type=F32)
        state_scr[:, gs] = st * chunk_decay[:, gs] + upd
        y = y + ybi_ref[:, gs].astype(F32) + dsk_ref[:, gs] * xs[:, gs]
        y = y * _silu(z_ref[:, gs].astype(F32))
        out_ref[:, gs] = _rms(y, nw_ref[:, gs]).astype(BF16)


def _ssd_fwd(xact, proj, ybi, zt, cs, dd, ww, d_skip, norm_w, e2f, batch, seq):
    n_chunks = seq // CHUNK
    zb = COL_Z // SSM_INNER
    small = pl.BlockSpec((CHUNK, LANES), lambda b, c: (b * n_chunks + c, 0))
    return pl.pallas_call(
        _ssd_fwd_kernel,
        grid=(batch, n_chunks),
        in_specs=[
            pl.BlockSpec((CHUNK, CONV_DIM), lambda b, c: (b * n_chunks + c, 0)),
            pl.BlockSpec((CHUNK, SSM_INNER), lambda b, c: (b * n_chunks + c, zb)),
            pl.BlockSpec((CHUNK, SSM_INNER), lambda b, c: (b * n_chunks + c, 0)),
            small, small, small, small,
            pl.BlockSpec((1, SSM_INNER), lambda b, c: (0, 0)),
            pl.BlockSpec((1, SSM_INNER), lambda b, c: (0, 0)),
            pl.BlockSpec((2 * LANES, SSM_INNER), lambda b, c: (0, 0)),
        ],
        out_specs=pl.BlockSpec((CHUNK, SSM_INNER), lambda b, c: (b * n_chunks + c, 0)),
        out_shape=jax.ShapeDtypeStruct((batch * seq, SSM_INNER), BF16),
        scratch_shapes=[pltpu.VMEM((SSM_STATE, SSM_INNER), F32)],
        compiler_params=_cparams(2),
        name="ssd_fwd",
    )(xact, proj, ybi, zt, cs, dd, ww, d_skip, norm_w, e2f)


def _mix_out_kernel(x_ref, ao_ref, so_ref, ga_ref, gs_ref, wa_ref, ws_ref, wo_ref, n2_ref, x1_ref, h2_ref):
    attn = jnp.dot(ao_ref[...], wa_ref[...], preferred_element_type=F32)
    ssd = jnp.dot(so_ref[...], ws_ref[...], preferred_element_type=F32)
    mixed = (jax.nn.sigmoid(ga_ref[...].astype(F32)) * attn
             + jax.nn.sigmoid(gs_ref[...].astype(F32)) * ssd)
    x1 = x_ref[...] + jnp.dot(mixed.astype(BF16), wo_ref[...], preferred_element_type=F32)
    x1_ref[...] = x1
    h2_ref[...] = _rms(x1, n2_ref[...]).astype(BF16)


def _mix_out(x2d, attn_o, ssd_o, proj, w_attn, w_ssm, w_out, norm2_w, tm):
    m, d = x2d.shape
    gb = COL_GATE // d
    return pl.pallas_call(
        _mix_out_kernel,
        grid=(m // tm,),
        in_specs=[
            pl.BlockSpec((tm, d), lambda i: (i, 0)),
            pl.BlockSpec((tm, attn_o.shape[1]), lambda i: (i, 0)),
            pl.BlockSpec((tm, ssd_o.shape[1]), lambda i: (i, 0)),
            pl.BlockSpec((tm, d), lambda i: (i, gb)),
            pl.BlockSpec((tm, d), lambda i: (i, gb + 1)),
            pl.BlockSpec(w_attn.shape, lambda i: (0, 0)),
            pl.BlockSpec(w_ssm.shape, lambda i: (0, 0)),
            pl.BlockSpec(w_out.shape, lambda i: (0, 0)),
            pl.BlockSpec((1, d), lambda i: (0, 0)),
        ],
        out_specs=[pl.BlockSpec((tm, d), lambda i: (i, 0)), pl.BlockSpec((tm, d), lambda i: (i, 0))],
        out_shape=[jax.ShapeDtypeStruct((m, d), F32), jax.ShapeDtypeStruct((m, d), BF16)],
        compiler_params=_cparams(1),
        name="mix_out",
    )(x2d, attn_o, ssd_o, proj, proj, w_attn, w_ssm, w_out, norm2_w)


FFN_CW = 256


def _ffn_kernel(x1_ref, h_ref, hp_ref, hn_ref, wu_ref, cw_ref, cb_ref, wd_ref, o_ref, act_scr, *, tiles_per_seq):
    i = pl.program_id(0)
    tm = h_ref.shape[0]
    f = wd_ref.shape[0]
    pos = i % tiles_per_seq
    keep_prev = jnp.where(pos > 0, 1.0, 0.0)
    keep_next = jnp.where(pos < tiles_per_seq - 1, 1.0, 0.0)
    h_ext = jnp.concatenate([hp_ref[...], h_ref[...], hn_ref[...]], axis=0)

    def conv(col):
        cs = slice(col, col + FFN_CW)
        u_ext = jnp.dot(h_ext, wu_ref[:, cs], preferred_element_type=F32)
        u = u_ext[HALO:HALO + tm]
        up = u_ext[HALO - 8:HALO] * keep_prev
        un = u_ext[HALO + tm:HALO + tm + 8] * keep_next
        return (cb_ref[:, cs] + _shift_rows(u, up, -1) * cw_ref[0:1, cs] + u * cw_ref[1:2, cs]
                + _shift_rows(u, un, 1) * cw_ref[2:3, cs])

    for c in range(f // FFN_CW):
        act = _silu(conv(c * FFN_CW)) * conv(f + c * FFN_CW)
        act_scr[:, c * FFN_CW:(c + 1) * FFN_CW] = act.astype(BF16)
    o_ref[...] = x1_ref[...] + jnp.dot(act_scr[...], wd_ref[...], preferred_element_type=F32)


def _ffn(x1, h2, w_up, conv_w, conv_b, w_down, seq, tm):
    m, d = x1.shape
    f = w_down.shape[0]
    sub = tm // HALO
    last = m // HALO - 1
    kern = functools.partial(_ffn_kernel, tiles_per_seq=seq // tm)
    resident = pl.Buffered(1)
    return pl.pallas_call(
        kern,
        grid=(m // tm,),
        in_specs=[
            pl.BlockSpec((tm, d), lambda i: (i, 0)),
            pl.BlockSpec((tm, d), lambda i: (i, 0)),
            pl.BlockSpec((HALO, d), lambda i: (jnp.maximum(i * sub - 1, 0), 0)),
            pl.BlockSpec((HALO, d), lambda i: (jnp.minimum((i + 1) * sub, last), 0)),
            pl.BlockSpec(w_up.shape, lambda i: (0, 0), pipeline_mode=resident),
            pl.BlockSpec(conv_w.shape, lambda i: (0, 0)),
            pl.BlockSpec(conv_b.shape, lambda i: (0, 0)),
            pl.BlockSpec(w_down.shape, lambda i: (0, 0), pipeline_mode=resident),
        ],
        out_specs=pl.BlockSpec((tm, d), lambda i: (i, 0)),
        out_shape=jax.ShapeDtypeStruct((m, d), F32),
        scratch_shapes=[pltpu.VMEM((tm, f), BF16)],
        compiler_params=_cparams(1),
        name="ffn",
    )(x1, h2, h2, h2, w_up, conv_w, conv_b, w_down)


def _head_expand(first_row):
    rows = jnp.arange(2 * LANES)[:, None] % LANES
    cols = jnp.arange(SSM_INNER)[None, :] // SSM_HEAD_DIM
    return (rows == cols + first_row).astype(BF16)


def _layer(x2d, batch, seq, lambda_init, p):
    d = x2d.shape[1]
    w_in = p["w_in"]
    sizes = (HEADS * 2 * HEAD_DIM,) * 3 + (SSM_INNER, CONV_DIM, 2 * SSM_HEADS, 2 * d)
    offs = [0]
    for s in sizes:
        offs.append(offs[-1] + s)
    wq, wk, wv, wz, wxbc, wdt, wg = (w_in[:, offs[n]:offs[n + 1]] for n in range(7))
    w_main = jnp.concatenate([wxbc, wz, wg, wv], axis=1).astype(BF16)
    w_qk = jnp.concatenate([wq, wk], axis=1).astype(BF16)
    w_dt = jnp.pad(wdt, ((0, 0), (0, LANES - wdt.shape[1]))).astype(BF16)
    qkw = jnp.stack([jnp.tile(p["q_norm_w"], 2 * HEADS) * (HEAD_DIM ** -0.5 * LOG2E),
                     jnp.tile(p["k_norm_w"], 2 * HEADS)])[:, None, :]
    gi = jnp.arange(MXU_DIM) // HEAD_DIM
    gsum = (gi[:, None] == gi[None, :]).astype(BF16)

    proj, dt_raw, h1 = _in_proj(x2d, p["norm1_w"][None, :], w_main, w_dt, tm=1024)
    qk = _qk_proj(h1, w_qk, qkw, gsum, tm=1024)

    lam = (jnp.exp(jnp.sum(p["lambda_q1"] * p["lambda_k1"])) - jnp.exp(jnp.sum(p["lambda_q2"] * p["lambda_k2"]))
           + lambda_init).reshape(1).astype(F32)
    bias_small = _bias_tiles(p["rel_bias"])
    attn_o = _diff_attn(qk, proj, bias_small, p["rel_bias"], lam, p["subln_w"][None, :], batch, seq,
                        tq=512, tk=1024, out_scale=1.0 - lambda_init)

    pad = LANES - 2 * SSM_HEADS
    dt_bias = jnp.pad(jnp.concatenate([p["dt_bias_f"], p["dt_bias_b"]]), (0, pad))[None, :]
    a_log = jnp.pad(jnp.concatenate([p["a_log_f"], p["a_log_b"]]), (0, pad))[None, :]
    zt, cs, dd, ww = _ssd_prep(dt_raw, dt_bias, a_log)
    xact, ybi = _ssd_bwd(proj, dd, ww, p["ssm_conv_w"], p["ssm_conv_b"][None, :],
                         _head_expand(SSM_HEADS), batch, seq)
    ssd_o = _ssd_fwd(xact, proj, ybi, zt, cs, dd, ww,
                     jnp.repeat(p["d_skip"], SSM_HEAD_DIM)[None, :], p["ssm_norm_w"][None, :],
                     _head_expand(0), batch, seq)

    x1, h2 = _mix_out(x2d, attn_o, ssd_o, proj, p["w_attn_out"].astype(BF16), p["w_ssm_out"].astype(BF16),
                      p["w_out"].astype(BF16), p["norm2_w"][None, :], tm=512)

    return _ffn(x1, h2, p["w_ffn_up"].astype(BF16), p["ffn_conv_w"], p["ffn_conv_b"][None, :],
                p["w_ffn_down"].astype(BF16), seq, tm=512)


def kernel(x, norm1_w, w_in, q_norm_w, k_norm_w, rel_bias, lambda_q1, lambda_k1, lambda_q2, lambda_k2, subln_w, w_attn_out, ssm_conv_w, ssm_conv_b, dt_bias_f, a_log_f, dt_bias_b, a_log_b, d_skip, ssm_norm_w, w_ssm_out, w_out, norm2_w, w_ffn_up, ffn_conv_w, ffn_conv_b, w_ffn_down):
    batch, seq, d = x.shape
    layered = dict(norm1_w=norm1_w, w_in=w_in, q_norm_w=q_norm_w, k_norm_w=k_norm_w, lambda_q1=lambda_q1,
                   lambda_k1=lambda_k1, lambda_q2=lambda_q2, lambda_k2=lambda_k2, subln_w=subln_w,
                   w_attn_out=w_attn_out, ssm_conv_w=ssm_conv_w, ssm_conv_b=ssm_conv_b, dt_bias_f=dt_bias_f,
                   a_log_f=a_log_f, dt_bias_b=dt_bias_b, a_log_b=a_log_b, d_skip=d_skip, ssm_norm_w=ssm_norm_w,
                   w_ssm_out=w_ssm_out, w_out=w_out, norm2_w=norm2_w, w_ffn_up=w_ffn_up, ffn_conv_w=ffn_conv_w,
                   ffn_conv_b=ffn_conv_b, w_ffn_down=w_ffn_down)
    x2d = x.reshape(batch * seq, d)
    for layer in range(norm1_w.shape[0]):
        p = {k: v[layer] for k, v in layered.items()}
        p["rel_bias"] = rel_bias
        lambda_init = 0.8 - 0.6 * math.exp(-0.3 * layer)
        x2d = _layer(x2d, batch, seq, lambda_init, p)
    return x2d.reshape(batch, seq, d)
```

```python
import functools
import math

import jax
import jax.numpy as jnp
from jax import lax
from jax.experimental import pallas as pl
from jax.experimental.pallas import tpu as pltpu

F32 = jnp.float32
BF16 = jnp.bfloat16

HEADS = 8
HEAD_DIM = 64
SSM_HEADS = 32
SSM_HEAD_DIM = 64
SSM_GROUPS = 8
SSM_STATE = 128
SSM_INNER = SSM_HEADS * SSM_HEAD_DIM
GROUP_W = SSM_INNER // SSM_GROUPS
CONV_DIM = SSM_INNER + 2 * SSM_GROUPS * SSM_STATE
CHUNK = 128
SSM_SHIFTS = (-2, -1, 1)
HALO = 16
RMS_EPS = 1e-6
LOG2E = 1.4426950408889634
NEG_BIG = -1e30

REL_THRESHOLDS = (12, 16, 23, 32, 46, 64, 91)
REL_FAR = 129

LANES = 128
VMEM_LIMIT = 52 * 1024 * 1024
VMEM_LIMIT_ATTN = 60 * 1024 * 1024

MXU_DIM = 256

COL_XBC = 0
COL_Z = 4096
COL_GATE = 6144
COL_V = 8192
PROJ_W = 9216
PROJ_TILE = 1024


def _cparams(n_axes, vmem_limit=VMEM_LIMIT):
    return pltpu.CompilerParams(dimension_semantics=("arbitrary",) * n_axes, vmem_limit_bytes=vmem_limit)


def _rms(x, w):
    return x * lax.rsqrt(jnp.mean(x * x, axis=-1, keepdims=True) + RMS_EPS) * w


def _split_bf16(x):
    hi = x.astype(BF16)
    lo = (x - hi.astype(F32)).astype(BF16)
    return hi, lo


def _silu(x):
    return x / (1.0 + jnp.exp(-x))


def _in_proj_kernel(x_ref, nw_ref, w_ref, wdt_ref, proj_ref, dt_ref, h_ref):
    @pl.when(pl.program_id(1) == 0)
    def _():
        h = _rms(x_ref[...], nw_ref[...]).astype(BF16)
        h_ref[...] = h
        dt_ref[...] = jnp.dot(h, wdt_ref[...], preferred_element_type=F32)

    proj_ref[...] = jnp.dot(h_ref[...], w_ref[...], preferred_element_type=F32).astype(BF16)


def _in_proj(x2d, norm_w, w_main, w_dt, tm):
    m, d = x2d.shape
    return pl.pallas_call(
        _in_proj_kernel,
        grid=(m // tm, PROJ_W // PROJ_TILE),
        in_specs=[
            pl.BlockSpec((tm, d), lambda i, j: (i, 0)),
            pl.BlockSpec((1, d), lambda i, j: (0, 0)),
            pl.BlockSpec((d, PROJ_TILE), lambda i, j: (0, j)),
            pl.BlockSpec((d, LANES), lambda i, j: (0, 0)),
        ],
        out_specs=[
            pl.BlockSpec((tm, PROJ_TILE), lambda i, j: (i, j)),
            pl.BlockSpec((tm, LANES), lambda i, j: (i, 0)),
            pl.BlockSpec((tm, d), lambda i, j: (i, 0)),
        ],
        out_shape=[
            jax.ShapeDtypeStruct((m, PROJ_W), BF16),
            jax.ShapeDtypeStruct((m, LANES), F32),
            jax.ShapeDtypeStruct((m, d), BF16),
        ],
        compiler_params=_cparams(2),
        name="in_proj",
    )(x2d, norm_w, w_main, w_dt)


def _qk_proj_kernel(h_ref, w_ref, qkw_ref, gsum_ref, o_ref):
    acc = jnp.dot(h_ref[...], w_ref[...], preferred_element_type=F32)
    for c in range(PROJ_TILE // MXU_DIM):
        cs = slice(c * MXU_DIM, (c + 1) * MXU_DIM)
        a = acc[:, cs]
        ss = jnp.dot((a * a).astype(BF16), gsum_ref[...], preferred_element_type=F32)
        o_ref[:, cs] = (a * lax.rsqrt(ss * (1.0 / HEAD_DIM) + RMS_EPS) * qkw_ref[0, :, cs]).astype(BF16)


def _qk_proj(h, w_qk, qkw, gsum, tm):
    m, d = h.shape
    return pl.pallas_call(
        _qk_proj_kernel,
        grid=(m // tm, 2),
        in_specs=[
            pl.BlockSpec((tm, d), lambda i, j: (i, 0)),
            pl.BlockSpec((d, PROJ_TILE), lambda i, j: (0, j)),
            pl.BlockSpec((1, 1, PROJ_TILE), lambda i, j: (j, 0, 0)),
            pl.BlockSpec((MXU_DIM, MXU_DIM), lambda i, j: (0, 0)),
        ],
        out_specs=pl.BlockSpec((tm, PROJ_TILE), lambda i, j: (i, j)),
        out_shape=jax.ShapeDtypeStruct((m, 2 * PROJ_TILE), BF16),
        compiler_params=_cparams(2),
        name="qk_proj",
    )(h, w_qk, qkw, gsum)


def _bias_tiles_kernel(table_ref, out_ref):
    h = pl.program_id(0)
    r = lax.broadcasted_iota(jnp.int32, (LANES, LANES), 0)
    c = lax.broadcasted_iota(jnp.int32, (LANES, LANES), 1)
    for d in range(3):
        rel = (d - 1) * LANES + c - r
        n = jnp.abs(rel)
        large = jnp.full((LANES, LANES), 8, jnp.int32)
        for t in REL_THRESHOLDS:
            large = large + jnp.where(n >= t, 1, 0)
        bucket = jnp.where(rel > 0, 16, 0) + jnp.where(n < 8, n, large)
        val = jnp.zeros((LANES, LANES), F32)
        for bkt in range(32):
            val = jnp.where(bucket == bkt, table_ref[bkt, h], val)
        out_ref[0, d] = val * LOG2E


def _bias_tiles(rel_bias):
    return pl.pallas_call(
        _bias_tiles_kernel,
        grid=(HEADS,),
        in_specs=[pl.BlockSpec(memory_space=pltpu.SMEM)],
        out_specs=pl.BlockSpec((1, 3, LANES, LANES), lambda h: (h, 0, 0, 0)),
        out_shape=jax.ShapeDtypeStruct((HEADS, 3, LANES, LANES), F32),
        compiler_params=_cparams(1),
        name="bias_tiles",
    )(rel_bias)


def _near_bias_tiles(tq, tk):
    nbq, nbk = tq // LANES, tk // LANES
    table = {}
    for a in range(tk // tq):
        for d in (-1, 0, 1):
            ob = d * nbk - a * nbq
            if ob + nbk - 1 >= -1 and ob - (nbq - 1) <= 1:
                table[(a, d)] = (2 + len(table), ob)
    return table


def _attn_kernel(lam_ref, table_ref, q_ref, k_ref, v_ref, bsm_ref, subln_ref, o_ref,
                 s_scr, p_scr, mcur_scr, alpha_scr, m_scr, acc_scr, bias_scr, *, tq, tk, seq, out_scale):
    nk = seq // tk
    ratio = tk // tq
    near = _near_bias_tiles(tq, tk)
    h = pl.program_id(0)
    b = pl.program_id(1)
    c_lo = table_ref[15, h] * LOG2E
    c_hi = table_ref[31, h] * LOG2E

    @pl.when(b == 0)
    def _():
        ones = jnp.ones((tq, tk), F32)
        bias_scr[0] = ones * c_lo
        bias_scr[1] = ones * c_hi
        for idx, ob in near.values():
            for rb in range(tq // LANES):
                for cb in range(tk // LANES):
                    off = ob + cb - rb
                    if abs(off) <= 1:
                        blk = bsm_ref[0, off + 1]
                    else:
                        blk = jnp.ones((LANES, LANES), F32) * (c_lo if off < 0 else c_hi)
                    bias_scr[idx, rb * LANES:(rb + 1) * LANES, cb * LANES:(cb + 1) * LANES] = blk

    lane = lax.broadcasted_iota(jnp.int32, (tq, LANES), 1)
    zero = jnp.zeros((tq, LANES), BF16)
    ones_blk = jnp.ones((tk, LANES), BF16)

    first_near = {a: min(d for (a0, d) in near if a0 == a) for a in range(ratio)}
    n_e = max(sum(1 for (a0, _) in near if a0 == a) for a in range(ratio))

    def explicit_start(i):
        c = i // ratio
        a = i - c * ratio
        lo = c + first_near[0]
        for a0 in range(1, ratio):
            lo = jnp.where(a == a0, c + first_near[a0], lo)
        return jnp.clip(lo, 0, nk - n_e)

    def key_tile(i, pos):
        e0 = explicit_start(i)
        if pos < n_e:
            return e0 + pos, None
        k = pos - n_e
        j = k + jnp.where(k >= e0, n_e, 0)
        return j, jnp.where(j < e0, c_lo, c_hi)

    def bias_kind(i, j):
        c = i // ratio
        a = i - c * ratio
        d = j - c
        kind = jnp.where(d < 0, 0, 1)
        for (a0, d0), (idx, _) in near.items():
            kind = jnp.where((a == a0) & (d == d0), idx, kind)
        return kind

    def scores(i, pos, slot):
        j, const = key_tile(i, pos)
        q = q_ref[pl.ds(pl.multiple_of(i * tq, tq), tq), :]
        qs = jnp.concatenate([jnp.where(lane < HEAD_DIM, q, zero), jnp.where(lane >= HEAD_DIM, q, zero)], axis=0)
        k = k_ref[pl.ds(pl.multiple_of(j * tk, tk), tk), :]
        s = lax.dot_general(qs, k, (((1,), (1,)), ((), ())), preferred_element_type=F32)
        if const is None:
            kind = bias_kind(i, j)
            s = jnp.concatenate([s[0:tq] + bias_scr[kind], s[tq:2 * tq] + bias_scr[kind]], axis=0)
            row_max = jnp.max(s, axis=1, keepdims=True)
        else:
            row_max = jnp.max(s, axis=1, keepdims=True) + const
        s_scr[slot] = s
        mcur_scr[slot] = jnp.broadcast_to(row_max, (2 * tq, LANES))

    def softmax(i, pos, slot):
        _, const = key_tile(i, pos)
        if pos == 0:
            m_next = mcur_scr[slot]
        else:
            m_prev = m_scr[...]
            m_next = jnp.maximum(m_prev, mcur_scr[slot])
            alpha_scr[slot] = jnp.exp2(m_prev - m_next)
        shift = m_next if const is None else m_next - const
        p_scr[slot] = jnp.exp2(s_scr[slot] - pltpu.repeat(shift, tk // LANES, axis=1)).astype(BF16)
        m_scr[...] = m_next

    def values(i, pos, slot):
        j, _ = key_tile(i, pos)
        vaug = jnp.concatenate([v_ref[pl.ds(pl.multiple_of(j * tk, tk), tk), :], ones_blk], axis=1)
        pv = jnp.dot(p_scr[slot], vaug, preferred_element_type=F32)
        if pos == 0:
            acc_scr[...] = pv
        else:
            acc_scr[...] = acc_scr[...] * pltpu.repeat(alpha_scr[slot], 2, axis=1) + pv

    def finalize(i):
        acc = acc_scr[...]
        o12 = acc[:, 0:LANES] / acc[:, LANES:2 * LANES]
        o = o12[0:tq] - lam_ref[0] * o12[tq:2 * tq]
        o_ref[pl.ds(pl.multiple_of(i * tq, tq), tq), :] = (_rms(o, subln_ref[...]) * out_scale).astype(BF16)

    nq = seq // tq
    scores(0, 0, 0)
    softmax(0, 0, 0)
    scores(0, 1, 1)

    def body(g, carry):
        nxt = jnp.minimum(g + 1, nq - 1)
        for r in range(nk):
            values(g, r, r % 2)
            softmax(g if r + 1 < nk else nxt, (r + 1) % nk, (r + 1) % 2)
            scores(g if r + 2 < nk else nxt, (r + 2) % nk, r % 2)
        finalize(g)
        return carry

    lax.fori_loop(0, nq, body, 0)


def _diff_attn(qk, proj, bias_small, rel_bias, lam, subln_w, batch, seq, tq, tk, out_scale):
    assert seq % tk == 0 and tk % tq == 0 and (seq // tk) % 2 == 0
    qb, kb, vb = 0, HEADS, COL_V // LANES
    kern = functools.partial(_attn_kernel, tq=tq, tk=tk, seq=seq, out_scale=out_scale)
    n_bias = 2 + len(_near_bias_tiles(tq, tk))
    return pl.pallas_call(
        kern,
        grid=(HEADS, batch),
        in_specs=[
            pl.BlockSpec(memory_space=pltpu.SMEM),
            pl.BlockSpec(memory_space=pltpu.SMEM),
            pl.BlockSpec((seq, LANES), lambda h, b: (b, qb + h)),
            pl.BlockSpec((seq, LANES), lambda h, b: (b, kb + h)),
            pl.BlockSpec((seq, LANES), lambda h, b: (b, vb + h)),
            pl.BlockSpec((1, 3, LANES, LANES), lambda h, b: (h, 0, 0, 0)),
            pl.BlockSpec((1, LANES), lambda h, b: (0, 0)),
        ],
        out_specs=pl.BlockSpec((seq, LANES), lambda h, b: (b, h)),
        out_shape=jax.ShapeDtypeStruct((batch * seq, HEADS * LANES), BF16),
        scratch_shapes=[
            pltpu.VMEM((2, 2 * tq, tk), F32),
            pltpu.VMEM((2, 2 * tq, tk), BF16),
            pltpu.VMEM((2, 2 * tq, LANES), F32),
            pltpu.VMEM((2, 2 * tq, LANES), F32),
            pltpu.VMEM((2 * tq, LANES), F32),
            pltpu.VMEM((2 * tq, 2 * LANES), F32),
            pltpu.VMEM((n_bias, tq, tk), F32),
        ],
        compiler_params=_cparams(2, VMEM_LIMIT_ATTN),
        name="diff_attn",
    )(lam, rel_bias, qk, qk, proj, bias_small, subln_w)


def _tri(upper):
    r = lax.broadcasted_iota(jnp.int32, (CHUNK, CHUNK), 0)
    c = lax.broadcasted_iota(jnp.int32, (CHUNK, CHUNK), 1)
    return (c >= r) if upper else (c <= r)


def _tri_dot(mask, x):
    m = jnp.where(mask, 1.0, 0.0).astype(BF16)
    hi, lo = _split_bf16(x)
    return jnp.dot(m, hi, preferred_element_type=F32) + jnp.dot(m, lo, preferred_element_type=F32)


PREP_CHUNKS = 16


def _ssd_prep_kernel(dt_ref, dtb_ref, alog_ref, zt_ref, cs_ref, dd_ref, ww_ref):
    nh = SSM_HEADS
    lane = lax.broadcasted_iota(jnp.int32, (CHUNK, LANES), 1)
    tril = _tri(False)
    triu = _tri(True)
    for k in range(PREP_CHUNKS):
        rows = slice(k * CHUNK, (k + 1) * CHUNK)
        raw = dt_ref[rows, :] + dtb_ref[...]
        dt = jnp.maximum(raw, 0.0) + jnp.log1p(jnp.exp(-jnp.abs(raw)))
        dt = jnp.where(lane < 2 * nh, dt, 0.0)
        da = dt * -jnp.exp(alog_ref[...])
        acs = _tri_dot(tril, da)
        rcs = _tri_dot(triu, da)
        cs = jnp.where(lane < nh, acs, rcs)
        end = jnp.where(lane < nh, acs[CHUNK - 1:CHUNK, :], rcs[0:1, :])
        cs_ref[rows, :] = cs
        dd_ref[rows, :] = jnp.exp(cs)
        ww_ref[rows, :] = jnp.exp(end - cs) * dt
        zt_ref[rows, :] = jnp.where(lane < 2 * nh, dt, pltpu.roll(cs, 2 * nh, 1)).T


def _ssd_prep(dt_raw, dt_bias, a_log):
    m = dt_raw.shape[0]
    rows = PREP_CHUNKS * CHUNK
    blk = pl.BlockSpec((rows, LANES), lambda i: (i, 0))
    vec = pl.BlockSpec((1, LANES), lambda i: (0, 0))
    return pl.pallas_call(
        _ssd_prep_kernel,
        grid=(m // rows,),
        in_specs=[blk, vec, vec],
        out_specs=[blk] * 4,
        out_shape=[jax.ShapeDtypeStruct((m, LANES), F32)] * 4,
        compiler_params=_cparams(1),
        name="ssd_prep",
    )(dt_raw, dt_bias, a_log)


def _expand(x, e2_ref):
    hi, lo = _split_bf16(x)
    return jnp.dot(jnp.concatenate([hi, lo], axis=1), e2_ref[...], preferred_element_type=F32)


def _shift_rows(u, halo, offset):
    n = u.shape[0]
    if offset == 0:
        return u
    rolled = pltpu.roll(u, (-offset) % n, 0)
    row8 = lax.broadcasted_iota(jnp.int32, (8, u.shape[1]), 0)
    if offset < 0:
        hfix = pltpu.roll(halo, (-offset) % 8, 0)
        first = jnp.where(row8 < -offset, hfix, rolled[0:8])
        return jnp.concatenate([first, rolled[8:n]], axis=0)
    hfix = pltpu.roll(halo, (8 - offset) % 8, 0)
    last = jnp.where(row8 >= 8 - offset, hfix, rolled[n - 8:n])
    return jnp.concatenate([rolled[0:n - 8], last], axis=0)


def _shift_select(n, shifts):
    t = jnp.arange(len(shifts) * n)
    src = t % n + HALO + jnp.repeat(jnp.asarray(shifts), n)
    return (src[:, None] == jnp.arange(n + 2 * HALO)[None, :]).astype(BF16)


SCAN_CHUNKS = 2


def _ssd_bwd_kernel(xc_ref, xp_ref, xn_ref, dd_ref, ww_ref, cw_ref, cb_ref, e2b_ref, sel_ref,
                    xact_ref, ybi_ref, state_scr, *, n_blocks):
    c = pl.program_id(1)
    cr = n_blocks - 1 - c

    @pl.when(c == 0)
    def _():
        state_scr[...] = jnp.zeros(state_scr.shape, F32)

    keep_prev = jnp.where(cr > 0, 1.0, 0.0).astype(BF16)
    keep_next = jnp.where(cr < n_blocks - 1, 1.0, 0.0).astype(BF16)
    slab = 512
    for sl in range(CONV_DIM // slab):
        cs = slice(sl * slab, (sl + 1) * slab)
        ext = jnp.concatenate([xp_ref[:, cs] * keep_prev, xc_ref[:, cs], xn_ref[:, cs] * keep_next], axis=0)
        for k in range(SCAN_CHUNKS):
            rows = slice(k * CHUNK, (k + 1) * CHUNK)
            sh = jnp.dot(sel_ref[...], ext[k * CHUNK:(k + 1) * CHUNK + 2 * HALO], preferred_element_type=F32)
            acc = cb_ref[:, cs] + xc_ref[rows, cs].astype(F32) * cw_ref[2:3, cs]
            for j, off in enumerate(SSM_SHIFTS):
                acc = acc + sh[j * CHUNK:(j + 1) * CHUNK] * cw_ref[off + 2:off + 3, cs]
            xact_ref[rows, cs] = _silu(acc).astype(BF16)

    for k in reversed(range(SCAN_CHUNKS)):
        rows = slice(k * CHUNK, (k + 1) * CHUNK)
        decay = _expand(dd_ref[rows, :], e2b_ref)
        chunk_decay = decay[0:1, :]
        xs = xact_ref[rows, 0:SSM_INNER].astype(F32)
        xw = (xs * _expand(ww_ref[rows, :], e2b_ref)).astype(BF16)
        for g in range(SSM_GROUPS):
            gs = slice(g * GROUP_W, (g + 1) * GROUP_W)
            bg = xact_ref[rows, SSM_INNER + g * SSM_STATE:SSM_INNER + (g + 1) * SSM_STATE]
            cg = xact_ref[rows, SSM_INNER + (SSM_GROUPS + g) * SSM_STATE:SSM_INNER + (SSM_GROUPS + g + 1) * SSM_STATE]
            st = state_scr[:, gs]
            yb = jnp.dot(cg, st.astype(BF16), preferred_element_type=F32) * decay[:, gs]
            ybi_ref[rows, gs] = yb.astype(BF16)
            upd = lax.dot_general(bg, xw[:, gs], (((0,), (0,)), ((), ())), preferred_element_type=F32)
            state_scr[:, gs] = st * chunk_decay[:, gs] + upd


def _ssd_bwd(proj, dd, ww, conv_w, conv_b, e2b, batch, seq):
    rows = SCAN_CHUNKS * CHUNK
    n_blocks = seq // rows
    sub = rows // HALO
    last16 = batch * seq // HALO - 1
    kern = functools.partial(_ssd_bwd_kernel, n_blocks=n_blocks)
    select = _shift_select(CHUNK, SSM_SHIFTS)

    def cur(b, c):
        return b * n_blocks + (n_blocks - 1 - c)

    return pl.pallas_call(
        kern,
        grid=(batch, n_blocks),
        in_specs=[
            pl.BlockSpec((rows, CONV_DIM), lambda b, c: (cur(b, c), 0)),
            pl.BlockSpec((HALO, CONV_DIM), lambda b, c: (jnp.maximum(cur(b, c) * sub - 1, 0), 0)),
            pl.BlockSpec((HALO, CONV_DIM), lambda b, c: (jnp.minimum((cur(b, c) + 1) * sub, last16), 0)),
            pl.BlockSpec((rows, LANES), lambda b, c: (cur(b, c), 0)),
            pl.BlockSpec((rows, LANES), lambda b, c: (cur(b, c), 0)),
            pl.BlockSpec((4, CONV_DIM), lambda b, c: (0, 0)),
            pl.BlockSpec((1, CONV_DIM), lambda b, c: (0, 0)),
            pl.BlockSpec((2 * LANES, SSM_INNER), lambda b, c: (0, 0)),
            pl.BlockSpec(select.shape, lambda b, c: (0, 0)),
        ],
        out_specs=[
            pl.BlockSpec((rows, CONV_DIM), lambda b, c: (cur(b, c), 0)),
            pl.BlockSpec((rows, SSM_INNER), lambda b, c: (cur(b, c), 0)),
        ],
        out_shape=[
            jax.ShapeDtypeStruct((batch * seq, CONV_DIM), BF16),
            jax.ShapeDtypeStruct((batch * seq, SSM_INNER), BF16),
        ],
        scratch_shapes=[pltpu.VMEM((SSM_STATE, SSM_INNER), F32)],
        compiler_params=_cparams(2),
        name="ssd_bwd",
    )(proj, proj, proj, dd, ww, conv_w, conv_b, e2b, select)


def _ssd_fwd_kernel(xact_ref, ybi_ref, zt_ref, cs_ref, dd_ref, ww_ref, dsk_ref, e2f_ref, out_ref, state_scr):
    @pl.when(pl.program_id(1) == 0)
    def _():
        state_scr[...] = jnp.zeros(state_scr.shape, F32)

    nh = SSM_HEADS
    lane_w = lax.broadcasted_iota(jnp.int32, (CHUNK, LANES), 1)
    row_i = lax.broadcasted_iota(jnp.int32, (CHUNK, CHUNK), 0)
    col_i = lax.broadcasted_iota(jnp.int32, (CHUNK, CHUNK), 1)
    lower = col_i < row_i
    upper = col_i > row_i

    for k in range(SCAN_CHUNKS):
        rows = slice(k * CHUNK, (k + 1) * CHUNK)
        zt = zt_ref.at[rows]
        cs = cs_ref[rows, :]
        decay = _expand(dd_ref[rows, :], e2f_ref)
        chunk_decay = decay[CHUNK - 1:CHUNK, :]
        xs_bf = xact_ref[rows, 0:SSM_INNER]
        xs = xs_bf.astype(F32)
        xw = (xs * _expand(ww_ref[rows, :], e2f_ref)).astype(BF16)
        for g in range(SSM_GROUPS):
            gs = slice(g * GROUP_W, (g + 1) * GROUP_W)
            bg = xact_ref[rows, SSM_INNER + g * SSM_STATE:SSM_INNER + (g + 1) * SSM_STATE]
            cg = xact_ref[rows, SSM_INNER + (SSM_GROUPS + g) * SSM_STATE:SSM_INNER + (SSM_GROUPS + g + 1) * SSM_STATE]
            cb = lax.dot_general(cg, bg, (((1,), (1,)), ((), ())), preferred_element_type=F32)
            ys = []
            for r in range(GROUP_W // SSM_HEAD_DIM):
                h = g * (GROUP_W // SSM_HEAD_DIM) + r
                seg_f = cs[:, h:h + 1] - zt[2 * nh + h:2 * nh + h + 1, :]
                seg_b = cs[:, nh + h:nh + h + 1] - zt[3 * nh + h:3 * nh + h + 1, :]
                dt_f = zt[h:h + 1, :]
                dt_b = zt[nh + h:nh + h + 1, :]
                dt_sel = jnp.where(lower, dt_f, jnp.where(upper, dt_b, dt_f + dt_b))
                mat = (cb * (jnp.exp(jnp.where(upper, seg_b, seg_f)) * dt_sel)).astype(BF16)
                pair = xs_bf[:, (h // 2) * LANES:(h // 2 + 1) * LANES]
                ys.append(jnp.dot(mat, pair, preferred_element_type=F32))
            y = jnp.concatenate([jnp.where(lane_w < SSM_HEAD_DIM, ys[0], ys[1]),
                                 jnp.where(lane_w < SSM_HEAD_DIM, ys[2], ys[3])], axis=1)
            st = state_scr[:, gs]
            y = y + jnp.dot(cg, st.astype(BF16), preferred_element_type=F32) * decay[:, gs]
            upd = lax.dot_general(bg, xw[:, gs], (((0,), (0,)), ((), ())), preferred_element_type=F32)
            state_scr[:, gs] = st * chunk_decay[:, gs] + upd
            y = y + ybi_ref[rows, gs].astype(F32) + dsk_ref[:, gs] * xs[:, gs]
            out_ref[rows, gs] = y.astype(BF16)


def _ssd_fwd(xact, ybi, zt, cs, dd, ww, d_skip, e2f, batch, seq):
    rows = SCAN_CHUNKS * CHUNK
    n_blocks = seq // rows
    small = pl.BlockSpec((rows, LANES), lambda b, c: (b * n_blocks + c, 0))
    return pl.pallas_call(
        _ssd_fwd_kernel,
        grid=(batch, n_blocks),
        in_specs=[
            pl.BlockSpec((rows, CONV_DIM), lambda b, c: (b * n_blocks + c, 0)),
            pl.BlockSpec((rows, SSM_INNER), lambda b, c: (b * n_blocks + c, 0)),
            small, small, small, small,
            pl.BlockSpec((1, SSM_INNER), lambda b, c: (0, 0)),
            pl.BlockSpec((2 * LANES, SSM_INNER), lambda b, c: (0, 0)),
        ],
        out_specs=pl.BlockSpec((rows, SSM_INNER), lambda b, c: (b * n_blocks + c, 0)),
        out_shape=jax.ShapeDtypeStruct((batch * seq, SSM_INNER), BF16),
        scratch_shapes=[pltpu.VMEM((SSM_STATE, SSM_INNER), F32)],
        compiler_params=_cparams(2),
        name="ssd_fwd",
    )(xact, ybi, zt, cs, dd, ww, d_skip, e2f)


def _mix_out_kernel(x_ref, ao_ref, so_ref, z_ref, nw_ref, ga_ref, gs_ref, wa_ref, ws_ref, wo_ref, n2_ref, x1_ref,
                    h2_ref):
    attn = jnp.dot(ao_ref[...], wa_ref[...], preferred_element_type=F32)
    gated = []
    for g in range(SSM_GROUPS):
        gs = slice(g * GROUP_W, (g + 1) * GROUP_W)
        y = so_ref[:, gs].astype(F32) * _silu(z_ref[:, gs].astype(F32))
        gated.append(_rms(y, nw_ref[:, gs]).astype(BF16))
    ssd = jnp.dot(jnp.concatenate(gated, axis=1), ws_ref[...], preferred_element_type=F32)
    mixed = (jax.nn.sigmoid(ga_ref[...].astype(F32)) * attn
             + jax.nn.sigmoid(gs_ref[...].astype(F32)) * ssd)
    x1 = x_ref[...] + jnp.dot(mixed.astype(BF16), wo_ref[...], preferred_element_type=F32)
    x1_ref[...] = x1
    h2_ref[...] = _rms(x1, n2_ref[...]).astype(BF16)


def _mix_out(x2d, attn_o, ssd_o, proj, ssm_norm_w, w_attn, w_ssm, w_out, norm2_w, tm):
    m, d = x2d.shape
    gb = COL_GATE // d
    zb = COL_Z // SSM_INNER
    return pl.pallas_call(
        _mix_out_kernel,
        grid=(m // tm,),
        in_specs=[
            pl.BlockSpec((tm, d), lambda i: (i, 0)),
            pl.BlockSpec((tm, attn_o.shape[1]), lambda i: (i, 0)),
            pl.BlockSpec((tm, ssd_o.shape[1]), lambda i: (i, 0)),
            pl.BlockSpec((tm, SSM_INNER), lambda i: (i, zb)),
            pl.BlockSpec((1, SSM_INNER), lambda i: (0, 0)),
            pl.BlockSpec((tm, d), lambda i: (i, gb)),
            pl.BlockSpec((tm, d), lambda i: (i, gb + 1)),
            pl.BlockSpec(w_attn.shape, lambda i: (0, 0)),
            pl.BlockSpec(w_ssm.shape, lambda i: (0, 0)),
            pl.BlockSpec(w_out.shape, lambda i: (0, 0)),
            pl.BlockSpec((1, d), lambda i: (0, 0)),
        ],
        out_specs=[pl.BlockSpec((tm, d), lambda i: (i, 0)), pl.BlockSpec((tm, d), lambda i: (i, 0))],
        out_shape=[jax.ShapeDtypeStruct((m, d), F32), jax.ShapeDtypeStruct((m, d), BF16)],
        compiler_params=_cparams(1),
        name="mix_out",
    )(x2d, attn_o, ssd_o, proj, ssm_norm_w, proj, proj, w_attn, w_ssm, w_out, norm2_w)


FFN_CW = 256


def _ffn_kernel(x1_ref, h_ref, hp_ref, hn_ref, wu_ref, cw_ref, cb_ref, wd_ref, o_ref, act_scr, *, tiles_per_seq):
    i = pl.program_id(0)
    tm = h_ref.shape[0]
    f = wd_ref.shape[0]
    pos = i % tiles_per_seq
    keep_prev = jnp.where(pos > 0, 1.0, 0.0)
    keep_next = jnp.where(pos < tiles_per_seq - 1, 1.0, 0.0)
    h_ext = jnp.concatenate([hp_ref[...], h_ref[...], hn_ref[...]], axis=0)

    def conv(col):
        cs = slice(col, col + FFN_CW)
        u_ext = jnp.dot(h_ext, wu_ref[:, cs], preferred_element_type=F32)
        u = u_ext[HALO:HALO + tm]
        up = u_ext[HALO - 8:HALO] * keep_prev
        un = u_ext[HALO + tm:HALO + tm + 8] * keep_next
        return (cb_ref[:, cs] + _shift_rows(u, up, -1) * cw_ref[0:1, cs] + u * cw_ref[1:2, cs]
                + _shift_rows(u, un, 1) * cw_ref[2:3, cs])

    for c in range(f // FFN_CW):
        act = _silu(conv(c * FFN_CW)) * conv(f + c * FFN_CW)
        act_scr[:, c * FFN_CW:(c + 1) * FFN_CW] = act.astype(BF16)
    o_ref[...] = x1_ref[...] + jnp.dot(act_scr[...], wd_ref[...], preferred_element_type=F32)


def _ffn(x1, h2, w_up, conv_w, conv_b, w_down, seq, tm):
    m, d = x1.shape
    f = w_down.shape[0]
    sub = tm // HALO
    last = m // HALO - 1
    kern = functools.partial(_ffn_kernel, tiles_per_seq=seq // tm)
    resident = pl.Buffered(1)
    return pl.pallas_call(
        kern,
        grid=(m // tm,),
        in_specs=[
            pl.BlockSpec((tm, d), lambda i: (i, 0)),
            pl.BlockSpec((tm, d), lambda i: (i, 0)),
            pl.BlockSpec((HALO, d), lambda i: (jnp.maximum(i * sub - 1, 0), 0)),
            pl.BlockSpec((HALO, d), lambda i: (jnp.minimum((i + 1) * sub, last), 0)),
            pl.BlockSpec(w_up.shape, lambda i: (0, 0), pipeline_mode=resident),
            pl.BlockSpec(conv_w.shape, lambda i: (0, 0)),
            pl.BlockSpec(conv_b.shape, lambda i: (0, 0)),
            pl.BlockSpec(w_down.shape, lambda i: (0, 0), pipeline_mode=resident),
        ],
        out_specs=pl.BlockSpec((tm, d), lambda i: (i, 0)),
        out_shape=jax.ShapeDtypeStruct((m, d), F32),
        scratch_shapes=[pltpu.VMEM((tm, f), BF16)],
        compiler_params=_cparams(1),
        name="ffn",
    )(x1, h2, h2, h2, w_up, conv_w, conv_b, w_down)


def _head_expand(first_row):
    rows = jnp.arange(2 * LANES)[:, None] % LANES
    cols = jnp.arange(SSM_INNER)[None, :] // SSM_HEAD_DIM
    return (rows == cols + first_row).astype(BF16)


def _layer(x2d, batch, seq, lambda_init, p):
    d = x2d.shape[1]
    w_in = p["w_in"]
    sizes = (HEADS * 2 * HEAD_DIM,) * 3 + (SSM_INNER, CONV_DIM, 2 * SSM_HEADS, 2 * d)
    offs = [0]
    for s in sizes:
        offs.append(offs[-1] + s)
    wq, wk, wv, wz, wxbc, wdt, wg = (w_in[:, offs[n]:offs[n + 1]] for n in range(7))
    w_main = jnp.concatenate([wxbc, wz, wg, wv], axis=1).astype(BF16)
    w_qk = jnp.concatenate([wq, wk], axis=1).astype(BF16)
    w_dt = jnp.pad(wdt, ((0, 0), (0, LANES - wdt.shape[1]))).astype(BF16)
    qkw = jnp.stack([jnp.tile(p["q_norm_w"], 2 * HEADS) * (HEAD_DIM ** -0.5 * LOG2E),
                     jnp.tile(p["k_norm_w"], 2 * HEADS)])[:, None, :]
    gi = jnp.arange(MXU_DIM) // HEAD_DIM
    gsum = (gi[:, None] == gi[None, :]).astype(BF16)

    proj, dt_raw, h1 = _in_proj(x2d, p["norm1_w"][None, :], w_main, w_dt, tm=1024)
    qk = _qk_proj(h1, w_qk, qkw, gsum, tm=1024)

    lam = (jnp.exp(jnp.sum(p["lambda_q1"] * p["lambda_k1"])) - jnp.exp(jnp.sum(p["lambda_q2"] * p["lambda_k2"]))
           + lambda_init).reshape(1).astype(F32)
    bias_small = _bias_tiles(p["rel_bias"])
    attn_o = _diff_attn(qk, proj, bias_small, p["rel_bias"], lam, p["subln_w"][None, :], batch, seq,
                        tq=512, tk=1024, out_scale=1.0 - lambda_init)

    pad = LANES - 2 * SSM_HEADS
    dt_bias = jnp.pad(jnp.concatenate([p["dt_bias_f"], p["dt_bias_b"]]), (0, pad))[None, :]
    a_log = jnp.pad(jnp.concatenate([p["a_log_f"], p["a_log_b"]]), (0, pad))[None, :]
    zt, cs, dd, ww = _ssd_prep(dt_raw, dt_bias, a_log)
    xact, ybi = _ssd_bwd(proj, dd, ww, p["ssm_conv_w"], p["ssm_conv_b"][None, :],
                         _head_expand(SSM_HEADS), batch, seq)
    ssd_y = _ssd_fwd(xact, ybi, zt, cs, dd, ww, jnp.repeat(p["d_skip"], SSM_HEAD_DIM)[None, :],
                     _head_expand(0), batch, seq)

    x1, h2 = _mix_out(x2d, attn_o, ssd_y, proj, p["ssm_norm_w"][None, :], p["w_attn_out"].astype(BF16),
                      p["w_ssm_out"].astype(BF16), p["w_out"].astype(BF16), p["norm2_w"][None, :], tm=512)

    return _ffn(x1, h2, p["w_ffn_up"].astype(BF16), p["ffn_conv_w"], p["ffn_conv_b"][None, :],
                p["w_ffn_down"].astype(BF16), seq, tm=512)


def kernel(x, norm1_w, w_in, q_norm_w, k_norm_w, rel_bias, lambda_q1, lambda_k1, lambda_q2, lambda_k2, subln_w, w_attn_out, ssm_conv_w, ssm_conv_b, dt_bias_f, a_log_f, dt_bias_b, a_log_b, d_skip, ssm_norm_w, w_ssm_out, w_out, norm2_w, w_ffn_up, ffn_conv_w, ffn_conv_b, w_ffn_down):
    batch, seq, d = x.shape
    layered = dict(norm1_w=norm1_w, w_in=w_in, q_norm_w=q_norm_w, k_norm_w=k_norm_w, lambda_q1=lambda_q1,
                   lambda_k1=lambda_k1, lambda_q2=lambda_q2, lambda_k2=lambda_k2, subln_w=subln_w,
                   w_attn_out=w_attn_out, ssm_conv_w=ssm_conv_w, ssm_conv_b=ssm_conv_b, dt_bias_f=dt_bias_f,
                   a_log_f=a_log_f, dt_bias_b=dt_bias_b, a_log_b=a_log_b, d_skip=d_skip, ssm_norm_w=ssm_norm_w,
                   w_ssm_out=w_ssm_out, w_out=w_out, norm2_w=norm2_w, w_ffn_up=w_ffn_up, ffn_conv_w=ffn_conv_w,
                   ffn_conv_b=ffn_conv_b, w_ffn_down=w_ffn_down)
    x2d = x.reshape(batch * seq, d)
    for layer in range(norm1_w.shape[0]):
        p = {k: v[layer] for k, v in layered.items()}
        p["rel_bias"] = rel_bias
        lambda_init = 0.8 - 0.6 * math.exp(-0.3 * layer)
        x2d = _layer(x2d, batch, seq, lambda_init, p)
    return x2d.reshape(batch, seq, d)
```

```python
import functools
import math

import jax
import jax.numpy as jnp
from jax import lax
from jax.experimental import pallas as pl
from jax.experimental.pallas import tpu as pltpu

F32 = jnp.float32
BF16 = jnp.bfloat16

HEADS = 8
HEAD_DIM = 64
SSM_HEADS = 32
SSM_HEAD_DIM = 64
SSM_GROUPS = 8
SSM_STATE = 128
SSM_INNER = SSM_HEADS * SSM_HEAD_DIM
GROUP_W = SSM_INNER // SSM_GROUPS
CONV_DIM = SSM_INNER + 2 * SSM_GROUPS * SSM_STATE
CHUNK = 128
SSM_SHIFTS = (-2, -1, 1)
HALO = 16
RMS_EPS = 1e-6
LOG2E = 1.4426950408889634
NEG_BIG = -1e30

REL_THRESHOLDS = (12, 16, 23, 32, 46, 64, 91)
REL_FAR = 129

LANES = 128
VMEM_LIMIT = 52 * 1024 * 1024
VMEM_LIMIT_ATTN = 60 * 1024 * 1024

MXU_DIM = 256

COL_XBC = 0
COL_Z = 4096
COL_GATE = 6144
COL_V = 8192
PROJ_W = 9216
PROJ_TILE = 1024


def _cparams(n_axes, vmem_limit=VMEM_LIMIT):
    return pltpu.CompilerParams(dimension_semantics=("arbitrary",) * n_axes, vmem_limit_bytes=vmem_limit)


def _rms(x, w):
    return x * lax.rsqrt(jnp.mean(x * x, axis=-1, keepdims=True) + RMS_EPS) * w


def _split_bf16(x):
    hi = x.astype(BF16)
    lo = (x - hi.astype(F32)).astype(BF16)
    return hi, lo


def _silu(x):
    return x / (1.0 + jnp.exp(-x))


def _in_proj_kernel(x_ref, nw_ref, w_ref, wdt_ref, proj_ref, dt_ref, h_ref):
    @pl.when(pl.program_id(1) == 0)
    def _():
        h = _rms(x_ref[...], nw_ref[...]).astype(BF16)
        h_ref[...] = h
        dt_ref[...] = jnp.dot(h, wdt_ref[...], preferred_element_type=F32)

    proj_ref[...] = jnp.dot(h_ref[...], w_ref[...], preferred_element_type=F32).astype(BF16)


def _in_proj(x2d, norm_w, w_main, w_dt, tm):
    m, d = x2d.shape
    return pl.pallas_call(
        _in_proj_kernel,
        grid=(m // tm, PROJ_W // PROJ_TILE),
        in_specs=[
            pl.BlockSpec((tm, d), lambda i, j: (i, 0)),
            pl.BlockSpec((1, d), lambda i, j: (0, 0)),
            pl.BlockSpec((d, PROJ_TILE), lambda i, j: (0, j)),
            pl.BlockSpec((d, LANES), lambda i, j: (0, 0)),
        ],
        out_specs=[
            pl.BlockSpec((tm, PROJ_TILE), lambda i, j: (i, j)),
            pl.BlockSpec((tm, LANES), lambda i, j: (i, 0)),
            pl.BlockSpec((tm, d), lambda i, j: (i, 0)),
        ],
        out_shape=[
            jax.ShapeDtypeStruct((m, PROJ_W), BF16),
            jax.ShapeDtypeStruct((m, LANES), F32),
            jax.ShapeDtypeStruct((m, d), BF16),
        ],
        compiler_params=_cparams(2),
        name="in_proj",
    )(x2d, norm_w, w_main, w_dt)


def _qk_proj_kernel(h_ref, w_ref, qkw_ref, gsum_ref, o_ref):
    acc = jnp.dot(h_ref[...], w_ref[...], preferred_element_type=F32)
    for c in range(PROJ_TILE // MXU_DIM):
        cs = slice(c * MXU_DIM, (c + 1) * MXU_DIM)
        a = acc[:, cs]
        ss = jnp.dot((a * a).astype(BF16), gsum_ref[...], preferred_element_type=F32)
        o_ref[:, cs] = (a * lax.rsqrt(ss * (1.0 / HEAD_DIM) + RMS_EPS) * qkw_ref[0, :, cs]).astype(BF16)


def _qk_proj(h, w_qk, qkw, gsum, tm):
    m, d = h.shape
    return pl.pallas_call(
        _qk_proj_kernel,
        grid=(m // tm, 2),
        in_specs=[
            pl.BlockSpec((tm, d), lambda i, j: (i, 0)),
            pl.BlockSpec((d, PROJ_TILE), lambda i, j: (0, j)),
            pl.BlockSpec((1, 1, PROJ_TILE), lambda i, j: (j, 0, 0)),
            pl.BlockSpec((MXU_DIM, MXU_DIM), lambda i, j: (0, 0)),
        ],
        out_specs=pl.BlockSpec((tm, PROJ_TILE), lambda i, j: (i, j)),
        out_shape=jax.ShapeDtypeStruct((m, 2 * PROJ_TILE), BF16),
        compiler_params=_cparams(2),
        name="qk_proj",
    )(h, w_qk, qkw, gsum)


def _bias_tiles_kernel(table_ref, out_ref):
    h = pl.program_id(0)
    r = lax.broadcasted_iota(jnp.int32, (LANES, LANES), 0)
    c = lax.broadcasted_iota(jnp.int32, (LANES, LANES), 1)
    for d in range(3):
        rel = (d - 1) * LANES + c - r
        n = jnp.abs(rel)
        large = jnp.full((LANES, LANES), 8, jnp.int32)
        for t in REL_THRESHOLDS:
            large = large + jnp.where(n >= t, 1, 0)
        bucket = jnp.where(rel > 0, 16, 0) + jnp.where(n < 8, n, large)
        val = jnp.zeros((LANES, LANES), F32)
        for bkt in range(32):
            val = jnp.where(bucket == bkt, table_ref[bkt, h], val)
        out_ref[0, d] = val * LOG2E


def _bias_tiles(rel_bias):
    return pl.pallas_call(
        _bias_tiles_kernel,
        grid=(HEADS,),
        in_specs=[pl.BlockSpec(memory_space=pltpu.SMEM)],
        out_specs=pl.BlockSpec((1, 3, LANES, LANES), lambda h: (h, 0, 0, 0)),
        out_shape=jax.ShapeDtypeStruct((HEADS, 3, LANES, LANES), F32),
        compiler_params=_cparams(1),
        name="bias_tiles",
    )(rel_bias)


def _near_bias_tiles(tq, tk):
    nbq, nbk = tq // LANES, tk // LANES
    table = {}
    for a in range(tk // tq):
        for d in (-1, 0, 1):
            ob = d * nbk - a * nbq
            if ob + nbk - 1 >= -1 and ob - (nbq - 1) <= 1:
                table[(a, d)] = (2 + len(table), ob)
    return table


def _attn_kernel(lam_ref, table_ref, q_ref, k_ref, v_ref, bsm_ref, subln_ref, o_ref,
                 s_scr, p_scr, mcur_scr, alpha_scr, m_scr, acc_scr, bias_scr, *, tq, tk, seq, out_scale):
    nk = seq // tk
    ratio = tk // tq
    near = _near_bias_tiles(tq, tk)
    h = pl.program_id(0)
    b = pl.program_id(1)
    c_lo = table_ref[15, h] * LOG2E
    c_hi = table_ref[31, h] * LOG2E

    @pl.when(b == 0)
    def _():
        ones = jnp.ones((tq, tk), F32)
        bias_scr[0] = ones * c_lo
        bias_scr[1] = ones * c_hi
        for idx, ob in near.values():
            for rb in range(tq // LANES):
                for cb in range(tk // LANES):
                    off = ob + cb - rb
                    if abs(off) <= 1:
                        blk = bsm_ref[0, off + 1]
                    else:
                        blk = jnp.ones((LANES, LANES), F32) * (c_lo if off < 0 else c_hi)
                    bias_scr[idx, rb * LANES:(rb + 1) * LANES, cb * LANES:(cb + 1) * LANES] = blk

    lane = lax.broadcasted_iota(jnp.int32, (tq, LANES), 1)
    zero = jnp.zeros((tq, LANES), BF16)
    ones_blk = jnp.ones((tk, LANES), BF16)

    first_near = {a: min(d for (a0, d) in near if a0 == a) for a in range(ratio)}
    n_e = max(sum(1 for (a0, _) in near if a0 == a) for a in range(ratio))

    def explicit_start(i):
        c = i // ratio
        a = i - c * ratio
        lo = c + first_near[0]
        for a0 in range(1, ratio):
            lo = jnp.where(a == a0, c + first_near[a0], lo)
        return jnp.clip(lo, 0, nk - n_e)

    def key_tile(i, pos):
        e0 = explicit_start(i)
        if pos < n_e:
            return e0 + pos, None
        k = pos - n_e
        j = k + jnp.where(k >= e0, n_e, 0)
        return j, jnp.where(j < e0, c_lo, c_hi)

    def bias_kind(i, j):
        c = i // ratio
        a = i - c * ratio
        d = j - c
        kind = jnp.where(d < 0, 0, 1)
        for (a0, d0), (idx, _) in near.items():
            kind = jnp.where((a == a0) & (d == d0), idx, kind)
        return kind

    def scores(i, pos, slot):
        j, const = key_tile(i, pos)
        q = q_ref[pl.ds(pl.multiple_of(i * tq, tq), tq), :]
        qs = jnp.concatenate([jnp.where(lane < HEAD_DIM, q, zero), jnp.where(lane >= HEAD_DIM, q, zero)], axis=0)
        k = k_ref[pl.ds(pl.multiple_of(j * tk, tk), tk), :]
        s = lax.dot_general(qs, k, (((1,), (1,)), ((), ())), preferred_element_type=F32)
        if const is None:
            kind = bias_kind(i, j)
            s = jnp.concatenate([s[0:tq] + bias_scr[kind], s[tq:2 * tq] + bias_scr[kind]], axis=0)
            row_max = jnp.max(s, axis=1, keepdims=True)
        else:
            row_max = jnp.max(s, axis=1, keepdims=True) + const
        s_scr[slot] = s
        mcur_scr[slot] = jnp.broadcast_to(row_max, (2 * tq, LANES))

    def softmax(i, pos, slot):
        _, const = key_tile(i, pos)
        if pos == 0:
            m_next = mcur_scr[slot]
        else:
            m_prev = m_scr[...]
            m_next = jnp.maximum(m_prev, mcur_scr[slot])
            alpha_scr[slot] = jnp.exp2(m_prev - m_next)
        shift = m_next if const is None else m_next - const
        p_scr[slot] = jnp.exp2(s_scr[slot] - pltpu.repeat(shift, tk // LANES, axis=1)).astype(BF16)
        m_scr[...] = m_next

    def values(i, pos, slot):
        j, _ = key_tile(i, pos)
        vaug = jnp.concatenate([v_ref[pl.ds(pl.multiple_of(j * tk, tk), tk), :], ones_blk], axis=1)
        pv = jnp.dot(p_scr[slot], vaug, preferred_element_type=F32)
        if pos == 0:
            acc_scr[...] = pv
        else:
            acc_scr[...] = acc_scr[...] * pltpu.repeat(alpha_scr[slot], 2, axis=1) + pv

    def finalize(i):
        acc = acc_scr[...]
        o12 = acc[:, 0:LANES] / acc[:, LANES:2 * LANES]
        o = o12[0:tq] - lam_ref[0] * o12[tq:2 * tq]
        o_ref[pl.ds(pl.multiple_of(i * tq, tq), tq), :] = (_rms(o, subln_ref[...]) * out_scale).astype(BF16)

    nq = seq // tq
    scores(0, 0, 0)
    softmax(0, 0, 0)
    scores(0, 1, 1)

    def body(g, carry):
        nxt = jnp.minimum(g + 1, nq - 1)
        for r in range(nk):
            values(g, r, r % 2)
            softmax(g if r + 1 < nk else nxt, (r + 1) % nk, (r + 1) % 2)
            scores(g if r + 2 < nk else nxt, (r + 2) % nk, r % 2)
        finalize(g)
        return carry

    lax.fori_loop(0, nq, body, 0)


def _diff_attn(qk, proj, bias_small, rel_bias, lam, subln_w, batch, seq, tq, tk, out_scale):
    assert seq % tk == 0 and tk % tq == 0 and (seq // tk) % 2 == 0
    qb, kb, vb = 0, HEADS, COL_V // LANES
    kern = functools.partial(_attn_kernel, tq=tq, tk=tk, seq=seq, out_scale=out_scale)
    n_bias = 2 + len(_near_bias_tiles(tq, tk))
    return pl.pallas_call(
        kern,
        grid=(HEADS, batch),
        in_specs=[
            pl.BlockSpec(memory_space=pltpu.SMEM),
            pl.BlockSpec(memory_space=pltpu.SMEM),
            pl.BlockSpec((seq, LANES), lambda h, b: (b, qb + h)),
            pl.BlockSpec((seq, LANES), lambda h, b: (b, kb + h)),
            pl.BlockSpec((seq, LANES), lambda h, b: (b, vb + h)),
            pl.BlockSpec((1, 3, LANES, LANES), lambda h, b: (h, 0, 0, 0)),
            pl.BlockSpec((1, LANES), lambda h, b: (0, 0)),
        ],
        out_specs=pl.BlockSpec((seq, LANES), lambda h, b: (b, h)),
        out_shape=jax.ShapeDtypeStruct((batch * seq, HEADS * LANES), BF16),
        scratch_shapes=[
            pltpu.VMEM((2, 2 * tq, tk), F32),
            pltpu.VMEM((2, 2 * tq, tk), BF16),
            pltpu.VMEM((2, 2 * tq, LANES), F32),
            pltpu.VMEM((2, 2 * tq, LANES), F32),
            pltpu.VMEM((2 * tq, LANES), F32),
            pltpu.VMEM((2 * tq, 2 * LANES), F32),
            pltpu.VMEM((n_bias, tq, tk), F32),
        ],
        compiler_params=_cparams(2, VMEM_LIMIT_ATTN),
        name="diff_attn",
    )(lam, rel_bias, qk, qk, proj, bias_small, subln_w)


def _tri(upper):
    r = lax.broadcasted_iota(jnp.int32, (CHUNK, CHUNK), 0)
    c = lax.broadcasted_iota(jnp.int32, (CHUNK, CHUNK), 1)
    return (c >= r) if upper else (c <= r)


def _tri_dot(mask, x):
    m = jnp.where(mask, 1.0, 0.0).astype(BF16)
    hi, lo = _split_bf16(x)
    return jnp.dot(m, hi, preferred_element_type=F32) + jnp.dot(m, lo, preferred_element_type=F32)


PREP_CHUNKS = 16


def _ssd_prep_kernel(dt_ref, dtb_ref, alog_ref, zt_ref, cs_ref, dd_ref, ww_ref):
    nh = SSM_HEADS
    lane = lax.broadcasted_iota(jnp.int32, (CHUNK, LANES), 1)
    tril = _tri(False)
    triu = _tri(True)
    for k in range(PREP_CHUNKS):
        rows = slice(k * CHUNK, (k + 1) * CHUNK)
        raw = dt_ref[rows, :] + dtb_ref[...]
        dt = jnp.maximum(raw, 0.0) + jnp.log1p(jnp.exp(-jnp.abs(raw)))
        dt = jnp.where(lane < 2 * nh, dt, 0.0)
        da = dt * -jnp.exp(alog_ref[...])
        acs = _tri_dot(tril, da)
        rcs = _tri_dot(triu, da)
        cs = jnp.where(lane < nh, acs, rcs)
        end = jnp.where(lane < nh, acs[CHUNK - 1:CHUNK, :], rcs[0:1, :])
        cs_ref[rows, :] = cs
        dd_ref[rows, :] = jnp.exp(cs)
        ww_ref[rows, :] = jnp.exp(end - cs) * dt
        zt_ref[rows, :] = jnp.where(lane < 2 * nh, dt, pltpu.roll(cs, 2 * nh, 1)).T


def _ssd_prep(dt_raw, dt_bias, a_log):
    m = dt_raw.shape[0]
    rows = PREP_CHUNKS * CHUNK
    blk = pl.BlockSpec((rows, LANES), lambda i: (i, 0))
    vec = pl.BlockSpec((1, LANES), lambda i: (0, 0))
    return pl.pallas_call(
        _ssd_prep_kernel,
        grid=(m // rows,),
        in_specs=[blk, vec, vec],
        out_specs=[blk] * 4,
        out_shape=[jax.ShapeDtypeStruct((m, LANES), F32)] * 4,
        compiler_params=_cparams(1),
        name="ssd_prep",
    )(dt_raw, dt_bias, a_log)


def _expand_operand(x):
    hi, lo = _split_bf16(x)
    return jnp.concatenate([hi, lo], axis=1)


def _expand(x_hl, e2_ref, gs):
    return jnp.dot(x_hl, e2_ref[:, gs], preferred_element_type=F32)


def _shift_rows(u, halo, offset):
    n = u.shape[0]
    if offset == 0:
        return u
    rolled = pltpu.roll(u, (-offset) % n, 0)
    row8 = lax.broadcasted_iota(jnp.int32, (8, u.shape[1]), 0)
    if offset < 0:
        hfix = pltpu.roll(halo, (-offset) % 8, 0)
        first = jnp.where(row8 < -offset, hfix, rolled[0:8])
        return jnp.concatenate([first, rolled[8:n]], axis=0)
    hfix = pltpu.roll(halo, (8 - offset) % 8, 0)
    last = jnp.where(row8 >= 8 - offset, hfix, rolled[n - 8:n])
    return jnp.concatenate([rolled[0:n - 8], last], axis=0)


def _shift_select(n, shifts):
    t = jnp.arange(len(shifts) * n)
    src = t % n + HALO + jnp.repeat(jnp.asarray(shifts), n)
    return (src[:, None] == jnp.arange(n + 2 * HALO)[None, :]).astype(BF16)


SCAN_CHUNKS = 2


def _ssd_bwd_kernel(xc_ref, xp_ref, xn_ref, dd_ref, ww_ref, cw_ref, cb_ref, e2b_ref, sel_ref,
                    xact_ref, ybi_ref, state_scr, *, n_blocks):
    c = pl.program_id(1)
    cr = n_blocks - 1 - c

    @pl.when(c == 0)
    def _():
        state_scr[...] = jnp.zeros(state_scr.shape, F32)

    keep_prev = jnp.where(cr > 0, 1.0, 0.0).astype(BF16)
    keep_next = jnp.where(cr < n_blocks - 1, 1.0, 0.0).astype(BF16)
    slab = 512
    for sl in range(CONV_DIM // slab):
        cs = slice(sl * slab, (sl + 1) * slab)
        ext = jnp.concatenate([xp_ref[:, cs] * keep_prev, xc_ref[:, cs], xn_ref[:, cs] * keep_next], axis=0)
        for k in range(SCAN_CHUNKS):
            rows = slice(k * CHUNK, (k + 1) * CHUNK)
            sh = jnp.dot(sel_ref[...], ext[k * CHUNK:(k + 1) * CHUNK + 2 * HALO], preferred_element_type=F32)
            acc = cb_ref[:, cs] + xc_ref[rows, cs].astype(F32) * cw_ref[2:3, cs]
            for j, off in enumerate(SSM_SHIFTS):
                acc = acc + sh[j * CHUNK:(j + 1) * CHUNK] * cw_ref[off + 2:off + 3, cs]
            xact_ref[rows, cs] = _silu(acc).astype(BF16)

    for k in reversed(range(SCAN_CHUNKS)):
        rows = slice(k * CHUNK, (k + 1) * CHUNK)
        everything = slice(0, SSM_INNER)
        decay = _expand(_expand_operand(dd_ref[rows, :]), e2b_ref, everything)
        chunk_decay = decay[0:1, :]
        xs = xact_ref[rows, 0:SSM_INNER].astype(F32)
        xw = (xs * _expand(_expand_operand(ww_ref[rows, :]), e2b_ref, everything)).astype(BF16)
        for g in range(SSM_GROUPS):
            gs = slice(g * GROUP_W, (g + 1) * GROUP_W)
            bg = xact_ref[rows, SSM_INNER + g * SSM_STATE:SSM_INNER + (g + 1) * SSM_STATE]
            cg = xact_ref[rows, SSM_INNER + (SSM_GROUPS + g) * SSM_STATE:SSM_INNER + (SSM_GROUPS + g + 1) * SSM_STATE]
            st = state_scr[:, gs]
            yb = jnp.dot(cg, st.astype(BF16), preferred_element_type=F32) * decay[:, gs]
            ybi_ref[rows, gs] = yb.astype(BF16)
            upd = lax.dot_general(bg, xw[:, gs], (((0,), (0,)), ((), ())), preferred_element_type=F32)
            state_scr[:, gs] = st * chunk_decay[:, gs] + upd


def _ssd_bwd(proj, dd, ww, conv_w, conv_b, e2b, batch, seq):
    rows = SCAN_CHUNKS * CHUNK
    n_blocks = seq // rows
    sub = rows // HALO
    last16 = batch * seq // HALO - 1
    kern = functools.partial(_ssd_bwd_kernel, n_blocks=n_blocks)
    select = _shift_select(CHUNK, SSM_SHIFTS)

    def cur(b, c):
        return b * n_blocks + (n_blocks - 1 - c)

    return pl.pallas_call(
        kern,
        grid=(batch, n_blocks),
        in_specs=[
            pl.BlockSpec((rows, CONV_DIM), lambda b, c: (cur(b, c), 0)),
            pl.BlockSpec((HALO, CONV_DIM), lambda b, c: (jnp.maximum(cur(b, c) * sub - 1, 0), 0)),
            pl.BlockSpec((HALO, CONV_DIM), lambda b, c: (jnp.minimum((cur(b, c) + 1) * sub, last16), 0)),
            pl.BlockSpec((rows, LANES), lambda b, c: (cur(b, c), 0)),
            pl.BlockSpec((rows, LANES), lambda b, c: (cur(b, c), 0)),
            pl.BlockSpec((4, CONV_DIM), lambda b, c: (0, 0)),
            pl.BlockSpec((1, CONV_DIM), lambda b, c: (0, 0)),
            pl.BlockSpec((2 * LANES, SSM_INNER), lambda b, c: (0, 0)),
            pl.BlockSpec(select.shape, lambda b, c: (0, 0)),
        ],
        out_specs=[
            pl.BlockSpec((rows, CONV_DIM), lambda b, c: (cur(b, c), 0)),
            pl.BlockSpec((rows, SSM_INNER), lambda b, c: (cur(b, c), 0)),
        ],
        out_shape=[
            jax.ShapeDtypeStruct((batch * seq, CONV_DIM), BF16),
            jax.ShapeDtypeStruct((batch * seq, SSM_INNER), BF16),
        ],
        scratch_shapes=[pltpu.VMEM((SSM_STATE, SSM_INNER), F32)],
        compiler_params=_cparams(2),
        name="ssd_bwd",
    )(proj, proj, proj, dd, ww, conv_w, conv_b, e2b, select)


def _ssd_fwd_kernel(xact_ref, ybi_ref, zt_ref, cs_ref, dd_ref, ww_ref, dsk_ref, e2f_ref, out_ref, state_scr):
    @pl.when(pl.program_id(1) == 0)
    def _():
        state_scr[...] = jnp.zeros(state_scr.shape, F32)

    nh = SSM_HEADS
    lane_w = lax.broadcasted_iota(jnp.int32, (CHUNK, LANES), 1)
    row_i = lax.broadcasted_iota(jnp.int32, (CHUNK, CHUNK), 0)
    col_i = lax.broadcasted_iota(jnp.int32, (CHUNK, CHUNK), 1)
    lower = col_i < row_i
    upper = col_i > row_i

    for k in range(SCAN_CHUNKS):
        rows = slice(k * CHUNK, (k + 1) * CHUNK)
        zt = zt_ref.at[rows]
        cs = cs_ref[rows, :]
        dd_hl = _expand_operand(dd_ref[rows, :])
        ww_hl = _expand_operand(ww_ref[rows, :])
        for g in range(SSM_GROUPS):
            gs = slice(g * GROUP_W, (g + 1) * GROUP_W)
            bg = xact_ref[rows, SSM_INNER + g * SSM_STATE:SSM_INNER + (g + 1) * SSM_STATE]
            cg = xact_ref[rows, SSM_INNER + (SSM_GROUPS + g) * SSM_STATE:SSM_INNER + (SSM_GROUPS + g + 1) * SSM_STATE]
            xs_bf = xact_ref[rows, gs]
            xs = xs_bf.astype(F32)
            decay = _expand(dd_hl, e2f_ref, gs)
            xw = (xs * _expand(ww_hl, e2f_ref, gs)).astype(BF16)
            cb = lax.dot_general(cg, bg, (((1,), (1,)), ((), ())), preferred_element_type=F32)
            ys = []
            for r in range(GROUP_W // SSM_HEAD_DIM):
                h = g * (GROUP_W // SSM_HEAD_DIM) + r
                seg_f = cs[:, h:h + 1] - zt[2 * nh + h:2 * nh + h + 1, :]
                seg_b = cs[:, nh + h:nh + h + 1] - zt[3 * nh + h:3 * nh + h + 1, :]
                dt_f = zt[h:h + 1, :]
                dt_b = zt[nh + h:nh + h + 1, :]
                dt_sel = jnp.where(lower, dt_f, jnp.where(upper, dt_b, dt_f + dt_b))
                mat = (cb * (jnp.exp(jnp.where(upper, seg_b, seg_f)) * dt_sel)).astype(BF16)
                pair = xs_bf[:, (r // 2) * LANES:(r // 2 + 1) * LANES]
                ys.append(jnp.dot(mat, pair, preferred_element_type=F32))
            y = jnp.concatenate([jnp.where(lane_w < SSM_HEAD_DIM, ys[0], ys[1]),
                                 jnp.where(lane_w < SSM_HEAD_DIM, ys[2], ys[3])], axis=1)
            st = state_scr[:, gs]
            y = y + jnp.dot(cg, st.astype(BF16), preferred_element_type=F32) * decay
            upd = lax.dot_general(bg, xw, (((0,), (0,)), ((), ())), preferred_element_type=F32)
            state_scr[:, gs] = st * decay[CHUNK - 1:CHUNK, :] + upd
            y = y + ybi_ref[rows, gs].astype(F32) + dsk_ref[:, gs] * xs
            out_ref[rows, gs] = y.astype(BF16)


def _ssd_fwd(xact, ybi, zt, cs, dd, ww, d_skip, e2f, batch, seq):
    rows = SCAN_CHUNKS * CHUNK
    n_blocks = seq // rows
    small = pl.BlockSpec((rows, LANES), lambda b, c: (b * n_blocks + c, 0))
    return pl.pallas_call(
        _ssd_fwd_kernel,
        grid=(batch, n_blocks),
        in_specs=[
            pl.BlockSpec((rows, CONV_DIM), lambda b, c: (b * n_blocks + c, 0)),
            pl.BlockSpec((rows, SSM_INNER), lambda b, c: (b * n_blocks + c, 0)),
            small, small, small, small,
            pl.BlockSpec((1, SSM_INNER), lambda b, c: (0, 0)),
            pl.BlockSpec((2 * LANES, SSM_INNER), lambda b, c: (0, 0)),
        ],
        out_specs=pl.BlockSpec((rows, SSM_INNER), lambda b, c: (b * n_blocks + c, 0)),
        out_shape=jax.ShapeDtypeStruct((batch * seq, SSM_INNER), BF16),
        scratch_shapes=[pltpu.VMEM((SSM_STATE, SSM_INNER), F32)],
        compiler_params=_cparams(2),
        name="ssd_fwd",
    )(xact, ybi, zt, cs, dd, ww, d_skip, e2f)


def _mix_out_kernel(x_ref, ao_ref, so_ref, z_ref, nw_ref, ga_ref, gs_ref, wa_ref, ws_ref, wo_ref, n2_ref, x1_ref,
                    h2_ref):
    attn = jnp.dot(ao_ref[...], wa_ref[...], preferred_element_type=F32)
    ssd = None
    for g in range(SSM_GROUPS):
        gs = slice(g * GROUP_W, (g + 1) * GROUP_W)
        y = so_ref[:, gs].astype(F32) * _silu(z_ref[:, gs].astype(F32))
        part = jnp.dot(_rms(y, nw_ref[:, gs]).astype(BF16), ws_ref[gs, :], preferred_element_type=F32)
        ssd = part if ssd is None else ssd + part
    mixed = (jax.nn.sigmoid(ga_ref[...].astype(F32)) * attn
             + jax.nn.sigmoid(gs_ref[...].astype(F32)) * ssd)
    x1 = x_ref[...] + jnp.dot(mixed.astype(BF16), wo_ref[...], preferred_element_type=F32)
    x1_ref[...] = x1
    h2_ref[...] = _rms(x1, n2_ref[...]).astype(BF16)


def _mix_out(x2d, attn_o, ssd_o, proj, ssm_norm_w, w_attn, w_ssm, w_out, norm2_w, tm):
    m, d = x2d.shape
    gb = COL_GATE // d
    zb = COL_Z // SSM_INNER
    return pl.pallas_call(
        _mix_out_kernel,
        grid=(m // tm,),
        in_specs=[
            pl.BlockSpec((tm, d), lambda i: (i, 0)),
            pl.BlockSpec((tm, attn_o.shape[1]), lambda i: (i, 0)),
            pl.BlockSpec((tm, ssd_o.shape[1]), lambda i: (i, 0)),
            pl.BlockSpec((tm, SSM_INNER), lambda i: (i, zb)),
            pl.BlockSpec((1, SSM_INNER), lambda i: (0, 0)),
            pl.BlockSpec((tm, d), lambda i: (i, gb)),
            pl.BlockSpec((tm, d), lambda i: (i, gb + 1)),
            pl.BlockSpec(w_attn.shape, lambda i: (0, 0)),
            pl.BlockSpec(w_ssm.shape, lambda i: (0, 0)),
            pl.BlockSpec(w_out.shape, lambda i: (0, 0)),
            pl.BlockSpec((1, d), lambda i: (0, 0)),
        ],
        out_specs=[pl.BlockSpec((tm, d), lambda i: (i, 0)), pl.BlockSpec((tm, d), lambda i: (i, 0))],
        out_shape=[jax.ShapeDtypeStruct((m, d), F32), jax.ShapeDtypeStruct((m, d), BF16)],
        compiler_params=_cparams(1),
        name="mix_out",
    )(x2d, attn_o, ssd_o, proj, ssm_norm_w, proj, proj, w_attn, w_ssm, w_out, norm2_w)


FFN_CW = 256


def _ffn_kernel(x1_ref, h_ref, hp_ref, hn_ref, wu_ref, cw_ref, cb_ref, wd_ref, o_ref, act_scr, *, tiles_per_seq):
    i = pl.program_id(0)
    tm = h_ref.shape[0]
    f = wd_ref.shape[0]
    pos = i % tiles_per_seq
    keep_prev = jnp.where(pos > 0, 1.0, 0.0)
    keep_next = jnp.where(pos < tiles_per_seq - 1, 1.0, 0.0)
    h_ext = jnp.concatenate([hp_ref[...], h_ref[...], hn_ref[...]], axis=0)

    def conv(col):
        cs = slice(col, col + FFN_CW)
        u_ext = jnp.dot(h_ext, wu_ref[:, cs], preferred_element_type=F32)
        u = u_ext[HALO:HALO + tm]
        up = u_ext[HALO - 8:HALO] * keep_prev
        un = u_ext[HALO + tm:HALO + tm + 8] * keep_next
        return (cb_ref[:, cs] + _shift_rows(u, up, -1) * cw_ref[0:1, cs] + u * cw_ref[1:2, cs]
                + _shift_rows(u, un, 1) * cw_ref[2:3, cs])

    for c in range(f // FFN_CW):
        act = _silu(conv(c * FFN_CW)) * conv(f + c * FFN_CW)
        act_scr[:, c * FFN_CW:(c + 1) * FFN_CW] = act.astype(BF16)
    o_ref[...] = x1_ref[...] + jnp.dot(act_scr[...], wd_ref[...], preferred_element_type=F32)


def _ffn(x1, h2, w_up, conv_w, conv_b, w_down, seq, tm):
    m, d = x1.shape
    f = w_down.shape[0]
    sub = tm // HALO
    last = m // HALO - 1
    kern = functools.partial(_ffn_kernel, tiles_per_seq=seq // tm)
    resident = pl.Buffered(1)
    return pl.pallas_call(
        kern,
        grid=(m // tm,),
        in_specs=[
            pl.BlockSpec((tm, d), lambda i: (i, 0)),
            pl.BlockSpec((tm, d), lambda i: (i, 0)),
            pl.BlockSpec((HALO, d), lambda i: (jnp.maximum(i * sub - 1, 0), 0)),
            pl.BlockSpec((HALO, d), lambda i: (jnp.minimum((i + 1) * sub, last), 0)),
            pl.BlockSpec(w_up.shape, lambda i: (0, 0), pipeline_mode=resident),
            pl.BlockSpec(conv_w.shape, lambda i: (0, 0)),
            pl.BlockSpec(conv_b.shape, lambda i: (0, 0)),
            pl.BlockSpec(w_down.shape, lambda i: (0, 0), pipeline_mode=resident),
        ],
        out_specs=pl.BlockSpec((tm, d), lambda i: (i, 0)),
        out_shape=jax.ShapeDtypeStruct((m, d), F32),
        scratch_shapes=[pltpu.VMEM((tm, f), BF16)],
        compiler_params=_cparams(1),
        name="ffn",
    )(x1, h2, h2, h2, w_up, conv_w, conv_b, w_down)


def _head_expand(first_row):
    rows = jnp.arange(2 * LANES)[:, None] % LANES
    cols = jnp.arange(SSM_INNER)[None, :] // SSM_HEAD_DIM
    return (rows == cols + first_row).astype(BF16)


def _layer(x2d, batch, seq, lambda_init, p):
    d = x2d.shape[1]
    w_in = p["w_in"]
    sizes = (HEADS * 2 * HEAD_DIM,) * 3 + (SSM_INNER, CONV_DIM, 2 * SSM_HEADS, 2 * d)
    offs = [0]
    for s in sizes:
        offs.append(offs[-1] + s)
    wq, wk, wv, wz, wxbc, wdt, wg = (w_in[:, offs[n]:offs[n + 1]] for n in range(7))
    w_main = jnp.concatenate([wxbc, wz, wg, wv], axis=1).astype(BF16)
    w_qk = jnp.concatenate([wq, wk], axis=1).astype(BF16)
    w_dt = jnp.pad(wdt, ((0, 0), (0, LANES - wdt.shape[1]))).astype(BF16)
    qkw = jnp.stack([jnp.tile(p["q_norm_w"], 2 * HEADS) * (HEAD_DIM ** -0.5 * LOG2E),
                     jnp.tile(p["k_norm_w"], 2 * HEADS)])[:, None, :]
    gi = jnp.arange(MXU_DIM) // HEAD_DIM
    gsum = (gi[:, None] == gi[None, :]).astype(BF16)

    proj, dt_raw, h1 = _in_proj(x2d, p["norm1_w"][None, :], w_main, w_dt, tm=2048)
    qk = _qk_proj(h1, w_qk, qkw, gsum, tm=1024)

    lam = (jnp.exp(jnp.sum(p["lambda_q1"] * p["lambda_k1"])) - jnp.exp(jnp.sum(p["lambda_q2"] * p["lambda_k2"]))
           + lambda_init).reshape(1).astype(F32)
    bias_small = _bias_tiles(p["rel_bias"])
    attn_o = _diff_attn(qk, proj, bias_small, p["rel_bias"], lam, p["subln_w"][None, :], batch, seq,
                        tq=512, tk=1024, out_scale=1.0 - lambda_init)

    pad = LANES - 2 * SSM_HEADS
    dt_bias = jnp.pad(jnp.concatenate([p["dt_bias_f"], p["dt_bias_b"]]), (0, pad))[None, :]
    a_log = jnp.pad(jnp.concatenate([p["a_log_f"], p["a_log_b"]]), (0, pad))[None, :]
    zt, cs, dd, ww = _ssd_prep(dt_raw, dt_bias, a_log)
    xact, ybi = _ssd_bwd(proj, dd, ww, p["ssm_conv_w"], p["ssm_conv_b"][None, :],
                         _head_expand(SSM_HEADS), batch, seq)
    ssd_y = _ssd_fwd(xact, ybi, zt, cs, dd, ww, jnp.repeat(p["d_skip"], SSM_HEAD_DIM)[None, :],
                     _head_expand(0), batch, seq)

    x1, h2 = _mix_out(x2d, attn_o, ssd_y, proj, p["ssm_norm_w"][None, :], p["w_attn_out"].astype(BF16),
                      p["w_ssm_out"].astype(BF16), p["w_out"].astype(BF16), p["norm2_w"][None, :], tm=512)

    return _ffn(x1, h2, p["w_ffn_up"].astype(BF16), p["ffn_conv_w"], p["ffn_conv_b"][None, :],
                p["w_ffn_down"].astype(BF16), seq, tm=1024)


def kernel(x, norm1_w, w_in, q_norm_w, k_norm_w, rel_bias, lambda_q1, lambda_k1, lambda_q2, lambda_k2, subln_w, w_attn_out, ssm_conv_w, ssm_conv_b, dt_bias_f, a_log_f, dt_bias_b, a_log_b, d_skip, ssm_norm_w, w_ssm_out, w_out, norm2_w, w_ffn_up, ffn_conv_w, ffn_conv_b, w_ffn_down):
    batch, seq, d = x.shape
    layered = dict(norm1_w=norm1_w, w_in=w_in, q_norm_w=q_norm_w, k_norm_w=k_norm_w, lambda_q1=lambda_q1,
                   lambda_k1=lambda_k1, lambda_q2=lambda_q2, lambda_k2=lambda_k2, subln_w=subln_w,
                   w_attn_out=w_attn_out, ssm_conv_w=ssm_conv_w, ssm_conv_b=ssm_conv_b, dt_bias_f=dt_bias_f,
                   a_log_f=a_log_f, dt_bias_b=dt_bias_b, a_log_b=a_log_b, d_skip=d_skip, ssm_norm_w=ssm_norm_w,
                   w_ssm_out=w_ssm_out, w_out=w_out, norm2_w=norm2_w, w_ffn_up=w_ffn_up, ffn_conv_w=ffn_conv_w,
                   ffn_conv_b=ffn_conv_b, w_ffn_down=w_ffn_down)
    x2d = x.reshape(batch * seq, d)
    for layer in range(norm1_w.shape[0]):
        p = {k: v[layer] for k, v in layered.items()}
        p["rel_bias"] = rel_bias
        lambda_init = 0.8 - 0.6 * math.exp(-0.3 * layer)
        x2d = _layer(x2d, batch, seq, lambda_init, p)
    return x2d.reshape(batch, seq, d)
```

```python
import functools
import math

import jax
import jax.numpy as jnp
from jax import lax
from jax.experimental import pallas as pl
from jax.experimental.pallas import tpu as pltpu

F32 = jnp.float32
BF16 = jnp.bfloat16

HEADS = 8
HEAD_DIM = 64
SSM_HEADS = 32
SSM_HEAD_DIM = 64
SSM_GROUPS = 8
SSM_STATE = 128
SSM_INNER = SSM_HEADS * SSM_HEAD_DIM
GROUP_W = SSM_INNER // SSM_GROUPS
CONV_DIM = SSM_INNER + 2 * SSM_GROUPS * SSM_STATE
CHUNK = 128
SSM_SHIFTS = (-2, -1, 1)
HALO = 16
RMS_EPS = 1e-6
LOG2E = 1.4426950408889634

REL_THRESHOLDS = (12, 16, 23, 32, 46, 64, 91)
REL_FAR = 129

LANES = 128
VMEM_LIMIT = 52 * 1024 * 1024
VMEM_LIMIT_ATTN = 60 * 1024 * 1024

MXU_DIM = 256

COL_XBC = 0
COL_Z = 4096
COL_GATE = 6144
COL_V = 8192
PROJ_W = 9216
PROJ_TILE = 1024


def _cparams(n_axes, vmem_limit=VMEM_LIMIT):
    return pltpu.CompilerParams(dimension_semantics=("arbitrary",) * n_axes, vmem_limit_bytes=vmem_limit)


def _rms(x, w):
    return x * lax.rsqrt(jnp.mean(x * x, axis=-1, keepdims=True) + RMS_EPS) * w


def _split_bf16(x):
    hi = x.astype(BF16)
    lo = (x - hi.astype(F32)).astype(BF16)
    return hi, lo


def _silu(x):
    return x / (1.0 + jnp.exp(-x))


def _in_proj_kernel(x_ref, nw_ref, w_ref, wdt_ref, proj_ref, dt_ref, h_ref):
    @pl.when(pl.program_id(1) == 0)
    def _():
        h = _rms(x_ref[...], nw_ref[...]).astype(BF16)
        h_ref[...] = h
        dt_ref[...] = jnp.dot(h, wdt_ref[...], preferred_element_type=F32)

    proj_ref[...] = jnp.dot(h_ref[...], w_ref[...], preferred_element_type=F32).astype(BF16)


def _in_proj(x2d, norm_w, w_main, w_dt, tm):
    m, d = x2d.shape
    return pl.pallas_call(
        _in_proj_kernel,
        grid=(m // tm, PROJ_W // PROJ_TILE),
        in_specs=[
            pl.BlockSpec((tm, d), lambda i, j: (i, 0)),
            pl.BlockSpec((1, d), lambda i, j: (0, 0)),
            pl.BlockSpec((d, PROJ_TILE), lambda i, j: (0, j)),
            pl.BlockSpec((d, LANES), lambda i, j: (0, 0)),
        ],
        out_specs=[
            pl.BlockSpec((tm, PROJ_TILE), lambda i, j: (i, j)),
            pl.BlockSpec((tm, LANES), lambda i, j: (i, 0)),
            pl.BlockSpec((tm, d), lambda i, j: (i, 0)),
        ],
        out_shape=[
            jax.ShapeDtypeStruct((m, PROJ_W), BF16),
            jax.ShapeDtypeStruct((m, LANES), F32),
            jax.ShapeDtypeStruct((m, d), BF16),
        ],
        compiler_params=_cparams(2),
        name="in_proj",
    )(x2d, norm_w, w_main, w_dt)


def _qk_proj_kernel(h_ref, w_ref, qkw_ref, gsum_ref, o_ref):
    acc = jnp.dot(h_ref[...], w_ref[...], preferred_element_type=F32)
    for c in range(PROJ_TILE // MXU_DIM):
        cs = slice(c * MXU_DIM, (c + 1) * MXU_DIM)
        a = acc[:, cs]
        ss = jnp.dot((a * a).astype(BF16), gsum_ref[...], preferred_element_type=F32)
        o_ref[:, cs] = (a * lax.rsqrt(ss * (1.0 / HEAD_DIM) + RMS_EPS) * qkw_ref[0, :, cs]).astype(BF16)


def _qk_proj(h, w_qk, qkw, gsum, tm):
    m, d = h.shape
    return pl.pallas_call(
        _qk_proj_kernel,
        grid=(m // tm, 2),
        in_specs=[
            pl.BlockSpec((tm, d), lambda i, j: (i, 0)),
            pl.BlockSpec((d, PROJ_TILE), lambda i, j: (0, j)),
            pl.BlockSpec((1, 1, PROJ_TILE), lambda i, j: (j, 0, 0)),
            pl.BlockSpec((MXU_DIM, MXU_DIM), lambda i, j: (0, 0)),
        ],
        out_specs=pl.BlockSpec((tm, PROJ_TILE), lambda i, j: (i, j)),
        out_shape=jax.ShapeDtypeStruct((m, 2 * PROJ_TILE), BF16),
        compiler_params=_cparams(2),
        name="qk_proj",
    )(h, w_qk, qkw, gsum)


def _bias_tiles_kernel(table_ref, out_ref):
    h = pl.program_id(0)
    r = lax.broadcasted_iota(jnp.int32, (LANES, LANES), 0)
    c = lax.broadcasted_iota(jnp.int32, (LANES, LANES), 1)
    for d in range(3):
        rel = (d - 1) * LANES + c - r
        n = jnp.abs(rel)
        large = jnp.full((LANES, LANES), 8, jnp.int32)
        for t in REL_THRESHOLDS:
            large = large + jnp.where(n >= t, 1, 0)
        bucket = jnp.where(rel > 0, 16, 0) + jnp.where(n < 8, n, large)
        val = jnp.zeros((LANES, LANES), F32)
        for bkt in range(32):
            val = jnp.where(bucket == bkt, table_ref[bkt, h], val)
        out_ref[0, d] = val * LOG2E


def _bias_tiles(rel_bias):
    return pl.pallas_call(
        _bias_tiles_kernel,
        grid=(HEADS,),
        in_specs=[pl.BlockSpec(memory_space=pltpu.SMEM)],
        out_specs=pl.BlockSpec((1, 3, LANES, LANES), lambda h: (h, 0, 0, 0)),
        out_shape=jax.ShapeDtypeStruct((HEADS, 3, LANES, LANES), F32),
        compiler_params=_cparams(1),
        name="bias_tiles",
    )(rel_bias)


def _near_bias_tiles(tq, tk):
    nbq, nbk = tq // LANES, tk // LANES
    table = {}
    for a in range(tk // tq):
        for d in (-1, 0, 1):
            ob = d * nbk - a * nbq
            if ob + nbk - 1 >= -1 and ob - (nbq - 1) <= 1:
                table[(a, d)] = (2 + len(table), ob)
    return table


def _attn_kernel(lam_ref, table_ref, q_ref, k_ref, v_ref, bsm_ref, subln_ref, o_ref,
                 s_scr, p_scr, mcur_scr, alpha_scr, m_scr, acc_scr, bias_scr, *, tq, tk, seq, out_scale):
    nk = seq // tk
    ratio = tk // tq
    near = _near_bias_tiles(tq, tk)
    h = pl.program_id(0)
    b = pl.program_id(1)
    c_lo = table_ref[15, h] * LOG2E
    c_hi = table_ref[31, h] * LOG2E

    @pl.when(b == 0)
    def _():
        ones = jnp.ones((tq, tk), F32)
        bias_scr[0] = ones * c_lo
        bias_scr[1] = ones * c_hi
        for idx, ob in near.values():
            for rb in range(tq // LANES):
                for cb in range(tk // LANES):
                    off = ob + cb - rb
                    if abs(off) <= 1:
                        blk = bsm_ref[0, off + 1]
                    else:
                        blk = jnp.ones((LANES, LANES), F32) * (c_lo if off < 0 else c_hi)
                    bias_scr[idx, rb * LANES:(rb + 1) * LANES, cb * LANES:(cb + 1) * LANES] = blk

    lane = lax.broadcasted_iota(jnp.int32, (tq, LANES), 1)
    zero = jnp.zeros((tq, LANES), BF16)
    ones_blk = jnp.ones((tk, LANES), BF16)

    first_near = {a: min(d for (a0, d) in near if a0 == a) for a in range(ratio)}
    n_e = max(sum(1 for (a0, _) in near if a0 == a) for a in range(ratio))

    def explicit_start(i):
        c = i // ratio
        a = i - c * ratio
        lo = c + first_near[0]
        for a0 in range(1, ratio):
            lo = jnp.where(a == a0, c + first_near[a0], lo)
        return jnp.clip(lo, 0, nk - n_e)

    def key_tile(i, pos):
        e0 = explicit_start(i)
        if pos < n_e:
            return e0 + pos, None
        k = pos - n_e
        j = k + jnp.where(k >= e0, n_e, 0)
        return j, jnp.where(j < e0, c_lo, c_hi)

    def bias_kind(i, j):
        c = i // ratio
        a = i - c * ratio
        d = j - c
        kind = jnp.where(d < 0, 0, 1)
        for (a0, d0), (idx, _) in near.items():
            kind = jnp.where((a == a0) & (d == d0), idx, kind)
        return kind

    def scores(i, pos, slot):
        j, const = key_tile(i, pos)
        q = q_ref[pl.ds(pl.multiple_of(i * tq, tq), tq), :]
        qs = jnp.concatenate([jnp.where(lane < HEAD_DIM, q, zero), jnp.where(lane >= HEAD_DIM, q, zero)], axis=0)
        k = k_ref[pl.ds(pl.multiple_of(j * tk, tk), tk), :]
        s = lax.dot_general(qs, k, (((1,), (1,)), ((), ())), preferred_element_type=F32)
        if const is None:
            kind = bias_kind(i, j)
            s = jnp.concatenate([s[0:tq] + bias_scr[kind], s[tq:2 * tq] + bias_scr[kind]], axis=0)
            row_max = jnp.max(s, axis=1, keepdims=True)
        else:
            row_max = jnp.max(s, axis=1, keepdims=True) + const
        s_scr[slot] = s
        mcur_scr[slot] = jnp.broadcast_to(row_max, (2 * tq, LANES))

    def softmax(i, pos, slot):
        _, const = key_tile(i, pos)
        if pos == 0:
            m_next = mcur_scr[slot]
        else:
            m_prev = m_scr[...]
            m_next = jnp.maximum(m_prev, mcur_scr[slot])
            alpha_scr[slot] = jnp.exp2(m_prev - m_next)
        shift = m_next if const is None else m_next - const
        p_scr[slot] = jnp.exp2(s_scr[slot] - jnp.concatenate([shift] * (tk // LANES), axis=1)).astype(BF16)
        m_scr[...] = m_next

    def values(i, pos, slot):
        j, _ = key_tile(i, pos)
        vaug = jnp.concatenate([v_ref[pl.ds(pl.multiple_of(j * tk, tk), tk), :], ones_blk], axis=1)
        pv = jnp.dot(p_scr[slot], vaug, preferred_element_type=F32)
        if pos == 0:
            acc_scr[...] = pv
        else:
            acc_scr[...] = acc_scr[...] * jnp.concatenate([alpha_scr[slot]] * 2, axis=1) + pv

    def finalize(i):
        acc = acc_scr[...]
        o12 = acc[:, 0:LANES] / acc[:, LANES:2 * LANES]
        o = o12[0:tq] - lam_ref[0] * o12[tq:2 * tq]
        o_ref[pl.ds(pl.multiple_of(i * tq, tq), tq), :] = (_rms(o, subln_ref[...]) * out_scale).astype(BF16)

    nq = seq // tq
    scores(0, 0, 0)
    softmax(0, 0, 0)
    scores(0, 1, 1)

    def body(g, carry):
        nxt = jnp.minimum(g + 1, nq - 1)
        for r in range(nk):
            values(g, r, r % 2)
            softmax(g if r + 1 < nk else nxt, (r + 1) % nk, (r + 1) % 2)
            scores(g if r + 2 < nk else nxt, (r + 2) % nk, r % 2)
        finalize(g)
        return carry

    lax.fori_loop(0, nq, body, 0)


def _diff_attn(qk, proj, bias_small, rel_bias, lam, subln_w, batch, seq, tq, tk, out_scale):
    assert seq % tk == 0 and tk % tq == 0 and (seq // tk) % 2 == 0
    qb, kb, vb = 0, HEADS, COL_V // LANES
    kern = functools.partial(_attn_kernel, tq=tq, tk=tk, seq=seq, out_scale=out_scale)
    n_bias = 2 + len(_near_bias_tiles(tq, tk))
    return pl.pallas_call(
        kern,
        grid=(HEADS, batch),
        in_specs=[
            pl.BlockSpec(memory_space=pltpu.SMEM),
            pl.BlockSpec(memory_space=pltpu.SMEM),
            pl.BlockSpec((seq, LANES), lambda h, b: (b, qb + h)),
            pl.BlockSpec((seq, LANES), lambda h, b: (b, kb + h)),
            pl.BlockSpec((seq, LANES), lambda h, b: (b, vb + h)),
            pl.BlockSpec((1, 3, LANES, LANES), lambda h, b: (h, 0, 0, 0)),
            pl.BlockSpec((1, LANES), lambda h, b: (0, 0)),
        ],
        out_specs=pl.BlockSpec((seq, LANES), lambda h, b: (b, h)),
        out_shape=jax.ShapeDtypeStruct((batch * seq, HEADS * LANES), BF16),
        scratch_shapes=[
            pltpu.VMEM((2, 2 * tq, tk), F32),
            pltpu.VMEM((2, 2 * tq, tk), BF16),
            pltpu.VMEM((2, 2 * tq, LANES), F32),
            pltpu.VMEM((2, 2 * tq, LANES), F32),
            pltpu.VMEM((2 * tq, LANES), F32),
            pltpu.VMEM((2 * tq, 2 * LANES), F32),
            pltpu.VMEM((n_bias, tq, tk), F32),
        ],
        compiler_params=_cparams(2, VMEM_LIMIT_ATTN),
        name="diff_attn",
    )(lam, rel_bias, qk, qk, proj, bias_small, subln_w)


def _tri(upper):
    r = lax.broadcasted_iota(jnp.int32, (CHUNK, CHUNK), 0)
    c = lax.broadcasted_iota(jnp.int32, (CHUNK, CHUNK), 1)
    return (c >= r) if upper else (c <= r)


def _tri_dot(mask, x):
    m = jnp.where(mask, 1.0, 0.0).astype(BF16)
    hi, lo = _split_bf16(x)
    return jnp.dot(m, hi, preferred_element_type=F32) + jnp.dot(m, lo, preferred_element_type=F32)


PREP_CHUNKS = 16


def _ssd_prep_kernel(dt_ref, dtb_ref, alog_ref, zt_ref, cs_ref, dd_ref, ww_ref):
    nh = SSM_HEADS
    lane = lax.broadcasted_iota(jnp.int32, (CHUNK, LANES), 1)
    tril = _tri(False)
    triu = _tri(True)
    for k in range(PREP_CHUNKS):
        rows = slice(k * CHUNK, (k + 1) * CHUNK)
        raw = dt_ref[rows, :] + dtb_ref[...]
        dt = jnp.maximum(raw, 0.0) + jnp.log1p(jnp.exp(-jnp.abs(raw)))
        dt = jnp.where(lane < 2 * nh, dt, 0.0)
        da = dt * -jnp.exp(alog_ref[...])
        acs = _tri_dot(tril, da)
        rcs = _tri_dot(triu, da)
        cs = jnp.where(lane < nh, acs, rcs)
        end = jnp.where(lane < nh, acs[CHUNK - 1:CHUNK, :], rcs[0:1, :])
        cs_ref[rows, :] = cs
        dd_ref[rows, :] = jnp.exp(cs)
        ww_ref[rows, :] = jnp.exp(end - cs) * dt
        zt_ref[rows, :] = jnp.where(lane < 2 * nh, dt, pltpu.roll(cs, 2 * nh, 1)).T


def _ssd_prep(dt_raw, dt_bias, a_log):
    m = dt_raw.shape[0]
    rows = PREP_CHUNKS * CHUNK
    blk = pl.BlockSpec((rows, LANES), lambda i: (i, 0))
    vec = pl.BlockSpec((1, LANES), lambda i: (0, 0))
    return pl.pallas_call(
        _ssd_prep_kernel,
        grid=(m // rows,),
        in_specs=[blk, vec, vec],
        out_specs=[blk] * 4,
        out_shape=[jax.ShapeDtypeStruct((m, LANES), F32)] * 4,
        compiler_params=_cparams(1),
        name="ssd_prep",
    )(dt_raw, dt_bias, a_log)


def _expand_operand(x):
    hi, lo = _split_bf16(x)
    return jnp.concatenate([hi, lo], axis=1)


def _expand(x_hl, e2_ref, gs):
    return jnp.dot(x_hl, e2_ref[:, gs], preferred_element_type=F32)


def _shift_rows(u, halo, offset):
    n = u.shape[0]
    if offset == 0:
        return u
    rolled = pltpu.roll(u, (-offset) % n, 0)
    row8 = lax.broadcasted_iota(jnp.int32, (8, u.shape[1]), 0)
    if offset < 0:
        hfix = pltpu.roll(halo, (-offset) % 8, 0)
        first = jnp.where(row8 < -offset, hfix, rolled[0:8])
        return jnp.concatenate([first, rolled[8:n]], axis=0)
    hfix = pltpu.roll(halo, (8 - offset) % 8, 0)
    last = jnp.where(row8 >= 8 - offset, hfix, rolled[n - 8:n])
    return jnp.concatenate([rolled[0:n - 8], last], axis=0)


def _shift_select(n, shifts):
    t = jnp.arange(len(shifts) * n)
    src = t % n + HALO + jnp.repeat(jnp.asarray(shifts), n)
    return (src[:, None] == jnp.arange(n + 2 * HALO)[None, :]).astype(BF16)


SCAN_CHUNKS = 4


def _ssd_bwd_kernel(xc_ref, xp_ref, xn_ref, dd_ref, ww_ref, cw_ref, cb_ref, e2b_ref, sel_ref,
                    xact_ref, ybi_ref, state_scr, *, n_blocks):
    c = pl.program_id(1)
    cr = n_blocks - 1 - c

    @pl.when(c == 0)
    def _():
        state_scr[...] = jnp.zeros(state_scr.shape, F32)

    keep_prev = jnp.where(cr > 0, 1.0, 0.0).astype(BF16)
    keep_next = jnp.where(cr < n_blocks - 1, 1.0, 0.0).astype(BF16)
    slab = 512
    for sl in range(CONV_DIM // slab):
        cs = slice(sl * slab, (sl + 1) * slab)
        ext = jnp.concatenate([xp_ref[:, cs] * keep_prev, xc_ref[:, cs], xn_ref[:, cs] * keep_next], axis=0)
        for k in range(SCAN_CHUNKS):
            rows = slice(k * CHUNK, (k + 1) * CHUNK)
            sh = jnp.dot(sel_ref[...], ext[k * CHUNK:(k + 1) * CHUNK + 2 * HALO], preferred_element_type=F32)
            acc = cb_ref[:, cs] + xc_ref[rows, cs].astype(F32) * cw_ref[2:3, cs]
            for j, off in enumerate(SSM_SHIFTS):
                acc = acc + sh[j * CHUNK:(j + 1) * CHUNK] * cw_ref[off + 2:off + 3, cs]
            xact_ref[rows, cs] = _silu(acc).astype(BF16)

    for k in reversed(range(SCAN_CHUNKS)):
        rows = slice(k * CHUNK, (k + 1) * CHUNK)
        everything = slice(0, SSM_INNER)
        decay = _expand(_expand_operand(dd_ref[rows, :]), e2b_ref, everything)
        chunk_decay = decay[0:1, :]
        xs = xact_ref[rows, 0:SSM_INNER].astype(F32)
        xw = (xs * _expand(_expand_operand(ww_ref[rows, :]), e2b_ref, everything)).astype(BF16)
        for g in range(SSM_GROUPS):
            gs = slice(g * GROUP_W, (g + 1) * GROUP_W)
            bg = xact_ref[rows, SSM_INNER + g * SSM_STATE:SSM_INNER + (g + 1) * SSM_STATE]
            cg = xact_ref[rows, SSM_INNER + (SSM_GROUPS + g) * SSM_STATE:SSM_INNER + (SSM_GROUPS + g + 1) * SSM_STATE]
            st = state_scr[:, gs]
            yb = jnp.dot(cg, st.astype(BF16), preferred_element_type=F32) * decay[:, gs]
            ybi_ref[rows, gs] = yb.astype(BF16)
            upd = lax.dot_general(bg, xw[:, gs], (((0,), (0,)), ((), ())), preferred_element_type=F32)
            state_scr[:, gs] = st * chunk_decay[:, gs] + upd


def _ssd_bwd(proj, dd, ww, conv_w, conv_b, e2b, batch, seq):
    rows = SCAN_CHUNKS * CHUNK
    n_blocks = seq // rows
    sub = rows // HALO
    last16 = batch * seq // HALO - 1
    kern = functools.partial(_ssd_bwd_kernel, n_blocks=n_blocks)
    select = _shift_select(CHUNK, SSM_SHIFTS)

    def cur(b, c):
        return b * n_blocks + (n_blocks - 1 - c)

    return pl.pallas_call(
        kern,
        grid=(batch, n_blocks),
        in_specs=[
            pl.BlockSpec((rows, CONV_DIM), lambda b, c: (cur(b, c), 0)),
            pl.BlockSpec((HALO, CONV_DIM), lambda b, c: (jnp.maximum(cur(b, c) * sub - 1, 0), 0)),
            pl.BlockSpec((HALO, CONV_DIM), lambda b, c: (jnp.minimum((cur(b, c) + 1) * sub, last16), 0)),
            pl.BlockSpec((rows, LANES), lambda b, c: (cur(b, c), 0)),
            pl.BlockSpec((rows, LANES), lambda b, c: (cur(b, c), 0)),
            pl.BlockSpec((4, CONV_DIM), lambda b, c: (0, 0)),
            pl.BlockSpec((1, CONV_DIM), lambda b, c: (0, 0)),
            pl.BlockSpec((2 * LANES, SSM_INNER), lambda b, c: (0, 0)),
            pl.BlockSpec(select.shape, lambda b, c: (0, 0)),
        ],
        out_specs=[
            pl.BlockSpec((rows, CONV_DIM), lambda b, c: (cur(b, c), 0)),
            pl.BlockSpec((rows, SSM_INNER), lambda b, c: (cur(b, c), 0)),
        ],
        out_shape=[
            jax.ShapeDtypeStruct((batch * seq, CONV_DIM), BF16),
            jax.ShapeDtypeStruct((batch * seq, SSM_INNER), BF16),
        ],
        scratch_shapes=[pltpu.VMEM((SSM_STATE, SSM_INNER), F32)],
        compiler_params=_cparams(2),
        name="ssd_bwd",
    )(proj, proj, proj, dd, ww, conv_w, conv_b, e2b, select)


def _ssd_fwd_kernel(xact_ref, ybi_ref, zt_ref, cs_ref, dd_ref, ww_ref, dsk_ref, e2f_ref, out_ref, state_scr):
    @pl.when(pl.program_id(1) == 0)
    def _():
        state_scr[...] = jnp.zeros(state_scr.shape, F32)

    nh = SSM_HEADS
    lane_w = lax.broadcasted_iota(jnp.int32, (CHUNK, LANES), 1)
    row_i = lax.broadcasted_iota(jnp.int32, (CHUNK, CHUNK), 0)
    col_i = lax.broadcasted_iota(jnp.int32, (CHUNK, CHUNK), 1)
    lower = col_i < row_i
    upper = col_i > row_i

    for k in range(SCAN_CHUNKS):
        rows = slice(k * CHUNK, (k + 1) * CHUNK)
        zt = zt_ref.at[rows]
        cs = cs_ref[rows, :]
        dd_hl = _expand_operand(dd_ref[rows, :])
        ww_hl = _expand_operand(ww_ref[rows, :])
        for g in range(SSM_GROUPS):
            gs = slice(g * GROUP_W, (g + 1) * GROUP_W)
            bg = xact_ref[rows, SSM_INNER + g * SSM_STATE:SSM_INNER + (g + 1) * SSM_STATE]
            cg = xact_ref[rows, SSM_INNER + (SSM_GROUPS + g) * SSM_STATE:SSM_INNER + (SSM_GROUPS + g + 1) * SSM_STATE]
            xs_bf = xact_ref[rows, gs]
            xs = xs_bf.astype(F32)
            decay = _expand(dd_hl, e2f_ref, gs)
            xw = (xs * _expand(ww_hl, e2f_ref, gs)).astype(BF16)
            cb = lax.dot_general(cg, bg, (((1,), (1,)), ((), ())), preferred_element_type=F32)
            ys = []
            for r in range(GROUP_W // SSM_HEAD_DIM):
                h = g * (GROUP_W // SSM_HEAD_DIM) + r
                seg_f = cs[:, h:h + 1] - zt[2 * nh + h:2 * nh + h + 1, :]
                seg_b = cs[:, nh + h:nh + h + 1] - zt[3 * nh + h:3 * nh + h + 1, :]
                dt_f = zt[h:h + 1, :]
                dt_b = zt[nh + h:nh + h + 1, :]
                dt_sel = jnp.where(lower, dt_f, jnp.where(upper, dt_b, dt_f + dt_b))
                mat = (cb * (jnp.exp(jnp.where(upper, seg_b, seg_f)) * dt_sel)).astype(BF16)
                pair = xs_bf[:, (r // 2) * LANES:(r // 2 + 1) * LANES]
                ys.append(jnp.dot(mat, pair, preferred_element_type=F32))
            y = jnp.concatenate([jnp.where(lane_w < SSM_HEAD_DIM, ys[0], ys[1]),
                                 jnp.where(lane_w < SSM_HEAD_DIM, ys[2], ys[3])], axis=1)
            st = state_scr[:, gs]
            y = y + jnp.dot(cg, st.astype(BF16), preferred_element_type=F32) * decay
            upd = lax.dot_general(bg, xw, (((0,), (0,)), ((), ())), preferred_element_type=F32)
            state_scr[:, gs] = st * decay[CHUNK - 1:CHUNK, :] + upd
            y = y + ybi_ref[rows, gs].astype(F32) + dsk_ref[:, gs] * xs
            out_ref[rows, gs] = y.astype(BF16)


def _ssd_fwd(xact, ybi, zt, cs, dd, ww, d_skip, e2f, batch, seq):
    rows = SCAN_CHUNKS * CHUNK
    n_blocks = seq // rows
    small = pl.BlockSpec((rows, LANES), lambda b, c: (b * n_blocks + c, 0))
    return pl.pallas_call(
        _ssd_fwd_kernel,
        grid=(batch, n_blocks),
        in_specs=[
            pl.BlockSpec((rows, CONV_DIM), lambda b, c: (b * n_blocks + c, 0)),
            pl.BlockSpec((rows, SSM_INNER), lambda b, c: (b * n_blocks + c, 0)),
            small, small, small, small,
            pl.BlockSpec((1, SSM_INNER), lambda b, c: (0, 0)),
            pl.BlockSpec((2 * LANES, SSM_INNER), lambda b, c: (0, 0)),
        ],
        out_specs=pl.BlockSpec((rows, SSM_INNER), lambda b, c: (b * n_blocks + c, 0)),
        out_shape=jax.ShapeDtypeStruct((batch * seq, SSM_INNER), BF16),
        scratch_shapes=[pltpu.VMEM((SSM_STATE, SSM_INNER), F32)],
        compiler_params=_cparams(2),
        name="ssd_fwd",
    )(xact, ybi, zt, cs, dd, ww, d_skip, e2f)


def _mix_out_kernel(x_ref, ao_ref, so_ref, z_ref, nw_ref, ga_ref, gs_ref, wa_ref, ws_ref, wo_ref, n2_ref, x1_ref,
                    h2_ref):
    attn = jnp.dot(ao_ref[...], wa_ref[...], preferred_element_type=F32)
    ssd = None
    for g in range(SSM_GROUPS):
        gs = slice(g * GROUP_W, (g + 1) * GROUP_W)
        y = so_ref[:, gs].astype(F32) * _silu(z_ref[:, gs].astype(F32))
        part = jnp.dot(_rms(y, nw_ref[:, gs]).astype(BF16), ws_ref[gs, :], preferred_element_type=F32)
        ssd = part if ssd is None else ssd + part
    mixed = (jax.nn.sigmoid(ga_ref[...].astype(F32)) * attn
             + jax.nn.sigmoid(gs_ref[...].astype(F32)) * ssd)
    x1 = x_ref[...] + jnp.dot(mixed.astype(BF16), wo_ref[...], preferred_element_type=F32)
    x1_ref[...] = x1
    h2_ref[...] = _rms(x1, n2_ref[...]).astype(BF16)


def _mix_out(x2d, attn_o, ssd_o, proj, ssm_norm_w, w_attn, w_ssm, w_out, norm2_w, tm):
    m, d = x2d.shape
    gb = COL_GATE // d
    zb = COL_Z // SSM_INNER
    return pl.pallas_call(
        _mix_out_kernel,
        grid=(m // tm,),
        in_specs=[
            pl.BlockSpec((tm, d), lambda i: (i, 0)),
            pl.BlockSpec((tm, attn_o.shape[1]), lambda i: (i, 0)),
            pl.BlockSpec((tm, ssd_o.shape[1]), lambda i: (i, 0)),
            pl.BlockSpec((tm, SSM_INNER), lambda i: (i, zb)),
            pl.BlockSpec((1, SSM_INNER), lambda i: (0, 0)),
            pl.BlockSpec((tm, d), lambda i: (i, gb)),
            pl.BlockSpec((tm, d), lambda i: (i, gb + 1)),
            pl.BlockSpec(w_attn.shape, lambda i: (0, 0)),
            pl.BlockSpec(w_ssm.shape, lambda i: (0, 0)),
            pl.BlockSpec(w_out.shape, lambda i: (0, 0)),
            pl.BlockSpec((1, d), lambda i: (0, 0)),
        ],
        out_specs=[pl.BlockSpec((tm, d), lambda i: (i, 0)), pl.BlockSpec((tm, d), lambda i: (i, 0))],
        out_shape=[jax.ShapeDtypeStruct((m, d), F32), jax.ShapeDtypeStruct((m, d), BF16)],
        compiler_params=_cparams(1),
        name="mix_out",
    )(x2d, attn_o, ssd_o, proj, ssm_norm_w, proj, proj, w_attn, w_ssm, w_out, norm2_w)


FFN_CW = 256


def _ffn_kernel(x1_ref, h_ref, hp_ref, hn_ref, wu_ref, cw_ref, cb_ref, wd_ref, o_ref, act_scr, *, tiles_per_seq):
    i = pl.program_id(0)
    tm = h_ref.shape[0]
    f = wd_ref.shape[0]
    pos = i % tiles_per_seq
    keep_prev = jnp.where(pos > 0, 1.0, 0.0)
    keep_next = jnp.where(pos < tiles_per_seq - 1, 1.0, 0.0)
    h_ext = jnp.concatenate([hp_ref[...], h_ref[...], hn_ref[...]], axis=0)

    def conv(col):
        cs = slice(col, col + FFN_CW)
        u_ext = jnp.dot(h_ext, wu_ref[:, cs], preferred_element_type=F32)
        u = u_ext[HALO:HALO + tm]
        up = u_ext[HALO - 8:HALO] * keep_prev
        un = u_ext[HALO + tm:HALO + tm + 8] * keep_next
        return (cb_ref[:, cs] + _shift_rows(u, up, -1) * cw_ref[0:1, cs] + u * cw_ref[1:2, cs]
                + _shift_rows(u, un, 1) * cw_ref[2:3, cs])

    for c in range(f // FFN_CW):
        act = _silu(conv(c * FFN_CW)) * conv(f + c * FFN_CW)
        act_scr[:, c * FFN_CW:(c + 1) * FFN_CW] = act.astype(BF16)
    o_ref[...] = x1_ref[...] + jnp.dot(act_scr[...], wd_ref[...], preferred_element_type=F32)


def _ffn(x1, h2, w_up, conv_w, conv_b, w_down, seq, tm):
    m, d = x1.shape
    f = w_down.shape[0]
    sub = tm // HALO
    last = m // HALO - 1
    kern = functools.partial(_ffn_kernel, tiles_per_seq=seq // tm)
    resident = pl.Buffered(1)
    return pl.pallas_call(
        kern,
        grid=(m // tm,),
        in_specs=[
            pl.BlockSpec((tm, d), lambda i: (i, 0)),
            pl.BlockSpec((tm, d), lambda i: (i, 0)),
            pl.BlockSpec((HALO, d), lambda i: (jnp.maximum(i * sub - 1, 0), 0)),
            pl.BlockSpec((HALO, d), lambda i: (jnp.minimum((i + 1) * sub, last), 0)),
            pl.BlockSpec(w_up.shape, lambda i: (0, 0), pipeline_mode=resident),
            pl.BlockSpec(conv_w.shape, lambda i: (0, 0)),
            pl.BlockSpec(conv_b.shape, lambda i: (0, 0)),
            pl.BlockSpec(w_down.shape, lambda i: (0, 0), pipeline_mode=resident),
        ],
        out_specs=pl.BlockSpec((tm, d), lambda i: (i, 0)),
        out_shape=jax.ShapeDtypeStruct((m, d), F32),
        scratch_shapes=[pltpu.VMEM((tm, f), BF16)],
        compiler_params=_cparams(1),
        name="ffn",
    )(x1, h2, h2, h2, w_up, conv_w, conv_b, w_down)


def _head_expand(first_row):
    rows = jnp.arange(2 * LANES)[:, None] % LANES
    cols = jnp.arange(SSM_INNER)[None, :] // SSM_HEAD_DIM
    return (rows == cols + first_row).astype(BF16)


def _layer(x2d, batch, seq, lambda_init, p):
    d = x2d.shape[1]
    w_in = p["w_in"].astype(BF16)
    sizes = (HEADS * 2 * HEAD_DIM,) * 3 + (SSM_INNER, CONV_DIM, 2 * SSM_HEADS, 2 * d)
    offs = [0]
    for s in sizes:
        offs.append(offs[-1] + s)
    wq, wk, wv, wz, wxbc, wdt, wg = (w_in[:, offs[n]:offs[n + 1]] for n in range(7))
    w_main = jnp.concatenate([wxbc, wz, wg, wv], axis=1)
    w_qk = jnp.concatenate([wq, wk], axis=1)
    w_dt = jnp.pad(wdt, ((0, 0), (0, LANES - wdt.shape[1])))
    qkw = jnp.stack([jnp.tile(p["q_norm_w"], 2 * HEADS) * (HEAD_DIM ** -0.5 * LOG2E),
                     jnp.tile(p["k_norm_w"], 2 * HEADS)])[:, None, :]
    gi = jnp.arange(MXU_DIM) // HEAD_DIM
    gsum = (gi[:, None] == gi[None, :]).astype(BF16)

    proj, dt_raw, h1 = _in_proj(x2d, p["norm1_w"][None, :], w_main, w_dt, tm=2048)
    qk = _qk_proj(h1, w_qk, qkw, gsum, tm=2048)

    lam = (jnp.exp(jnp.sum(p["lambda_q1"] * p["lambda_k1"])) - jnp.exp(jnp.sum(p["lambda_q2"] * p["lambda_k2"]))
           + lambda_init).reshape(1).astype(F32)
    bias_small = _bias_tiles(p["rel_bias"])
    attn_o = _diff_attn(qk, proj, bias_small, p["rel_bias"], lam, p["subln_w"][None, :], batch, seq,
                        tq=512, tk=1024, out_scale=1.0 - lambda_init)

    pad = LANES - 2 * SSM_HEADS
    dt_bias = jnp.pad(jnp.concatenate([p["dt_bias_f"], p["dt_bias_b"]]), (0, pad))[None, :]
    a_log = jnp.pad(jnp.concatenate([p["a_log_f"], p["a_log_b"]]), (0, pad))[None, :]
    zt, cs, dd, ww = _ssd_prep(dt_raw, dt_bias, a_log)
    xact, ybi = _ssd_bwd(proj, dd, ww, p["ssm_conv_w"], p["ssm_conv_b"][None, :],
                         _head_expand(SSM_HEADS), batch, seq)
    ssd_y = _ssd_fwd(xact, ybi, zt, cs, dd, ww, jnp.repeat(p["d_skip"], SSM_HEAD_DIM)[None, :],
                     _head_expand(0), batch, seq)

    x1, h2 = _mix_out(x2d, attn_o, ssd_y, proj, p["ssm_norm_w"][None, :], p["w_attn_out"].astype(BF16),
                      p["w_ssm_out"].astype(BF16), p["w_out"].astype(BF16), p["norm2_w"][None, :], tm=512)

    return _ffn(x1, h2, p["w_ffn_up"].astype(BF16), p["ffn_conv_w"], p["ffn_conv_b"][None, :],
                p["w_ffn_down"].astype(BF16), seq, tm=1024)


def kernel(x, norm1_w, w_in, q_norm_w, k_norm_w, rel_bias, lambda_q1, lambda_k1, lambda_q2, lambda_k2, subln_w, w_attn_out, ssm_conv_w, ssm_conv_b, dt_bias_f, a_log_f, dt_bias_b, a_log_b, d_skip, ssm_norm_w, w_ssm_out, w_out, norm2_w, w_ffn_up, ffn_conv_w, ffn_conv_b, w_ffn_down):
    batch, seq, d = x.shape
    layered = dict(norm1_w=norm1_w, w_in=w_in, q_norm_w=q_norm_w, k_norm_w=k_norm_w, lambda_q1=lambda_q1,
                   lambda_k1=lambda_k1, lambda_q2=lambda_q2, lambda_k2=lambda_k2, subln_w=subln_w,
                   w_attn_out=w_attn_out, ssm_conv_w=ssm_conv_w, ssm_conv_b=ssm_conv_b, dt_bias_f=dt_bias_f,
                   a_log_f=a_log_f, dt_bias_b=dt_bias_b, a_log_b=a_log_b, d_skip=d_skip, ssm_norm_w=ssm_norm_w,
                   w_ssm_out=w_ssm_out, w_out=w_out, norm2_w=norm2_w, w_ffn_up=w_ffn_up, ffn_conv_w=ffn_conv_w,
                   ffn_conv_b=ffn_conv_b, w_ffn_down=w_ffn_down)
    x2d = x.reshape(batch * seq, d)
    for layer in range(norm1_w.shape[0]):
        p = {k: v[layer] for k, v in layered.items()}
        p["rel_bias"] = rel_bias
        lambda_init = 0.8 - 0.6 * math.exp(-0.3 * layer)
        x2d = _layer(x2d, batch, seq, lambda_init, p)
    return x2d.reshape(batch, seq, d)
```

```python
import functools
import math

import jax
import jax.numpy as jnp
from jax import lax
from jax.experimental import pallas as pl
from jax.experimental.pallas import tpu as pltpu

F32 = jnp.float32
BF16 = jnp.bfloat16

HEADS = 8
HEAD_DIM = 64
SSM_HEADS = 32
SSM_HEAD_DIM = 64
SSM_GROUPS = 8
SSM_STATE = 128
SSM_INNER = SSM_HEADS * SSM_HEAD_DIM
GROUP_W = SSM_INNER // SSM_GROUPS
CONV_DIM = SSM_INNER + 2 * SSM_GROUPS * SSM_STATE
CHUNK = 128
SSM_SHIFTS = (-2, -1, 1)
HALO = 16
RMS_EPS = 1e-6
LOG2E = 1.4426950408889634

REL_THRESHOLDS = (12, 16, 23, 32, 46, 64, 91)
REL_FAR = 129

LANES = 128
VMEM_LIMIT = 52 * 1024 * 1024
VMEM_LIMIT_ATTN = 60 * 1024 * 1024

MXU_DIM = 256

COL_XBC = 0
COL_Z = 4096
COL_GATE = 6144
COL_V = 8192
PROJ_W = 9216
PROJ_TILE = 1024


def _cparams(n_axes, vmem_limit=VMEM_LIMIT):
    return pltpu.CompilerParams(dimension_semantics=("arbitrary",) * n_axes, vmem_limit_bytes=vmem_limit)


def _rms(x, w):
    return x * lax.rsqrt(jnp.mean(x * x, axis=-1, keepdims=True) + RMS_EPS) * w


def _split_bf16(x):
    hi = x.astype(BF16)
    lo = (x - hi.astype(F32)).astype(BF16)
    return hi, lo


def _silu(x):
    return x / (1.0 + jnp.exp(-x))


def _in_proj_kernel(x_ref, nw_ref, w_ref, wdt_ref, proj_ref, dt_ref, h_ref):
    @pl.when(pl.program_id(1) == 0)
    def _():
        h = _rms(x_ref[...], nw_ref[...]).astype(BF16)
        h_ref[...] = h
        dt_ref[...] = jnp.dot(h, wdt_ref[...], preferred_element_type=F32)

    proj_ref[...] = jnp.dot(h_ref[...], w_ref[...], preferred_element_type=F32).astype(BF16)


def _in_proj(x2d, norm_w, w_main, w_dt, tm):
    m, d = x2d.shape
    return pl.pallas_call(
        _in_proj_kernel,
        grid=(m // tm, PROJ_W // PROJ_TILE),
        in_specs=[
            pl.BlockSpec((tm, d), lambda i, j: (i, 0)),
            pl.BlockSpec((1, d), lambda i, j: (0, 0)),
            pl.BlockSpec((d, PROJ_TILE), lambda i, j: (0, j)),
            pl.BlockSpec((d, LANES), lambda i, j: (0, 0)),
        ],
        out_specs=[
            pl.BlockSpec((tm, PROJ_TILE), lambda i, j: (i, j)),
            pl.BlockSpec((tm, LANES), lambda i, j: (i, 0)),
            pl.BlockSpec((tm, d), lambda i, j: (i, 0)),
        ],
        out_shape=[
            jax.ShapeDtypeStruct((m, PROJ_W), BF16),
            jax.ShapeDtypeStruct((m, LANES), F32),
            jax.ShapeDtypeStruct((m, d), BF16),
        ],
        compiler_params=_cparams(2),
        name="in_proj",
    )(x2d, norm_w, w_main, w_dt)


def _qk_proj_kernel(h_ref, w_ref, qkw_ref, gsum_ref, o_ref):
    acc = jnp.dot(h_ref[...], w_ref[...], preferred_element_type=F32)
    for c in range(PROJ_TILE // MXU_DIM):
        cs = slice(c * MXU_DIM, (c + 1) * MXU_DIM)
        a = acc[:, cs]
        ss = jnp.dot((a * a).astype(BF16), gsum_ref[...], preferred_element_type=F32)
        o_ref[:, cs] = (a * lax.rsqrt(ss * (1.0 / HEAD_DIM) + RMS_EPS) * qkw_ref[0, :, cs]).astype(BF16)


def _qk_proj(h, w_qk, qkw, gsum, tm):
    m, d = h.shape
    return pl.pallas_call(
        _qk_proj_kernel,
        grid=(m // tm, 2),
        in_specs=[
            pl.BlockSpec((tm, d), lambda i, j: (i, 0)),
            pl.BlockSpec((d, PROJ_TILE), lambda i, j: (0, j)),
            pl.BlockSpec((1, 1, PROJ_TILE), lambda i, j: (j, 0, 0)),
            pl.BlockSpec((MXU_DIM, MXU_DIM), lambda i, j: (0, 0)),
        ],
        out_specs=pl.BlockSpec((tm, PROJ_TILE), lambda i, j: (i, j)),
        out_shape=jax.ShapeDtypeStruct((m, 2 * PROJ_TILE), BF16),
        compiler_params=_cparams(2),
        name="qk_proj",
    )(h, w_qk, qkw, gsum)


def _bias_tiles_kernel(table_ref, out_ref):
    h = pl.program_id(0)
    r = lax.broadcasted_iota(jnp.int32, (LANES, LANES), 0)
    c = lax.broadcasted_iota(jnp.int32, (LANES, LANES), 1)
    for d in range(3):
        rel = (d - 1) * LANES + c - r
        n = jnp.abs(rel)
        large = jnp.full((LANES, LANES), 8, jnp.int32)
        for t in REL_THRESHOLDS:
            large = large + jnp.where(n >= t, 1, 0)
        bucket = jnp.where(rel > 0, 16, 0) + jnp.where(n < 8, n, large)
        val = jnp.zeros((LANES, LANES), F32)
        for bkt in range(32):
            val = jnp.where(bucket == bkt, table_ref[bkt, h], val)
        out_ref[0, d] = val * LOG2E


def _bias_tiles(rel_bias):
    return pl.pallas_call(
        _bias_tiles_kernel,
        grid=(HEADS,),
        in_specs=[pl.BlockSpec(memory_space=pltpu.SMEM)],
        out_specs=pl.BlockSpec((1, 3, LANES, LANES), lambda h: (h, 0, 0, 0)),
        out_shape=jax.ShapeDtypeStruct((HEADS, 3, LANES, LANES), F32),
        compiler_params=_cparams(1),
        name="bias_tiles",
    )(rel_bias)


def _near_bias_tiles(tq, tk):
    nbq, nbk = tq // LANES, tk // LANES
    table = {}
    for a in range(tk // tq):
        for d in (-1, 0, 1):
            ob = d * nbk - a * nbq
            if ob + nbk - 1 >= -1 and ob - (nbq - 1) <= 1:
                table[(a, d)] = (2 + len(table), ob)
    return table


SOFTMAX_LEAD = 1
S_SLOTS = 2
P_SLOTS = 2


def _attn_kernel(lam_ref, table_ref, q_ref, k_ref, v_ref, bsm_ref, subln_ref, o_ref,
                 s_scr, p_scr, mcur_scr, alpha_scr, m_scr, acc_scr, bias_scr, *, tq, tk, seq, out_scale):
    nk = seq // tk
    ratio = tk // tq
    near = _near_bias_tiles(tq, tk)
    h = pl.program_id(0)
    b = pl.program_id(1)
    c_lo = table_ref[15, h] * LOG2E
    c_hi = table_ref[31, h] * LOG2E

    @pl.when(b == 0)
    def _():
        ones = jnp.ones((tq, tk), F32)
        bias_scr[0] = ones * c_lo
        bias_scr[1] = ones * c_hi
        for idx, ob in near.values():
            for rb in range(tq // LANES):
                for cb in range(tk // LANES):
                    off = ob + cb - rb
                    if abs(off) <= 1:
                        blk = bsm_ref[0, off + 1]
                    else:
                        blk = jnp.ones((LANES, LANES), F32) * (c_lo if off < 0 else c_hi)
                    bias_scr[idx, rb * LANES:(rb + 1) * LANES, cb * LANES:(cb + 1) * LANES] = blk

    lane = lax.broadcasted_iota(jnp.int32, (tq, LANES), 1)
    zero = jnp.zeros((tq, LANES), BF16)
    ones_blk = jnp.ones((tk, LANES), BF16)

    first_near = {a: min(d for (a0, d) in near if a0 == a) for a in range(ratio)}
    n_e = max(sum(1 for (a0, _) in near if a0 == a) for a in range(ratio))

    def explicit_start(i):
        c = i // ratio
        a = i - c * ratio
        lo = c + first_near[0]
        for a0 in range(1, ratio):
            lo = jnp.where(a == a0, c + first_near[a0], lo)
        return jnp.clip(lo, 0, nk - n_e)

    def key_tile(i, pos):
        e0 = explicit_start(i)
        if pos < n_e:
            return e0 + pos, None
        k = pos - n_e
        j = k + jnp.where(k >= e0, n_e, 0)
        return j, jnp.where(j < e0, c_lo, c_hi)

    def bias_kind(i, j):
        c = i // ratio
        a = i - c * ratio
        d = j - c
        kind = jnp.where(d < 0, 0, 1)
        for (a0, d0), (idx, _) in near.items():
            kind = jnp.where((a == a0) & (d == d0), idx, kind)
        return kind

    def scores(i, pos):
        j, const = key_tile(i, pos)
        q = q_ref[pl.ds(pl.multiple_of(i * tq, tq), tq), :]
        qs = jnp.concatenate([jnp.where(lane < HEAD_DIM, q, zero), jnp.where(lane >= HEAD_DIM, q, zero)], axis=0)
        k = k_ref[pl.ds(pl.multiple_of(j * tk, tk), tk), :]
        s = lax.dot_general(qs, k, (((1,), (1,)), ((), ())), preferred_element_type=F32)
        if const is None:
            kind = bias_kind(i, j)
            s = jnp.concatenate([s[0:tq] + bias_scr[kind], s[tq:2 * tq] + bias_scr[kind]], axis=0)
            row_max = jnp.max(s, axis=1, keepdims=True)
        else:
            row_max = jnp.max(s, axis=1, keepdims=True) + const
        s_scr[pos % S_SLOTS] = s
        mcur_scr[pos % S_SLOTS] = jnp.broadcast_to(row_max, (2 * tq, LANES))

    def softmax(i, pos):
        _, const = key_tile(i, pos)
        slot, pslot = pos % S_SLOTS, pos % P_SLOTS
        if pos == 0:
            m_next = mcur_scr[slot]
        else:
            m_prev = m_scr[...]
            m_next = jnp.maximum(m_prev, mcur_scr[slot])
            alpha_scr[pslot] = jnp.exp2(m_prev - m_next)
        shift = m_next if const is None else m_next - const
        p_scr[pslot] = jnp.exp2(s_scr[slot] - jnp.concatenate([shift] * (tk // LANES), axis=1)).astype(BF16)
        m_scr[...] = m_next

    def values(i, pos):
        j, _ = key_tile(i, pos)
        pslot = pos % P_SLOTS
        vaug = jnp.concatenate([v_ref[pl.ds(pl.multiple_of(j * tk, tk), tk), :], ones_blk], axis=1)
        pv = jnp.dot(p_scr[pslot], vaug, preferred_element_type=F32)
        if pos == 0:
            acc_scr[...] = pv
        else:
            acc_scr[...] = acc_scr[...] * jnp.concatenate([alpha_scr[pslot]] * 2, axis=1) + pv

    def finalize(i):
        acc = acc_scr[...]
        o12 = acc[:, 0:LANES] / acc[:, LANES:2 * LANES]
        o = o12[0:tq] - lam_ref[0] * o12[tq:2 * tq]
        o_ref[pl.ds(pl.multiple_of(i * tq, tq), tq), :] = (_rms(o, subln_ref[...]) * out_scale).astype(BF16)

    nq = seq // tq
    assert nk % P_SLOTS == 0 and P_SLOTS > SOFTMAX_LEAD
    for pos in range(SOFTMAX_LEAD):
        scores(0, pos)
        softmax(0, pos)
    scores(0, SOFTMAX_LEAD)

    def body(g, carry):
        nxt = jnp.minimum(g + 1, nq - 1)
        for r in range(nk):
            values(g, r)
            ahead = r + SOFTMAX_LEAD
            softmax(g if ahead < nk else nxt, ahead % nk)
            scores(g if ahead + 1 < nk else nxt, (ahead + 1) % nk)
        finalize(g)
        return carry

    lax.fori_loop(0, nq, body, 0)


def _diff_attn(qk, proj, bias_small, rel_bias, lam, subln_w, batch, seq, tq, tk, out_scale):
    assert seq % tk == 0 and tk % tq == 0 and (seq // tk) % 2 == 0
    qb, kb, vb = 0, HEADS, COL_V // LANES
    kern = functools.partial(_attn_kernel, tq=tq, tk=tk, seq=seq, out_scale=out_scale)
    n_bias = 2 + len(_near_bias_tiles(tq, tk))
    return pl.pallas_call(
        kern,
        grid=(HEADS, batch),
        in_specs=[
            pl.BlockSpec(memory_space=pltpu.SMEM),
            pl.BlockSpec(memory_space=pltpu.SMEM),
            pl.BlockSpec((seq, LANES), lambda h, b: (b, qb + h)),
            pl.BlockSpec((seq, LANES), lambda h, b: (b, kb + h)),
            pl.BlockSpec((seq, LANES), lambda h, b: (b, vb + h)),
            pl.BlockSpec((1, 3, LANES, LANES), lambda h, b: (h, 0, 0, 0)),
            pl.BlockSpec((1, LANES), lambda h, b: (0, 0)),
        ],
        out_specs=pl.BlockSpec((seq, LANES), lambda h, b: (b, h)),
        out_shape=jax.ShapeDtypeStruct((batch * seq, HEADS * LANES), BF16),
        scratch_shapes=[
            pltpu.VMEM((S_SLOTS, 2 * tq, tk), F32),
            pltpu.VMEM((P_SLOTS, 2 * tq, tk), BF16),
            pltpu.VMEM((S_SLOTS, 2 * tq, LANES), F32),
            pltpu.VMEM((P_SLOTS, 2 * tq, LANES), F32),
            pltpu.VMEM((2 * tq, LANES), F32),
            pltpu.VMEM((2 * tq, 2 * LANES), F32),
            pltpu.VMEM((n_bias, tq, tk), F32),
        ],
        compiler_params=_cparams(2, VMEM_LIMIT_ATTN),
        name="diff_attn",
    )(lam, rel_bias, qk, qk, proj, bias_small, subln_w)


def _tri(upper):
    r = lax.broadcasted_iota(jnp.int32, (CHUNK, CHUNK), 0)
    c = lax.broadcasted_iota(jnp.int32, (CHUNK, CHUNK), 1)
    return (c >= r) if upper else (c <= r)


def _tri_dot(mask, x):
    m = jnp.where(mask, 1.0, 0.0).astype(BF16)
    hi, lo = _split_bf16(x)
    return jnp.dot(m, hi, preferred_element_type=F32) + jnp.dot(m, lo, preferred_element_type=F32)


PREP_CHUNKS = 16


def _ssd_prep_kernel(dt_ref, dtb_ref, alog_ref, zt_ref, cs_ref, dd_ref, ww_ref):
    nh = SSM_HEADS
    lane = lax.broadcasted_iota(jnp.int32, (CHUNK, LANES), 1)
    tril = _tri(False)
    triu = _tri(True)
    for k in range(PREP_CHUNKS):
        rows = slice(k * CHUNK, (k + 1) * CHUNK)
        raw = dt_ref[rows, :] + dtb_ref[...]
        dt = jnp.maximum(raw, 0.0) + jnp.log1p(jnp.exp(-jnp.abs(raw)))
        dt = jnp.where(lane < 2 * nh, dt, 0.0)
        da = dt * -jnp.exp(alog_ref[...])
        acs = _tri_dot(tril, da)
        rcs = _tri_dot(triu, da)
        cs = jnp.where(lane < nh, acs, rcs)
        end = jnp.where(lane < nh, acs[CHUNK - 1:CHUNK, :], rcs[0:1, :])
        cs_ref[rows, :] = cs
        dd_ref[rows, :] = jnp.exp(cs)
        ww_ref[rows, :] = jnp.exp(end - cs) * dt
        zt_ref[rows, :] = jnp.where(lane < 2 * nh, dt, pltpu.roll(cs, 2 * nh, 1)).T


def _ssd_prep(dt_raw, dt_bias, a_log):
    m = dt_raw.shape[0]
    rows = PREP_CHUNKS * CHUNK
    blk = pl.BlockSpec((rows, LANES), lambda i: (i, 0))
    vec = pl.BlockSpec((1, LANES), lambda i: (0, 0))
    return pl.pallas_call(
        _ssd_prep_kernel,
        grid=(m // rows,),
        in_specs=[blk, vec, vec],
        out_specs=[blk] * 4,
        out_shape=[jax.ShapeDtypeStruct((m, LANES), F32)] * 4,
        compiler_params=_cparams(1),
        name="ssd_prep",
    )(dt_raw, dt_bias, a_log)


def _expand_operand(x):
    hi, lo = _split_bf16(x)
    return jnp.concatenate([hi, lo], axis=1)


def _expand(x_hl, e2_ref, gs):
    return jnp.dot(x_hl, e2_ref[:, gs], preferred_element_type=F32)


def _shift_rows(u, halo, offset):
    n = u.shape[0]
    if offset == 0:
        return u
    rolled = pltpu.roll(u, (-offset) % n, 0)
    row8 = lax.broadcasted_iota(jnp.int32, (8, u.shape[1]), 0)
    if offset < 0:
        hfix = pltpu.roll(halo, (-offset) % 8, 0)
        first = jnp.where(row8 < -offset, hfix, rolled[0:8])
        return jnp.concatenate([first, rolled[8:n]], axis=0)
    hfix = pltpu.roll(halo, (8 - offset) % 8, 0)
    last = jnp.where(row8 >= 8 - offset, hfix, rolled[n - 8:n])
    return jnp.concatenate([rolled[0:n - 8], last], axis=0)


def _shift_select(n, shifts):
    t = jnp.arange(len(shifts) * n)
    src = t % n + HALO + jnp.repeat(jnp.asarray(shifts), n)
    return (src[:, None] == jnp.arange(n + 2 * HALO)[None, :]).astype(BF16)


SCAN_CHUNKS = 4


def _ssd_bwd_kernel(xc_ref, xp_ref, xn_ref, dd_ref, ww_ref, cw_ref, cb_ref, e2b_ref, sel_ref,
                    xact_ref, ybi_ref, state_scr, *, n_blocks):
    c = pl.program_id(1)
    cr = n_blocks - 1 - c

    @pl.when(c == 0)
    def _():
        state_scr[...] = jnp.zeros(state_scr.shape, F32)

    keep_prev = jnp.where(cr > 0, 1.0, 0.0).astype(BF16)
    keep_next = jnp.where(cr < n_blocks - 1, 1.0, 0.0).astype(BF16)
    slab = 512
    for sl in range(CONV_DIM // slab):
        cs = slice(sl * slab, (sl + 1) * slab)
        ext = jnp.concatenate([xp_ref[:, cs] * keep_prev, xc_ref[:, cs], xn_ref[:, cs] * keep_next], axis=0)
        for k in range(SCAN_CHUNKS):
            rows = slice(k * CHUNK, (k + 1) * CHUNK)
            sh = jnp.dot(sel_ref[...], ext[k * CHUNK:(k + 1) * CHUNK + 2 * HALO], preferred_element_type=F32)
            acc = cb_ref[:, cs] + xc_ref[rows, cs].astype(F32) * cw_ref[2:3, cs]
            for j, off in enumerate(SSM_SHIFTS):
                acc = acc + sh[j * CHUNK:(j + 1) * CHUNK] * cw_ref[off + 2:off + 3, cs]
            xact_ref[rows, cs] = _silu(acc).astype(BF16)

    for k in reversed(range(SCAN_CHUNKS)):
        rows = slice(k * CHUNK, (k + 1) * CHUNK)
        everything = slice(0, SSM_INNER)
        decay = _expand(_expand_operand(dd_ref[rows, :]), e2b_ref, everything)
        chunk_decay = decay[0:1, :]
        xs = xact_ref[rows, 0:SSM_INNER].astype(F32)
        xw = (xs * _expand(_expand_operand(ww_ref[rows, :]), e2b_ref, everything)).astype(BF16)
        for g in range(SSM_GROUPS):
            gs = slice(g * GROUP_W, (g + 1) * GROUP_W)
            bg = xact_ref[rows, SSM_INNER + g * SSM_STATE:SSM_INNER + (g + 1) * SSM_STATE]
            cg = xact_ref[rows, SSM_INNER + (SSM_GROUPS + g) * SSM_STATE:SSM_INNER + (SSM_GROUPS + g + 1) * SSM_STATE]
            st = state_scr[:, gs]
            yb = jnp.dot(cg, st.astype(BF16), preferred_element_type=F32) * decay[:, gs]
            ybi_ref[rows, gs] = yb.astype(BF16)
            upd = lax.dot_general(bg, xw[:, gs], (((0,), (0,)), ((), ())), preferred_element_type=F32)
            state_scr[:, gs] = st * chunk_decay[:, gs] + upd


def _ssd_bwd(proj, dd, ww, conv_w, conv_b, e2b, batch, seq):
    rows = SCAN_CHUNKS * CHUNK
    n_blocks = seq // rows
    sub = rows // HALO
    last16 = batch * seq // HALO - 1
    kern = functools.partial(_ssd_bwd_kernel, n_blocks=n_blocks)
    select = _shift_select(CHUNK, SSM_SHIFTS)

    def cur(b, c):
        return b * n_blocks + (n_blocks - 1 - c)

    return pl.pallas_call(
        kern,
        grid=(batch, n_blocks),
        in_specs=[
            pl.BlockSpec((rows, CONV_DIM), lambda b, c: (cur(b, c), 0)),
            pl.BlockSpec((HALO, CONV_DIM), lambda b, c: (jnp.maximum(cur(b, c) * sub - 1, 0), 0)),
            pl.BlockSpec((HALO, CONV_DIM), lambda b, c: (jnp.minimum((cur(b, c) + 1) * sub, last16), 0)),
            pl.BlockSpec((rows, LANES), lambda b, c: (cur(b, c), 0)),
            pl.BlockSpec((rows, LANES), lambda b, c: (cur(b, c), 0)),
            pl.BlockSpec((4, CONV_DIM), lambda b, c: (0, 0)),
            pl.BlockSpec((1, CONV_DIM), lambda b, c: (0, 0)),
            pl.BlockSpec((2 * LANES, SSM_INNER), lambda b, c: (0, 0)),
            pl.BlockSpec(select.shape, lambda b, c: (0, 0)),
        ],
        out_specs=[
            pl.BlockSpec((rows, CONV_DIM), lambda b, c: (cur(b, c), 0)),
            pl.BlockSpec((rows, SSM_INNER), lambda b, c: (cur(b, c), 0)),
        ],
        out_shape=[
            jax.ShapeDtypeStruct((batch * seq, CONV_DIM), BF16),
            jax.ShapeDtypeStruct((batch * seq, SSM_INNER), BF16),
        ],
        scratch_shapes=[pltpu.VMEM((SSM_STATE, SSM_INNER), F32)],
        compiler_params=_cparams(2),
        name="ssd_bwd",
    )(proj, proj, proj, dd, ww, conv_w, conv_b, e2b, select)


def _ssd_fwd_kernel(xact_ref, ybi_ref, zt_ref, cs_ref, dd_ref, ww_ref, dsk_ref, e2f_ref, out_ref, state_scr):
    @pl.when(pl.program_id(1) == 0)
    def _():
        state_scr[...] = jnp.zeros(state_scr.shape, F32)

    nh = SSM_HEADS
    lane_w = lax.broadcasted_iota(jnp.int32, (CHUNK, LANES), 1)
    row_i = lax.broadcasted_iota(jnp.int32, (CHUNK, CHUNK), 0)
    col_i = lax.broadcasted_iota(jnp.int32, (CHUNK, CHUNK), 1)
    lower = col_i < row_i
    upper = col_i > row_i

    for k in range(SCAN_CHUNKS):
        rows = slice(k * CHUNK, (k + 1) * CHUNK)
        zt = zt_ref.at[rows]
        cs = cs_ref[rows, :]
        dd_hl = _expand_operand(dd_ref[rows, :])
        ww_hl = _expand_operand(ww_ref[rows, :])
        for g in range(SSM_GROUPS):
            gs = slice(g * GROUP_W, (g + 1) * GROUP_W)
            bg = xact_ref[rows, SSM_INNER + g * SSM_STATE:SSM_INNER + (g + 1) * SSM_STATE]
            cg = xact_ref[rows, SSM_INNER + (SSM_GROUPS + g) * SSM_STATE:SSM_INNER + (SSM_GROUPS + g + 1) * SSM_STATE]
            xs_bf = xact_ref[rows, gs]
            xs = xs_bf.astype(F32)
            decay = _expand(dd_hl, e2f_ref, gs)
            xw = (xs * _expand(ww_hl, e2f_ref, gs)).astype(BF16)
            cb = lax.dot_general(cg, bg, (((1,), (1,)), ((), ())), preferred_element_type=F32)
            ys = []
            for r in range(GROUP_W // SSM_HEAD_DIM):
                h = g * (GROUP_W // SSM_HEAD_DIM) + r
                seg_f = cs[:, h:h + 1] - zt[2 * nh + h:2 * nh + h + 1, :]
                seg_b = cs[:, nh + h:nh + h + 1] - zt[3 * nh + h:3 * nh + h + 1, :]
                dt_f = zt[h:h + 1, :]
                dt_b = zt[nh + h:nh + h + 1, :]
                dt_sel = jnp.where(lower, dt_f, jnp.where(upper, dt_b, dt_f + dt_b))
                mat = (cb * (jnp.exp(jnp.where(upper, seg_b, seg_f)) * dt_sel)).astype(BF16)
                pair = xs_bf[:, (r // 2) * LANES:(r // 2 + 1) * LANES]
                ys.append(jnp.dot(mat, pair, preferred_element_type=F32))
            y = jnp.concatenate([jnp.where(lane_w < SSM_HEAD_DIM, ys[0], ys[1]),
                                 jnp.where(lane_w < SSM_HEAD_DIM, ys[2], ys[3])], axis=1)
            st = state_scr[:, gs]
            y = y + jnp.dot(cg, st.astype(BF16), preferred_element_type=F32) * decay
            upd = lax.dot_general(bg, xw, (((0,), (0,)), ((), ())), preferred_element_type=F32)
            state_scr[:, gs] = st * decay[CHUNK - 1:CHUNK, :] + upd
            y = y + ybi_ref[rows, gs].astype(F32) + dsk_ref[:, gs] * xs
            out_ref[rows, gs] = y.astype(BF16)


def _ssd_fwd(xact, ybi, zt, cs, dd, ww, d_skip, e2f, batch, seq):
    rows = SCAN_CHUNKS * CHUNK
    n_blocks = seq // rows
    small = pl.BlockSpec((rows, LANES), lambda b, c: (b * n_blocks + c, 0))
    return pl.pallas_call(
        _ssd_fwd_kernel,
        grid=(batch, n_blocks),
        in_specs=[
            pl.BlockSpec((rows, CONV_DIM), lambda b, c: (b * n_blocks + c, 0)),
            pl.BlockSpec((rows, SSM_INNER), lambda b, c: (b * n_blocks + c, 0)),
            small, small, small, small,
            pl.BlockSpec((1, SSM_INNER), lambda b, c: (0, 0)),
            pl.BlockSpec((2 * LANES, SSM_INNER), lambda b, c: (0, 0)),
        ],
        out_specs=pl.BlockSpec((rows, SSM_INNER), lambda b, c: (b * n_blocks + c, 0)),
        out_shape=jax.ShapeDtypeStruct((batch * seq, SSM_INNER), BF16),
        scratch_shapes=[pltpu.VMEM((SSM_STATE, SSM_INNER), F32)],
        compiler_params=_cparams(2),
        name="ssd_fwd",
    )(xact, ybi, zt, cs, dd, ww, d_skip, e2f)


def _mix_out_kernel(x_ref, ao_ref, so_ref, z_ref, nw_ref, ga_ref, gs_ref, wa_ref, ws_ref, wo_ref, n2_ref, x1_ref,
                    h2_ref):
    attn = jnp.dot(ao_ref[...], wa_ref[...], preferred_element_type=F32)
    ssd = None
    for g in range(SSM_GROUPS):
        gs = slice(g * GROUP_W, (g + 1) * GROUP_W)
        y = so_ref[:, gs].astype(F32) * _silu(z_ref[:, gs].astype(F32))
        part = jnp.dot(_rms(y, nw_ref[:, gs]).astype(BF16), ws_ref[gs, :], preferred_element_type=F32)
        ssd = part if ssd is None else ssd + part
    mixed = (jax.nn.sigmoid(ga_ref[...].astype(F32)) * attn
             + jax.nn.sigmoid(gs_ref[...].astype(F32)) * ssd)
    x1 = x_ref[...] + jnp.dot(mixed.astype(BF16), wo_ref[...], preferred_element_type=F32)
    x1_ref[...] = x1
    h2_ref[...] = _rms(x1, n2_ref[...]).astype(BF16)


def _mix_out(x2d, attn_o, ssd_o, proj, ssm_norm_w, w_attn, w_ssm, w_out, norm2_w, tm):
    m, d = x2d.shape
    gb = COL_GATE // d
    zb = COL_Z // SSM_INNER
    return pl.pallas_call(
        _mix_out_kernel,
        grid=(m // tm,),
        in_specs=[
            pl.BlockSpec((tm, d), lambda i: (i, 0)),
            pl.BlockSpec((tm, attn_o.shape[1]), lambda i: (i, 0)),
            pl.BlockSpec((tm, ssd_o.shape[1]), lambda i: (i, 0)),
            pl.BlockSpec((tm, SSM_INNER), lambda i: (i, zb)),
            pl.BlockSpec((1, SSM_INNER), lambda i: (0, 0)),
            pl.BlockSpec((tm, d), lambda i: (i, gb)),
            pl.BlockSpec((tm, d), lambda i: (i, gb + 1)),
            pl.BlockSpec(w_attn.shape, lambda i: (0, 0)),
            pl.BlockSpec(w_ssm.shape, lambda i: (0, 0)),
            pl.BlockSpec(w_out.shape, lambda i: (0, 0)),
            pl.BlockSpec((1, d), lambda i: (0, 0)),
        ],
        out_specs=[pl.BlockSpec((tm, d), lambda i: (i, 0)), pl.BlockSpec((tm, d), lambda i: (i, 0))],
        out_shape=[jax.ShapeDtypeStruct((m, d), F32), jax.ShapeDtypeStruct((m, d), BF16)],
        compiler_params=_cparams(1),
        name="mix_out",
    )(x2d, attn_o, ssd_o, proj, ssm_norm_w, proj, proj, w_attn, w_ssm, w_out, norm2_w)


FFN_CW = 256


def _ffn_kernel(x1_ref, h_ref, hp_ref, hn_ref, wu_ref, cw_ref, cb_ref, wd_ref, o_ref, act_scr, *, tiles_per_seq):
    i = pl.program_id(0)
    tm = h_ref.shape[0]
    f = wd_ref.shape[0]
    pos = i % tiles_per_seq
    keep_prev = jnp.where(pos > 0, 1.0, 0.0)
    keep_next = jnp.where(pos < tiles_per_seq - 1, 1.0, 0.0)
    h_ext = jnp.concatenate([hp_ref[...], h_ref[...], hn_ref[...]], axis=0)

    def conv(col):
        cs = slice(col, col + FFN_CW)
        u_ext = jnp.dot(h_ext, wu_ref[:, cs], preferred_element_type=F32)
        u = u_ext[HALO:HALO + tm]
        up = u_ext[HALO - 8:HALO] * keep_prev
        un = u_ext[HALO + tm:HALO + tm + 8] * keep_next
        return (cb_ref[:, cs] + _shift_rows(u, up, -1) * cw_ref[0:1, cs] + u * cw_ref[1:2, cs]
                + _shift_rows(u, un, 1) * cw_ref[2:3, cs])

    for c in range(f // FFN_CW):
        act = _silu(conv(c * FFN_CW)) * conv(f + c * FFN_CW)
        act_scr[:, c * FFN_CW:(c + 1) * FFN_CW] = act.astype(BF16)
    o_ref[...] = x1_ref[...] + jnp.dot(act_scr[...], wd_ref[...], preferred_element_type=F32)


def _ffn(x1, h2, w_up, conv_w, conv_b, w_down, seq, tm):
    m, d = x1.shape
    f = w_down.shape[0]
    sub = tm // HALO
    last = m // HALO - 1
    kern = functools.partial(_ffn_kernel, tiles_per_seq=seq // tm)
    resident = pl.Buffered(1)
    return pl.pallas_call(
        kern,
        grid=(m // tm,),
        in_specs=[
            pl.BlockSpec((tm, d), lambda i: (i, 0)),
            pl.BlockSpec((tm, d), lambda i: (i, 0)),
            pl.BlockSpec((HALO, d), lambda i: (jnp.maximum(i * sub - 1, 0), 0)),
            pl.BlockSpec((HALO, d), lambda i: (jnp.minimum((i + 1) * sub, last), 0)),
            pl.BlockSpec(w_up.shape, lambda i: (0, 0), pipeline_mode=resident),
            pl.BlockSpec(conv_w.shape, lambda i: (0, 0)),
            pl.BlockSpec(conv_b.shape, lambda i: (0, 0)),
            pl.BlockSpec(w_down.shape, lambda i: (0, 0), pipeline_mode=resident),
        ],
        out_specs=pl.BlockSpec((tm, d), lambda i: (i, 0)),
        out_shape=jax.ShapeDtypeStruct((m, d), F32),
        scratch_shapes=[pltpu.VMEM((tm, f), BF16)],
        compiler_params=_cparams(1),
        name="ffn",
    )(x1, h2, h2, h2, w_up, conv_w, conv_b, w_down)


def _head_expand(first_row):
    rows = jnp.arange(2 * LANES)[:, None] % LANES
    cols = jnp.arange(SSM_INNER)[None, :] // SSM_HEAD_DIM
    return (rows == cols + first_row).astype(BF16)


def _layer(x2d, batch, seq, lambda_init, p):
    d = x2d.shape[1]
    w_in = p["w_in"].astype(BF16)
    sizes = (HEADS * 2 * HEAD_DIM,) * 3 + (SSM_INNER, CONV_DIM, 2 * SSM_HEADS, 2 * d)
    offs = [0]
    for s in sizes:
        offs.append(offs[-1] + s)
    wq, wk, wv, wz, wxbc, wdt, wg = (w_in[:, offs[n]:offs[n + 1]] for n in range(7))
    w_main = jnp.concatenate([wxbc, wz, wg, wv], axis=1)
    w_qk = jnp.concatenate([wq, wk], axis=1)
    w_dt = jnp.pad(wdt, ((0, 0), (0, LANES - wdt.shape[1])))
    qkw = jnp.stack([jnp.tile(p["q_norm_w"], 2 * HEADS) * (HEAD_DIM ** -0.5 * LOG2E),
                     jnp.tile(p["k_norm_w"], 2 * HEADS)])[:, None, :]
    gi = jnp.arange(MXU_DIM) // HEAD_DIM
    gsum = (gi[:, None] == gi[None, :]).astype(BF16)

    proj, dt_raw, h1 = _in_proj(x2d, p["norm1_w"][None, :], w_main, w_dt, tm=2048)
    qk = _qk_proj(h1, w_qk, qkw, gsum, tm=1024)

    lam = (jnp.exp(jnp.sum(p["lambda_q1"] * p["lambda_k1"])) - jnp.exp(jnp.sum(p["lambda_q2"] * p["lambda_k2"]))
           + lambda_init).reshape(1).astype(F32)
    bias_small = _bias_tiles(p["rel_bias"])
    attn_o = _diff_attn(qk, proj, bias_small, p["rel_bias"], lam, p["subln_w"][None, :], batch, seq,
                        tq=512, tk=1024, out_scale=1.0 - lambda_init)

    pad = LANES - 2 * SSM_HEADS
    dt_bias = jnp.pad(jnp.concatenate([p["dt_bias_f"], p["dt_bias_b"]]), (0, pad))[None, :]
    a_log = jnp.pad(jnp.concatenate([p["a_log_f"], p["a_log_b"]]), (0, pad))[None, :]
    zt, cs, dd, ww = _ssd_prep(dt_raw, dt_bias, a_log)
    xact, ybi = _ssd_bwd(proj, dd, ww, p["ssm_conv_w"], p["ssm_conv_b"][None, :],
                         _head_expand(SSM_HEADS), batch, seq)
    ssd_y = _ssd_fwd(xact, ybi, zt, cs, dd, ww, jnp.repeat(p["d_skip"], SSM_HEAD_DIM)[None, :],
                     _head_expand(0), batch, seq)

    x1, h2 = _mix_out(x2d, attn_o, ssd_y, proj, p["ssm_norm_w"][None, :], p["w_attn_out"].astype(BF16),
                      p["w_ssm_out"].astype(BF16), p["w_out"].astype(BF16), p["norm2_w"][None, :], tm=512)

    return _ffn(x1, h2, p["w_ffn_up"].astype(BF16), p["ffn_conv_w"], p["ffn_conv_b"][None, :],
                p["w_ffn_down"].astype(BF16), seq, tm=1024)


def kernel(x, norm1_w, w_in, q_norm_w, k_norm_w, rel_bias, lambda_q1, lambda_k1, lambda_q2, lambda_k2, subln_w, w_attn_out, ssm_conv_w, ssm_conv_b, dt_bias_f, a_log_f, dt_bias_b, a_log_b, d_skip, ssm_norm_w, w_ssm_out, w_out, norm2_w, w_ffn_up, ffn_conv_w, ffn_conv_b, w_ffn_down):
    batch, seq, d = x.shape
    layered = dict(norm1_w=norm1_w, w_in=w_in, q_norm_w=q_norm_w, k_norm_w=k_norm_w, lambda_q1=lambda_q1,
                   lambda_k1=lambda_k1, lambda_q2=lambda_q2, lambda_k2=lambda_k2, subln_w=subln_w,
                   w_attn_out=w_attn_out, ssm_conv_w=ssm_conv_w, ssm_conv_b=ssm_conv_b, dt_bias_f=dt_bias_f,
                   a_log_f=a_log_f, dt_bias_b=dt_bias_b, a_log_b=a_log_b, d_skip=d_skip, ssm_norm_w=ssm_norm_w,
                   w_ssm_out=w_ssm_out, w_out=w_out, norm2_w=norm2_w, w_ffn_up=w_ffn_up, ffn_conv_w=ffn_conv_w,
                   ffn_conv_b=ffn_conv_b, w_ffn_down=w_ffn_down)
    x2d = x.reshape(batch * seq, d)
    for layer in range(norm1_w.shape[0]):
        p = {k: v[layer] for k, v in layered.items()}
        p["rel_bias"] = rel_bias
        lambda_init = 0.8 - 0.6 * math.exp(-0.3 * layer)
        x2d = _layer(x2d, batch, seq, lambda_init, p)
    return x2d.reshape(batch, seq, d)
```

```python
import functools
import math

import jax
import jax.numpy as jnp
from jax import lax
from jax.experimental import pallas as pl
from jax.experimental.pallas import tpu as pltpu

F32 = jnp.float32
BF16 = jnp.bfloat16

HEADS = 8
HEAD_DIM = 64
SSM_HEADS = 32
SSM_HEAD_DIM = 64
SSM_GROUPS = 8
SSM_STATE = 128
SSM_INNER = SSM_HEADS * SSM_HEAD_DIM
GROUP_W = SSM_INNER // SSM_GROUPS
CONV_DIM = SSM_INNER + 2 * SSM_GROUPS * SSM_STATE
CHUNK = 128
SSM_SHIFTS = (-2, -1, 1)
HALO = 16
RMS_EPS = 1e-6
LOG2E = 1.4426950408889634

REL_THRESHOLDS = (12, 16, 23, 32, 46, 64, 91)
REL_FAR = 129

LANES = 128
VMEM_LIMIT = 52 * 1024 * 1024
VMEM_LIMIT_ATTN = 60 * 1024 * 1024

MXU_DIM = 256

COL_XBC = 0
COL_Z = 4096
COL_GATE = 6144
COL_V = 8192
PROJ_W = 9216
PROJ_TILE = 1024


def _cparams(n_axes, vmem_limit=VMEM_LIMIT):
    return pltpu.CompilerParams(dimension_semantics=("arbitrary",) * n_axes, vmem_limit_bytes=vmem_limit)


def _rms(x, w):
    return x * lax.rsqrt(jnp.mean(x * x, axis=-1, keepdims=True) + RMS_EPS) * w


def _split_bf16(x):
    hi = x.astype(BF16)
    lo = (x - hi.astype(F32)).astype(BF16)
    return hi, lo


def _silu(x):
    return x / (1.0 + jnp.exp(-x))


def _in_proj_kernel(x_ref, nw_ref, w_ref, wdt_ref, proj_ref, dt_ref, h_ref):
    @pl.when(pl.program_id(1) == 0)
    def _():
        h = _rms(x_ref[...], nw_ref[...]).astype(BF16)
        h_ref[...] = h
        dt_ref[...] = jnp.dot(h, wdt_ref[...], preferred_element_type=F32)

    proj_ref[...] = jnp.dot(h_ref[...], w_ref[...], preferred_element_type=F32).astype(BF16)


def _in_proj(x2d, norm_w, w_main, w_dt, tm):
    m, d = x2d.shape
    return pl.pallas_call(
        _in_proj_kernel,
        grid=(m // tm, PROJ_W // PROJ_TILE),
        in_specs=[
            pl.BlockSpec((tm, d), lambda i, j: (i, 0)),
            pl.BlockSpec((1, d), lambda i, j: (0, 0)),
            pl.BlockSpec((d, PROJ_TILE), lambda i, j: (0, j)),
            pl.BlockSpec((d, LANES), lambda i, j: (0, 0)),
        ],
        out_specs=[
            pl.BlockSpec((tm, PROJ_TILE), lambda i, j: (i, j)),
            pl.BlockSpec((tm, LANES), lambda i, j: (i, 0)),
            pl.BlockSpec((tm, d), lambda i, j: (i, 0)),
        ],
        out_shape=[
            jax.ShapeDtypeStruct((m, PROJ_W), BF16),
            jax.ShapeDtypeStruct((m, LANES), F32),
            jax.ShapeDtypeStruct((m, d), BF16),
        ],
        compiler_params=_cparams(2),
        name="in_proj",
    )(x2d, norm_w, w_main, w_dt)


def _qk_proj_kernel(h_ref, w_ref, qkw_ref, gsum_ref, o_ref):
    acc = jnp.dot(h_ref[...], w_ref[...], preferred_element_type=F32)
    for c in range(PROJ_TILE // MXU_DIM):
        cs = slice(c * MXU_DIM, (c + 1) * MXU_DIM)
        a = acc[:, cs]
        ss = jnp.dot((a * a).astype(BF16), gsum_ref[...], preferred_element_type=F32)
        o_ref[:, cs] = (a * lax.rsqrt(ss * (1.0 / HEAD_DIM) + RMS_EPS) * qkw_ref[0, :, cs]).astype(BF16)


def _qk_proj(h, w_qk, qkw, gsum, tm):
    m, d = h.shape
    return pl.pallas_call(
        _qk_proj_kernel,
        grid=(m // tm, 2),
        in_specs=[
            pl.BlockSpec((tm, d), lambda i, j: (i, 0)),
            pl.BlockSpec((d, PROJ_TILE), lambda i, j: (0, j)),
            pl.BlockSpec((1, 1, PROJ_TILE), lambda i, j: (j, 0, 0)),
            pl.BlockSpec((MXU_DIM, MXU_DIM), lambda i, j: (0, 0)),
        ],
        out_specs=pl.BlockSpec((tm, PROJ_TILE), lambda i, j: (i, j)),
        out_shape=jax.ShapeDtypeStruct((m, 2 * PROJ_TILE), BF16),
        compiler_params=_cparams(2),
        name="qk_proj",
    )(h, w_qk, qkw, gsum)


def _bias_tiles_kernel(table_ref, out_ref):
    h = pl.program_id(0)
    r = lax.broadcasted_iota(jnp.int32, (LANES, LANES), 0)
    c = lax.broadcasted_iota(jnp.int32, (LANES, LANES), 1)
    for d in range(3):
        rel = (d - 1) * LANES + c - r
        n = jnp.abs(rel)
        large = jnp.full((LANES, LANES), 8, jnp.int32)
        for t in REL_THRESHOLDS:
            large = large + jnp.where(n >= t, 1, 0)
        bucket = jnp.where(rel > 0, 16, 0) + jnp.where(n < 8, n, large)
        val = jnp.zeros((LANES, LANES), F32)
        for bkt in range(32):
            val = jnp.where(bucket == bkt, table_ref[bkt, h], val)
        out_ref[0, d] = val * LOG2E


def _bias_tiles(rel_bias):
    return pl.pallas_call(
        _bias_tiles_kernel,
        grid=(HEADS,),
        in_specs=[pl.BlockSpec(memory_space=pltpu.SMEM)],
        out_specs=pl.BlockSpec((1, 3, LANES, LANES), lambda h: (h, 0, 0, 0)),
        out_shape=jax.ShapeDtypeStruct((HEADS, 3, LANES, LANES), F32),
        compiler_params=_cparams(1),
        name="bias_tiles",
    )(rel_bias)


def _near_bias_tiles(tq, tk):
    nbq, nbk = tq // LANES, tk // LANES
    table = {}
    for a in range(tk // tq):
        for d in (-1, 0, 1):
            ob = d * nbk - a * nbq
            if ob + nbk - 1 >= -1 and ob - (nbq - 1) <= 1:
                table[(a, d)] = (2 + len(table), ob)
    return table


SOFTMAX_LEAD = 1
S_SLOTS = 2
P_SLOTS = 2


def _attn_kernel(lam_ref, table_ref, q_ref, k_ref, v_ref, bsm_ref, subln_ref, o_ref,
                 s_scr, p_scr, mcur_scr, alpha_scr, m_scr, acc_scr, bias_scr, *, tq, tk, seq, out_scale):
    nk = seq // tk
    ratio = tk // tq
    near = _near_bias_tiles(tq, tk)
    h = pl.program_id(0)
    b = pl.program_id(1)
    c_lo = table_ref[15, h] * LOG2E
    c_hi = table_ref[31, h] * LOG2E

    @pl.when(b == 0)
    def _():
        ones = jnp.ones((tq, tk), F32)
        bias_scr[0] = ones * c_lo
        bias_scr[1] = ones * c_hi
        for idx, ob in near.values():
            for rb in range(tq // LANES):
                for cb in range(tk // LANES):
                    off = ob + cb - rb
                    if abs(off) <= 1:
                        blk = bsm_ref[0, off + 1]
                    else:
                        blk = jnp.ones((LANES, LANES), F32) * (c_lo if off < 0 else c_hi)
                    bias_scr[idx, rb * LANES:(rb + 1) * LANES, cb * LANES:(cb + 1) * LANES] = blk

    lane = lax.broadcasted_iota(jnp.int32, (tq, LANES), 1)
    zero = jnp.zeros((tq, LANES), BF16)
    ones_blk = jnp.ones((tk, LANES), BF16)

    first_near = {a: min(d for (a0, d) in near if a0 == a) for a in range(ratio)}
    n_e = max(sum(1 for (a0, _) in near if a0 == a) for a in range(ratio))

    def explicit_start(i):
        c = i // ratio
        a = i - c * ratio
        lo = c + first_near[0]
        for a0 in range(1, ratio):
            lo = jnp.where(a == a0, c + first_near[a0], lo)
        return jnp.clip(lo, 0, nk - n_e)

    def key_tile(i, pos):
        e0 = explicit_start(i)
        if pos < n_e:
            return e0 + pos, None
        k = pos - n_e
        j = k + jnp.where(k >= e0, n_e, 0)
        return j, jnp.where(j < e0, c_lo, c_hi)

    def bias_kind(i, j):
        c = i // ratio
        a = i - c * ratio
        d = j - c
        kind = jnp.where(d < 0, 0, 1)
        for (a0, d0), (idx, _) in near.items():
            kind = jnp.where((a == a0) & (d == d0), idx, kind)
        return kind

    def scores(i, pos):
        j, const = key_tile(i, pos)
        q = q_ref[pl.ds(pl.multiple_of(i * tq, tq), tq), :]
        qs = jnp.concatenate([jnp.where(lane < HEAD_DIM, q, zero), jnp.where(lane >= HEAD_DIM, q, zero)], axis=0)
        k = k_ref[pl.ds(pl.multiple_of(j * tk, tk), tk), :]
        s = lax.dot_general(qs, k, (((1,), (1,)), ((), ())), preferred_element_type=F32)
        if const is None:
            kind = bias_kind(i, j)
            s = jnp.concatenate([s[0:tq] + bias_scr[kind], s[tq:2 * tq] + bias_scr[kind]], axis=0)
            row_max = jnp.max(s, axis=1, keepdims=True)
        else:
            row_max = jnp.max(s, axis=1, keepdims=True) + const
        s_scr[pos % S_SLOTS] = s
        mcur_scr[pos % S_SLOTS] = jnp.broadcast_to(row_max, (2 * tq, LANES))

    def softmax(i, pos):
        _, const = key_tile(i, pos)
        slot, pslot = pos % S_SLOTS, pos % P_SLOTS
        if pos == 0:
            m_next = mcur_scr[slot]
        else:
            m_prev = m_scr[...]
            m_next = jnp.maximum(m_prev, mcur_scr[slot])
            alpha_scr[pslot] = jnp.exp2(m_prev - m_next)
        shift = m_next if const is None else m_next - const
        p_scr[pslot] = jnp.exp2(s_scr[slot] - jnp.concatenate([shift] * (tk // LANES), axis=1)).astype(BF16)
        m_scr[...] = m_next

    def values(i, pos):
        j, _ = key_tile(i, pos)
        pslot = pos % P_SLOTS
        vaug = jnp.concatenate([v_ref[pl.ds(pl.multiple_of(j * tk, tk), tk), :], ones_blk], axis=1)
        pv = jnp.dot(p_scr[pslot], vaug, preferred_element_type=F32)
        if pos == 0:
            acc_scr[...] = pv
        else:
            acc_scr[...] = acc_scr[...] * jnp.concatenate([alpha_scr[pslot]] * 2, axis=1) + pv

    def finalize(i):
        acc = acc_scr[...]
        o12 = acc[:, 0:LANES] / acc[:, LANES:2 * LANES]
        o = o12[0:tq] - lam_ref[0] * o12[tq:2 * tq]
        o_ref[pl.ds(pl.multiple_of(i * tq, tq), tq), :] = (_rms(o, subln_ref[...]) * out_scale).astype(BF16)

    nq = seq // tq
    assert nk % P_SLOTS == 0 and P_SLOTS > SOFTMAX_LEAD
    for pos in range(SOFTMAX_LEAD):
        scores(0, pos)
        softmax(0, pos)
    scores(0, SOFTMAX_LEAD)

    def body(g, carry):
        nxt = jnp.minimum(g + 1, nq - 1)
        for r in range(nk):
            values(g, r)
            ahead = r + SOFTMAX_LEAD
            softmax(g if ahead < nk else nxt, ahead % nk)
            scores(g if ahead + 1 < nk else nxt, (ahead + 1) % nk)
        finalize(g)
        return carry

    lax.fori_loop(0, nq, body, 0)


def _diff_attn(qk, proj, bias_small, rel_bias, lam, subln_w, batch, seq, tq, tk, out_scale):
    assert seq % tk == 0 and tk % tq == 0 and (seq // tk) % 2 == 0
    qb, kb, vb = 0, HEADS, COL_V // LANES
    kern = functools.partial(_attn_kernel, tq=tq, tk=tk, seq=seq, out_scale=out_scale)
    n_bias = 2 + len(_near_bias_tiles(tq, tk))
    return pl.pallas_call(
        kern,
        grid=(HEADS, batch),
        in_specs=[
            pl.BlockSpec(memory_space=pltpu.SMEM),
            pl.BlockSpec(memory_space=pltpu.SMEM),
            pl.BlockSpec((seq, LANES), lambda h, b: (b, qb + h)),
            pl.BlockSpec((seq, LANES), lambda h, b: (b, kb + h)),
            pl.BlockSpec((seq, LANES), lambda h, b: (b, vb + h)),
            pl.BlockSpec((1, 3, LANES, LANES), lambda h, b: (h, 0, 0, 0)),
            pl.BlockSpec((1, LANES), lambda h, b: (0, 0)),
        ],
        out_specs=pl.BlockSpec((seq, LANES), lambda h, b: (b, h)),
        out_shape=jax.ShapeDtypeStruct((batch * seq, HEADS * LANES), BF16),
        scratch_shapes=[
            pltpu.VMEM((S_SLOTS, 2 * tq, tk), F32),
            pltpu.VMEM((P_SLOTS, 2 * tq, tk), BF16),
            pltpu.VMEM((S_SLOTS, 2 * tq, LANES), F32),
            pltpu.VMEM((P_SLOTS, 2 * tq, LANES), F32),
            pltpu.VMEM((2 * tq, LANES), F32),
            pltpu.VMEM((2 * tq, 2 * LANES), F32),
            pltpu.VMEM((n_bias, tq, tk), F32),
        ],
        compiler_params=_cparams(2, VMEM_LIMIT_ATTN),
        name="diff_attn",
    )(lam, rel_bias, qk, qk, proj, bias_small, subln_w)


def _tri(upper):
    r = lax.broadcasted_iota(jnp.int32, (CHUNK, CHUNK), 0)
    c = lax.broadcasted_iota(jnp.int32, (CHUNK, CHUNK), 1)
    return (c >= r) if upper else (c <= r)


def _tri_dot(mask, x):
    m = jnp.where(mask, 1.0, 0.0).astype(BF16)
    hi, lo = _split_bf16(x)
    return jnp.dot(m, hi, preferred_element_type=F32) + jnp.dot(m, lo, preferred_element_type=F32)


PREP_CHUNKS = 16


def _ssd_prep_kernel(dt_ref, dtb_ref, alog_ref, zt_ref, cs_ref, dd_ref, ww_ref):
    nh = SSM_HEADS
    lane = lax.broadcasted_iota(jnp.int32, (CHUNK, LANES), 1)
    tril = _tri(False)
    triu = _tri(True)
    for k in range(PREP_CHUNKS):
        rows = slice(k * CHUNK, (k + 1) * CHUNK)
        raw = dt_ref[rows, :] + dtb_ref[...]
        dt = jnp.maximum(raw, 0.0) + jnp.log1p(jnp.exp(-jnp.abs(raw)))
        dt = jnp.where(lane < 2 * nh, dt, 0.0)
        da = dt * -jnp.exp(alog_ref[...])
        acs = _tri_dot(tril, da)
        rcs = _tri_dot(triu, da)
        cs = jnp.where(lane < nh, acs, rcs)
        end = jnp.where(lane < nh, acs[CHUNK - 1:CHUNK, :], rcs[0:1, :])
        cs_ref[rows, :] = cs
        dd_ref[rows, :] = jnp.exp(cs)
        ww_ref[rows, :] = jnp.exp(end - cs) * dt
        zt_ref[rows, :] = jnp.where(lane < 2 * nh, dt, pltpu.roll(cs, 2 * nh, 1)).T


def _ssd_prep(dt_raw, dt_bias, a_log):
    m = dt_raw.shape[0]
    rows = PREP_CHUNKS * CHUNK
    blk = pl.BlockSpec((rows, LANES), lambda i: (i, 0))
    vec = pl.BlockSpec((1, LANES), lambda i: (0, 0))
    return pl.pallas_call(
        _ssd_prep_kernel,
        grid=(m // rows,),
        in_specs=[blk, vec, vec],
        out_specs=[blk] * 4,
        out_shape=[jax.ShapeDtypeStruct((m, LANES), F32)] * 4,
        compiler_params=_cparams(1),
        name="ssd_prep",
    )(dt_raw, dt_bias, a_log)


def _expand_operand(x):
    hi, lo = _split_bf16(x)
    return jnp.concatenate([hi, lo], axis=1)


def _expand(x_hl, e2_ref, gs):
    return jnp.dot(x_hl, e2_ref[:, gs], preferred_element_type=F32)


def _shift_rows(u, halo, offset):
    n = u.shape[0]
    if offset == 0:
        return u
    rolled = pltpu.roll(u, (-offset) % n, 0)
    row8 = lax.broadcasted_iota(jnp.int32, (8, u.shape[1]), 0)
    if offset < 0:
        hfix = pltpu.roll(halo, (-offset) % 8, 0)
        first = jnp.where(row8 < -offset, hfix, rolled[0:8])
        return jnp.concatenate([first, rolled[8:n]], axis=0)
    hfix = pltpu.roll(halo, (8 - offset) % 8, 0)
    last = jnp.where(row8 >= 8 - offset, hfix, rolled[n - 8:n])
    return jnp.concatenate([rolled[0:n - 8], last], axis=0)


def _shift_select(n, shifts):
    t = jnp.arange(len(shifts) * n)
    src = t % n + HALO + jnp.repeat(jnp.asarray(shifts), n)
    return (src[:, None] == jnp.arange(n + 2 * HALO)[None, :]).astype(BF16)


SCAN_CHUNKS = 8


def _ssd_bwd_kernel(xc_ref, xp_ref, xn_ref, dd_ref, ww_ref, cw_ref, cb_ref, e2b_ref, sel_ref,
                    xact_ref, ybi_ref, state_scr, *, n_blocks):
    c = pl.program_id(1)
    cr = n_blocks - 1 - c

    @pl.when(c == 0)
    def _():
        state_scr[...] = jnp.zeros(state_scr.shape, F32)

    keep_prev = jnp.where(cr > 0, 1.0, 0.0).astype(BF16)
    keep_next = jnp.where(cr < n_blocks - 1, 1.0, 0.0).astype(BF16)
    slab = 512
    for sl in range(CONV_DIM // slab):
        cs = slice(sl * slab, (sl + 1) * slab)
        ext = jnp.concatenate([xp_ref[:, cs] * keep_prev, xc_ref[:, cs], xn_ref[:, cs] * keep_next], axis=0)
        for k in range(SCAN_CHUNKS):
            rows = slice(k * CHUNK, (k + 1) * CHUNK)
            sh = jnp.dot(sel_ref[...], ext[k * CHUNK:(k + 1) * CHUNK + 2 * HALO], preferred_element_type=F32)
            acc = cb_ref[:, cs] + xc_ref[rows, cs].astype(F32) * cw_ref[2:3, cs]
            for j, off in enumerate(SSM_SHIFTS):
                acc = acc + sh[j * CHUNK:(j + 1) * CHUNK] * cw_ref[off + 2:off + 3, cs]
            xact_ref[rows, cs] = _silu(acc).astype(BF16)

    for k in reversed(range(SCAN_CHUNKS)):
        rows = slice(k * CHUNK, (k + 1) * CHUNK)
        everything = slice(0, SSM_INNER)
        decay = _expand(_expand_operand(dd_ref[rows, :]), e2b_ref, everything)
        chunk_decay = decay[0:1, :]
        xs = xact_ref[rows, 0:SSM_INNER].astype(F32)
        xw = (xs * _expand(_expand_operand(ww_ref[rows, :]), e2b_ref, everything)).astype(BF16)
        for g in range(SSM_GROUPS):
            gs = slice(g * GROUP_W, (g + 1) * GROUP_W)
            bg = xact_ref[rows, SSM_INNER + g * SSM_STATE:SSM_INNER + (g + 1) * SSM_STATE]
            cg = xact_ref[rows, SSM_INNER + (SSM_GROUPS + g) * SSM_STATE:SSM_INNER + (SSM_GROUPS + g + 1) * SSM_STATE]
            st = state_scr[:, gs]
            yb = jnp.dot(cg, st.astype(BF16), preferred_element_type=F32) * decay[:, gs]
            ybi_ref[rows, gs] = yb.astype(BF16)
            upd = lax.dot_general(bg, xw[:, gs], (((0,), (0,)), ((), ())), preferred_element_type=F32)
            state_scr[:, gs] = st * chunk_decay[:, gs] + upd


def _ssd_bwd(proj, dd, ww, conv_w, conv_b, e2b, batch, seq):
    rows = SCAN_CHUNKS * CHUNK
    n_blocks = seq // rows
    sub = rows // HALO
    last16 = batch * seq // HALO - 1
    kern = functools.partial(_ssd_bwd_kernel, n_blocks=n_blocks)
    select = _shift_select(CHUNK, SSM_SHIFTS)

    def cur(b, c):
        return b * n_blocks + (n_blocks - 1 - c)

    return pl.pallas_call(
        kern,
        grid=(batch, n_blocks),
        in_specs=[
            pl.BlockSpec((rows, CONV_DIM), lambda b, c: (cur(b, c), 0)),
            pl.BlockSpec((HALO, CONV_DIM), lambda b, c: (jnp.maximum(cur(b, c) * sub - 1, 0), 0)),
            pl.BlockSpec((HALO, CONV_DIM), lambda b, c: (jnp.minimum((cur(b, c) + 1) * sub, last16), 0)),
            pl.BlockSpec((rows, LANES), lambda b, c: (cur(b, c), 0)),
            pl.BlockSpec((rows, LANES), lambda b, c: (cur(b, c), 0)),
            pl.BlockSpec((4, CONV_DIM), lambda b, c: (0, 0)),
            pl.BlockSpec((1, CONV_DIM), lambda b, c: (0, 0)),
            pl.BlockSpec((2 * LANES, SSM_INNER), lambda b, c: (0, 0)),
            pl.BlockSpec(select.shape, lambda b, c: (0, 0)),
        ],
        out_specs=[
            pl.BlockSpec((rows, CONV_DIM), lambda b, c: (cur(b, c), 0)),
            pl.BlockSpec((rows, SSM_INNER), lambda b, c: (cur(b, c), 0)),
        ],
        out_shape=[
            jax.ShapeDtypeStruct((batch * seq, CONV_DIM), BF16),
            jax.ShapeDtypeStruct((batch * seq, SSM_INNER), BF16),
        ],
        scratch_shapes=[pltpu.VMEM((SSM_STATE, SSM_INNER), F32)],
        compiler_params=_cparams(2),
        name="ssd_bwd",
    )(proj, proj, proj, dd, ww, conv_w, conv_b, e2b, select)


def _ssd_fwd_kernel(xact_ref, ybi_ref, zt_ref, cs_ref, dd_ref, ww_ref, dsk_ref, e2f_ref, out_ref, state_scr):
    @pl.when(pl.program_id(1) == 0)
    def _():
        state_scr[...] = jnp.zeros(state_scr.shape, F32)

    nh = SSM_HEADS
    lane_w = lax.broadcasted_iota(jnp.int32, (CHUNK, LANES), 1)
    row_i = lax.broadcasted_iota(jnp.int32, (CHUNK, CHUNK), 0)
    col_i = lax.broadcasted_iota(jnp.int32, (CHUNK, CHUNK), 1)
    lower = col_i < row_i
    upper = col_i > row_i

    for k in range(SCAN_CHUNKS):
        rows = slice(k * CHUNK, (k + 1) * CHUNK)
        zt = zt_ref.at[rows]
        cs = cs_ref[rows, :]
        dd_hl = _expand_operand(dd_ref[rows, :])
        ww_hl = _expand_operand(ww_ref[rows, :])
        for g in range(SSM_GROUPS):
            gs = slice(g * GROUP_W, (g + 1) * GROUP_W)
            bg = xact_ref[rows, SSM_INNER + g * SSM_STATE:SSM_INNER + (g + 1) * SSM_STATE]
            cg = xact_ref[rows, SSM_INNER + (SSM_GROUPS + g) * SSM_STATE:SSM_INNER + (SSM_GROUPS + g + 1) * SSM_STATE]
            xs_bf = xact_ref[rows, gs]
            xs = xs_bf.astype(F32)
            decay = _expand(dd_hl, e2f_ref, gs)
            xw = (xs * _expand(ww_hl, e2f_ref, gs)).astype(BF16)
            cb = lax.dot_general(cg, bg, (((1,), (1,)), ((), ())), preferred_element_type=F32)
            ys = []
            for r in range(GROUP_W // SSM_HEAD_DIM):
                h = g * (GROUP_W // SSM_HEAD_DIM) + r
                seg_f = cs[:, h:h + 1] - zt[2 * nh + h:2 * nh + h + 1, :]
                seg_b = cs[:, nh + h:nh + h + 1] - zt[3 * nh + h:3 * nh + h + 1, :]
                dt_f = zt[h:h + 1, :]
                dt_b = zt[nh + h:nh + h + 1, :]
                dt_sel = jnp.where(lower, dt_f, jnp.where(upper, dt_b, dt_f + dt_b))
                mat = (cb * (jnp.exp(jnp.where(upper, seg_b, seg_f)) * dt_sel)).astype(BF16)
                pair = xs_bf[:, (r // 2) * LANES:(r // 2 + 1) * LANES]
                ys.append(jnp.dot(mat, pair, preferred_element_type=F32))
            y = jnp.concatenate([jnp.where(lane_w < SSM_HEAD_DIM, ys[0], ys[1]),
                                 jnp.where(lane_w < SSM_HEAD_DIM, ys[2], ys[3])], axis=1)
            st = state_scr[:, gs]
            y = y + jnp.dot(cg, st.astype(BF16), preferred_element_type=F32) * decay
            upd = lax.dot_general(bg, xw, (((0,), (0,)), ((), ())), preferred_element_type=F32)
            state_scr[:, gs] = st * decay[CHUNK - 1:CHUNK, :] + upd
            y = y + ybi_ref[rows, gs].astype(F32) + dsk_ref[:, gs] * xs
            out_ref[rows, gs] = y.astype(BF16)


def _ssd_fwd(xact, ybi, zt, cs, dd, ww, d_skip, e2f, batch, seq):
    rows = SCAN_CHUNKS * CHUNK
    n_blocks = seq // rows
    small = pl.BlockSpec((rows, LANES), lambda b, c: (b * n_blocks + c, 0))
    return pl.pallas_call(
        _ssd_fwd_kernel,
        grid=(batch, n_blocks),
        in_specs=[
            pl.BlockSpec((rows, CONV_DIM), lambda b, c: (b * n_blocks + c, 0)),
            pl.BlockSpec((rows, SSM_INNER), lambda b, c: (b * n_blocks + c, 0)),
            small, small, small, small,
            pl.BlockSpec((1, SSM_INNER), lambda b, c: (0, 0)),
            pl.BlockSpec((2 * LANES, SSM_INNER), lambda b, c: (0, 0)),
        ],
        out_specs=pl.BlockSpec((rows, SSM_INNER), lambda b, c: (b * n_blocks + c, 0)),
        out_shape=jax.ShapeDtypeStruct((batch * seq, SSM_INNER), BF16),
        scratch_shapes=[pltpu.VMEM((SSM_STATE, SSM_INNER), F32)],
        compiler_params=_cparams(2),
        name="ssd_fwd",
    )(xact, ybi, zt, cs, dd, ww, d_skip, e2f)


def _mix_out_kernel(x_ref, ao_ref, so_ref, z_ref, nw_ref, ga_ref, gs_ref, wa_ref, ws_ref, wo_ref, n2_ref, x1_ref,
                    h2_ref):
    attn = jnp.dot(ao_ref[...], wa_ref[...], preferred_element_type=F32)
    ssd = None
    for g in range(SSM_GROUPS):
        gs = slice(g * GROUP_W, (g + 1) * GROUP_W)
        y = so_ref[:, gs].astype(F32) * _silu(z_ref[:, gs].astype(F32))
        part = jnp.dot(_rms(y, nw_ref[:, gs]).astype(BF16), ws_ref[gs, :], preferred_element_type=F32)
        ssd = part if ssd is None else ssd + part
    mixed = (jax.nn.sigmoid(ga_ref[...].astype(F32)) * attn
             + jax.nn.sigmoid(gs_ref[...].astype(F32)) * ssd)
    x1 = x_ref[...] + jnp.dot(mixed.astype(BF16), wo_ref[...], preferred_element_type=F32)
    x1_ref[...] = x1
    h2_ref[...] = _rms(x1, n2_ref[...]).astype(BF16)


def _mix_out(x2d, attn_o, ssd_o, proj, ssm_norm_w, w_attn, w_ssm, w_out, norm2_w, tm):
    m, d = x2d.shape
    gb = COL_GATE // d
    zb = COL_Z // SSM_INNER
    return pl.pallas_call(
        _mix_out_kernel,
        grid=(m // tm,),
        in_specs=[
            pl.BlockSpec((tm, d), lambda i: (i, 0)),
            pl.BlockSpec((tm, attn_o.shape[1]), lambda i: (i, 0)),
            pl.BlockSpec((tm, ssd_o.shape[1]), lambda i: (i, 0)),
            pl.BlockSpec((tm, SSM_INNER), lambda i: (i, zb)),
            pl.BlockSpec((1, SSM_INNER), lambda i: (0, 0)),
            pl.BlockSpec((tm, d), lambda i: (i, gb)),
            pl.BlockSpec((tm, d), lambda i: (i, gb + 1)),
            pl.BlockSpec(w_attn.shape, lambda i: (0, 0)),
            pl.BlockSpec(w_ssm.shape, lambda i: (0, 0)),
            pl.BlockSpec(w_out.shape, lambda i: (0, 0)),
            pl.BlockSpec((1, d), lambda i: (0, 0)),
        ],
        out_specs=[pl.BlockSpec((tm, d), lambda i: (i, 0)), pl.BlockSpec((tm, d), lambda i: (i, 0))],
        out_shape=[jax.ShapeDtypeStruct((m, d), F32), jax.ShapeDtypeStruct((m, d), BF16)],
        compiler_params=_cparams(1),
        name="mix_out",
    )(x2d, attn_o, ssd_o, proj, ssm_norm_w, proj, proj, w_attn, w_ssm, w_out, norm2_w)


FFN_CW = 256


def _ffn_kernel(x1_ref, h_ref, hp_ref, hn_ref, wu_ref, cw_ref, cb_ref, wd_ref, o_ref, act_scr, *, tiles_per_seq):
    i = pl.program_id(0)
    tm = h_ref.shape[0]
    f = wd_ref.shape[0]
    pos = i % tiles_per_seq
    keep_prev = jnp.where(pos > 0, 1.0, 0.0)
    keep_next = jnp.where(pos < tiles_per_seq - 1, 1.0, 0.0)
    h_ext = jnp.concatenate([hp_ref[...], h_ref[...], hn_ref[...]], axis=0)

    def conv(col):
        cs = slice(col, col + FFN_CW)
        u_ext = jnp.dot(h_ext, wu_ref[:, cs], preferred_element_type=F32)
        u = u_ext[HALO:HALO + tm]
        up = u_ext[HALO - 8:HALO] * keep_prev
        un = u_ext[HALO + tm:HALO + tm + 8] * keep_next
        return (cb_ref[:, cs] + _shift_rows(u, up, -1) * cw_ref[0:1, cs] + u * cw_ref[1:2, cs]
                + _shift_rows(u, un, 1) * cw_ref[2:3, cs])

    for c in range(f // FFN_CW):
        act = _silu(conv(c * FFN_CW)) * conv(f + c * FFN_CW)
        act_scr[:, c * FFN_CW:(c + 1) * FFN_CW] = act.astype(BF16)
    o_ref[...] = x1_ref[...] + jnp.dot(act_scr[...], wd_ref[...], preferred_element_type=F32)


def _ffn(x1, h2, w_up, conv_w, conv_b, w_down, seq, tm):
    m, d = x1.shape
    f = w_down.shape[0]
    sub = tm // HALO
    last = m // HALO - 1
    kern = functools.partial(_ffn_kernel, tiles_per_seq=seq // tm)
    resident = pl.Buffered(1)
    return pl.pallas_call(
        kern,
        grid=(m // tm,),
        in_specs=[
            pl.BlockSpec((tm, d), lambda i: (i, 0)),
            pl.BlockSpec((tm, d), lambda i: (i, 0)),
            pl.BlockSpec((HALO, d), lambda i: (jnp.maximum(i * sub - 1, 0), 0)),
            pl.BlockSpec((HALO, d), lambda i: (jnp.minimum((i + 1) * sub, last), 0)),
            pl.BlockSpec(w_up.shape, lambda i: (0, 0), pipeline_mode=resident),
            pl.BlockSpec(conv_w.shape, lambda i: (0, 0)),
            pl.BlockSpec(conv_b.shape, lambda i: (0, 0)),
            pl.BlockSpec(w_down.shape, lambda i: (0, 0), pipeline_mode=resident),
        ],
        out_specs=pl.BlockSpec((tm, d), lambda i: (i, 0)),
        out_shape=jax.ShapeDtypeStruct((m, d), F32),
        scratch_shapes=[pltpu.VMEM((tm, f), BF16)],
        compiler_params=_cparams(1),
        name="ffn",
    )(x1, h2, h2, h2, w_up, conv_w, conv_b, w_down)


def _head_expand(first_row):
    rows = jnp.arange(2 * LANES)[:, None] % LANES
    cols = jnp.arange(SSM_INNER)[None, :] // SSM_HEAD_DIM
    return (rows == cols + first_row).astype(BF16)


def _layer(x2d, batch, seq, lambda_init, p):
    d = x2d.shape[1]
    w_in = p["w_in"].astype(BF16)
    sizes = (HEADS * 2 * HEAD_DIM,) * 3 + (SSM_INNER, CONV_DIM, 2 * SSM_HEADS, 2 * d)
    offs = [0]
    for s in sizes:
        offs.append(offs[-1] + s)
    wq, wk, wv, wz, wxbc, wdt, wg = (w_in[:, offs[n]:offs[n + 1]] for n in range(7))
    w_main = jnp.concatenate([wxbc, wz, wg, wv], axis=1)
    w_qk = jnp.concatenate([wq, wk], axis=1)
    w_dt = jnp.pad(wdt, ((0, 0), (0, LANES - wdt.shape[1])))
    qkw = jnp.stack([jnp.tile(p["q_norm_w"], 2 * HEADS) * (HEAD_DIM ** -0.5 * LOG2E),
                     jnp.tile(p["k_norm_w"], 2 * HEADS)])[:, None, :]
    gi = jnp.arange(MXU_DIM) // HEAD_DIM
    gsum = (gi[:, None] == gi[None, :]).astype(BF16)

    proj, dt_raw, h1 = _in_proj(x2d, p["norm1_w"][None, :], w_main, w_dt, tm=2048)
    qk = _qk_proj(h1, w_qk, qkw, gsum, tm=1024)

    lam = (jnp.exp(jnp.sum(p["lambda_q1"] * p["lambda_k1"])) - jnp.exp(jnp.sum(p["lambda_q2"] * p["lambda_k2"]))
           + lambda_init).reshape(1).astype(F32)
    bias_small = _bias_tiles(p["rel_bias"])
    attn_o = _diff_attn(qk, proj, bias_small, p["rel_bias"], lam, p["subln_w"][None, :], batch, seq,
                        tq=512, tk=1024, out_scale=1.0 - lambda_init)

    pad = LANES - 2 * SSM_HEADS
    dt_bias = jnp.pad(jnp.concatenate([p["dt_bias_f"], p["dt_bias_b"]]), (0, pad))[None, :]
    a_log = jnp.pad(jnp.concatenate([p["a_log_f"], p["a_log_b"]]), (0, pad))[None, :]
    zt, cs, dd, ww = _ssd_prep(dt_raw, dt_bias, a_log)
    xact, ybi = _ssd_bwd(proj, dd, ww, p["ssm_conv_w"], p["ssm_conv_b"][None, :],
                         _head_expand(SSM_HEADS), batch, seq)
    ssd_y = _ssd_fwd(xact, ybi, zt, cs, dd, ww, jnp.repeat(p["d_skip"], SSM_HEAD_DIM)[None, :],
                     _head_expand(0), batch, seq)

    x1, h2 = _mix_out(x2d, attn_o, ssd_y, proj, p["ssm_norm_w"][None, :], p["w_attn_out"].astype(BF16),
                      p["w_ssm_out"].astype(BF16), p["w_out"].astype(BF16), p["norm2_w"][None, :], tm=512)

    return _ffn(x1, h2, p["w_ffn_up"].astype(BF16), p["ffn_conv_w"], p["ffn_conv_b"][None, :],
                p["w_ffn_down"].astype(BF16), seq, tm=1024)


def kernel(x, norm1_w, w_in, q_norm_w, k_norm_w, rel_bias, lambda_q1, lambda_k1, lambda_q2, lambda_k2, subln_w, w_attn_out, ssm_conv_w, ssm_conv_b, dt_bias_f, a_log_f, dt_bias_b, a_log_b, d_skip, ssm_norm_w, w_ssm_out, w_out, norm2_w, w_ffn_up, ffn_conv_w, ffn_conv_b, w_ffn_down):
    batch, seq, d = x.shape
    layered = dict(norm1_w=norm1_w, w_in=w_in, q_norm_w=q_norm_w, k_norm_w=k_norm_w, lambda_q1=lambda_q1,
                   lambda_k1=lambda_k1, lambda_q2=lambda_q2, lambda_k2=lambda_k2, subln_w=subln_w,
                   w_attn_out=w_attn_out, ssm_conv_w=ssm_conv_w, ssm_conv_b=ssm_conv_b, dt_bias_f=dt_bias_f,
                   a_log_f=a_log_f, dt_bias_b=dt_bias_b, a_log_b=a_log_b, d_skip=d_skip, ssm_norm_w=ssm_norm_w,
                   w_ssm_out=w_ssm_out, w_out=w_out, norm2_w=norm2_w, w_ffn_up=w_ffn_up, ffn_conv_w=ffn_conv_w,
                   ffn_conv_b=ffn_conv_b, w_ffn_down=w_ffn_down)
    x2d = x.reshape(batch * seq, d)
    for layer in range(norm1_w.shape[0]):
        p = {k: v[layer] for k, v in layered.items()}
        p["rel_bias"] = rel_bias
        lambda_init = 0.8 - 0.6 * math.exp(-0.3 * layer)
        x2d = _layer(x2d, batch, seq, lambda_init, p)
    return x2d.reshape(batch, seq, d)
```

```python
import functools
import math

import jax
import jax.numpy as jnp
from jax import lax
from jax.experimental import pallas as pl
from jax.experimental.pallas import tpu as pltpu

F32 = jnp.float32
BF16 = jnp.bfloat16

HEADS = 8
HEAD_DIM = 64
SSM_HEADS = 32
SSM_HEAD_DIM = 64
SSM_GROUPS = 8
SSM_STATE = 128
SSM_INNER = SSM_HEADS * SSM_HEAD_DIM
GROUP_W = SSM_INNER // SSM_GROUPS
CONV_DIM = SSM_INNER + 2 * SSM_GROUPS * SSM_STATE
CHUNK = 128
SSM_SHIFTS = (-2, -1, 1)
HALO = 16
RMS_EPS = 1e-6
LOG2E = 1.4426950408889634

REL_THRESHOLDS = (12, 16, 23, 32, 46, 64, 91)
REL_FAR = 129

LANES = 128
VMEM_LIMIT = 52 * 1024 * 1024
VMEM_LIMIT_ATTN = 60 * 1024 * 1024

MXU_DIM = 256

COL_XBC = 0
COL_Z = 4096
COL_GATE = 6144
COL_V = 8192
PROJ_W = 9216
PROJ_TILE = 1024


def _cparams(n_axes, vmem_limit=VMEM_LIMIT):
    return pltpu.CompilerParams(dimension_semantics=("arbitrary",) * n_axes, vmem_limit_bytes=vmem_limit)


def _rms(x, w):
    return x * lax.rsqrt(jnp.mean(x * x, axis=-1, keepdims=True) + RMS_EPS) * w


def _split_bf16(x):
    hi = x.astype(BF16)
    lo = (x - hi.astype(F32)).astype(BF16)
    return hi, lo


def _silu(x):
    return x / (1.0 + jnp.exp(-x))


GATE_TILES = (COL_GATE // PROJ_TILE, COL_V // PROJ_TILE - 1)


def _in_proj_kernel(x_ref, nw_ref, w_ref, wg_ref, wdt_ref, proj_ref, dt_ref, h_ref):
    j = pl.program_id(1)

    @pl.when(j == 0)
    def _():
        h = _rms(x_ref[...], nw_ref[...]).astype(BF16)
        h_ref[...] = h
        dt_ref[...] = jnp.dot(h, wdt_ref[...], preferred_element_type=F32)

    w = jnp.where((j >= GATE_TILES[0]) & (j <= GATE_TILES[-1]), wg_ref[...], w_ref[...])
    proj_ref[...] = jnp.dot(h_ref[...], w, preferred_element_type=F32).astype(BF16)


def _in_proj(x2d, norm_w, w_in, w_gate, w_dt, src_tiles, tm):
    m, d = x2d.shape
    xbc0, z0, v0 = src_tiles
    n_xbc, n_z = COL_Z // PROJ_TILE, (COL_GATE - COL_Z) // PROJ_TILE

    def w_tile(i, j):
        z_tile = z0 + jnp.minimum(j - n_xbc, n_z - 1)
        return (0, jnp.where(j < n_xbc, xbc0 + j, jnp.where(j <= GATE_TILES[-1], z_tile, v0)))

    return pl.pallas_call(
        _in_proj_kernel,
        grid=(m // tm, PROJ_W // PROJ_TILE),
        in_specs=[
            pl.BlockSpec((tm, d), lambda i, j: (i, 0)),
            pl.BlockSpec((1, d), lambda i, j: (0, 0)),
            pl.BlockSpec((d, PROJ_TILE), w_tile),
            pl.BlockSpec((d, PROJ_TILE), lambda i, j: (0, jnp.clip(j - GATE_TILES[0], 0, 1))),
            pl.BlockSpec((d, LANES), lambda i, j: (0, 0)),
        ],
        out_specs=[
            pl.BlockSpec((tm, PROJ_TILE), lambda i, j: (i, j)),
            pl.BlockSpec((tm, LANES), lambda i, j: (i, 0)),
            pl.BlockSpec((tm, d), lambda i, j: (i, 0)),
        ],
        out_shape=[
            jax.ShapeDtypeStruct((m, PROJ_W), BF16),
            jax.ShapeDtypeStruct((m, LANES), F32),
            jax.ShapeDtypeStruct((m, d), BF16),
        ],
        compiler_params=_cparams(2),
        name="in_proj",
    )(x2d, norm_w, w_in, w_gate, w_dt)


def _qk_proj_kernel(h_ref, w_ref, qkw_ref, gsum_ref, o_ref):
    acc = jnp.dot(h_ref[...], w_ref[...], preferred_element_type=F32)
    for c in range(PROJ_TILE // MXU_DIM):
        cs = slice(c * MXU_DIM, (c + 1) * MXU_DIM)
        a = acc[:, cs]
        ss = jnp.dot((a * a).astype(BF16), gsum_ref[...], preferred_element_type=F32)
        o_ref[:, cs] = (a * lax.rsqrt(ss * (1.0 / HEAD_DIM) + RMS_EPS) * qkw_ref[0, :, cs]).astype(BF16)


def _qk_proj(h, w_qk, qkw, gsum, tm):
    m, d = h.shape
    return pl.pallas_call(
        _qk_proj_kernel,
        grid=(m // tm, 2),
        in_specs=[
            pl.BlockSpec((tm, d), lambda i, j: (i, 0)),
            pl.BlockSpec((d, PROJ_TILE), lambda i, j: (0, j)),
            pl.BlockSpec((1, 1, PROJ_TILE), lambda i, j: (j, 0, 0)),
            pl.BlockSpec((MXU_DIM, MXU_DIM), lambda i, j: (0, 0)),
        ],
        out_specs=pl.BlockSpec((tm, PROJ_TILE), lambda i, j: (i, j)),
        out_shape=jax.ShapeDtypeStruct((m, 2 * PROJ_TILE), BF16),
        compiler_params=_cparams(2),
        name="qk_proj",
    )(h, w_qk, qkw, gsum)


def _bias_tiles_kernel(table_ref, out_ref):
    h = pl.program_id(0)
    r = lax.broadcasted_iota(jnp.int32, (LANES, LANES), 0)
    c = lax.broadcasted_iota(jnp.int32, (LANES, LANES), 1)
    for d in range(3):
        rel = (d - 1) * LANES + c - r
        n = jnp.abs(rel)
        large = jnp.full((LANES, LANES), 8, jnp.int32)
        for t in REL_THRESHOLDS:
            large = large + jnp.where(n >= t, 1, 0)
        bucket = jnp.where(rel > 0, 16, 0) + jnp.where(n < 8, n, large)
        val = jnp.zeros((LANES, LANES), F32)
        for bkt in range(32):
            val = jnp.where(bucket == bkt, table_ref[bkt, h], val)
        out_ref[0, d] = val * LOG2E


def _bias_tiles(rel_bias):
    return pl.pallas_call(
        _bias_tiles_kernel,
        grid=(HEADS,),
        in_specs=[pl.BlockSpec(memory_space=pltpu.SMEM)],
        out_specs=pl.BlockSpec((1, 3, LANES, LANES), lambda h: (h, 0, 0, 0)),
        out_shape=jax.ShapeDtypeStruct((HEADS, 3, LANES, LANES), F32),
        compiler_params=_cparams(1),
        name="bias_tiles",
    )(rel_bias)


def _near_bias_tiles(tq, tk):
    nbq, nbk = tq // LANES, tk // LANES
    table = {}
    for a in range(tk // tq):
        for d in (-1, 0, 1):
            ob = d * nbk - a * nbq
            if ob + nbk - 1 >= -1 and ob - (nbq - 1) <= 1:
                table[(a, d)] = (2 + len(table), ob)
    return table


SOFTMAX_LEAD = 1
S_SLOTS = 2
P_SLOTS = 2


def _attn_kernel(lam_ref, table_ref, q_ref, k_ref, v_ref, bsm_ref, subln_ref, o_ref,
                 s_scr, p_scr, mcur_scr, alpha_scr, m_scr, acc_scr, bias_scr, *, tq, tk, seq, out_scale):
    nk = seq // tk
    ratio = tk // tq
    near = _near_bias_tiles(tq, tk)
    h = pl.program_id(0)
    b = pl.program_id(1)
    c_lo = table_ref[15, h] * LOG2E
    c_hi = table_ref[31, h] * LOG2E

    @pl.when(b == 0)
    def _():
        ones = jnp.ones((tq, tk), F32)
        bias_scr[0] = ones * c_lo
        bias_scr[1] = ones * c_hi
        for idx, ob in near.values():
            for rb in range(tq // LANES):
                for cb in range(tk // LANES):
                    off = ob + cb - rb
                    if abs(off) <= 1:
                        blk = bsm_ref[0, off + 1]
                    else:
                        blk = jnp.ones((LANES, LANES), F32) * (c_lo if off < 0 else c_hi)
                    bias_scr[idx, rb * LANES:(rb + 1) * LANES, cb * LANES:(cb + 1) * LANES] = blk

    lane = lax.broadcasted_iota(jnp.int32, (tq, LANES), 1)
    zero = jnp.zeros((tq, LANES), BF16)
    ones_blk = jnp.ones((tk, LANES), BF16)

    first_near = {a: min(d for (a0, d) in near if a0 == a) for a in range(ratio)}
    n_e = max(sum(1 for (a0, _) in near if a0 == a) for a in range(ratio))

    def explicit_start(i):
        c = i // ratio
        a = i - c * ratio
        lo = c + first_near[0]
        for a0 in range(1, ratio):
            lo = jnp.where(a == a0, c + first_near[a0], lo)
        return jnp.clip(lo, 0, nk - n_e)

    def key_tile(i, pos):
        e0 = explicit_start(i)
        if pos < n_e:
            return e0 + pos, None
        k = pos - n_e
        j = k + jnp.where(k >= e0, n_e, 0)
        return j, jnp.where(j < e0, c_lo, c_hi)

    def bias_kind(i, j):
        c = i // ratio
        a = i - c * ratio
        d = j - c
        kind = jnp.where(d < 0, 0, 1)
        for (a0, d0), (idx, _) in near.items():
            kind = jnp.where((a == a0) & (d == d0), idx, kind)
        return kind

    def scores(i, pos):
        j, const = key_tile(i, pos)
        q = q_ref[pl.ds(pl.multiple_of(i * tq, tq), tq), :]
        qs = jnp.concatenate([jnp.where(lane < HEAD_DIM, q, zero), jnp.where(lane >= HEAD_DIM, q, zero)], axis=0)
        k = k_ref[pl.ds(pl.multiple_of(j * tk, tk), tk), :]
        s = lax.dot_general(qs, k, (((1,), (1,)), ((), ())), preferred_element_type=F32)
        if const is None:
            kind = bias_kind(i, j)
            s = jnp.concatenate([s[0:tq] + bias_scr[kind], s[tq:2 * tq] + bias_scr[kind]], axis=0)
            row_max = jnp.max(s, axis=1, keepdims=True)
        else:
            row_max = jnp.max(s, axis=1, keepdims=True) + const
        s_scr[pos % S_SLOTS] = s
        mcur_scr[pos % S_SLOTS] = jnp.broadcast_to(row_max, (2 * tq, LANES))

    def softmax(i, pos):
        _, const = key_tile(i, pos)
        slot, pslot = pos % S_SLOTS, pos % P_SLOTS
        if pos == 0:
            m_next = mcur_scr[slot]
        else:
            m_prev = m_scr[...]
            m_next = jnp.maximum(m_prev, mcur_scr[slot])
            alpha_scr[pslot] = jnp.exp2(m_prev - m_next)
        shift = m_next if const is None else m_next - const
        p_scr[pslot] = jnp.exp2(s_scr[slot] - jnp.concatenate([shift] * (tk // LANES), axis=1)).astype(BF16)
        m_scr[...] = m_next

    def values(i, pos):
        j, _ = key_tile(i, pos)
        pslot = pos % P_SLOTS
        vaug = jnp.concatenate([v_ref[pl.ds(pl.multiple_of(j * tk, tk), tk), :], ones_blk], axis=1)
        pv = jnp.dot(p_scr[pslot], vaug, preferred_element_type=F32)
        if pos == 0:
            acc_scr[...] = pv
        else:
            acc_scr[...] = acc_scr[...] * jnp.concatenate([alpha_scr[pslot]] * 2, axis=1) + pv

    def finalize(i):
        acc = acc_scr[...]
        o12 = acc[:, 0:LANES] / acc[:, LANES:2 * LANES]
        o = o12[0:tq] - lam_ref[0] * o12[tq:2 * tq]
        o_ref[pl.ds(pl.multiple_of(i * tq, tq), tq), :] = (_rms(o, subln_ref[...]) * out_scale).astype(BF16)

    nq = seq // tq
    assert nk % P_SLOTS == 0 and P_SLOTS > SOFTMAX_LEAD
    for pos in range(SOFTMAX_LEAD):
        scores(0, pos)
        softmax(0, pos)
    scores(0, SOFTMAX_LEAD)

    def body(g, carry):
        nxt = jnp.minimum(g + 1, nq - 1)
        for r in range(nk):
            values(g, r)
            ahead = r + SOFTMAX_LEAD
            softmax(g if ahead < nk else nxt, ahead % nk)
            scores(g if ahead + 1 < nk else nxt, (ahead + 1) % nk)
        finalize(g)
        return carry

    lax.fori_loop(0, nq, body, 0)


def _diff_attn(qk, proj, bias_small, rel_bias, lam, subln_w, batch, seq, tq, tk, out_scale):
    assert seq % tk == 0 and tk % tq == 0 and (seq // tk) % 2 == 0
    qb, kb, vb = 0, HEADS, COL_V // LANES
    kern = functools.partial(_attn_kernel, tq=tq, tk=tk, seq=seq, out_scale=out_scale)
    n_bias = 2 + len(_near_bias_tiles(tq, tk))
    return pl.pallas_call(
        kern,
        grid=(HEADS, batch),
        in_specs=[
            pl.BlockSpec(memory_space=pltpu.SMEM),
            pl.BlockSpec(memory_space=pltpu.SMEM),
            pl.BlockSpec((seq, LANES), lambda h, b: (b, qb + h)),
            pl.BlockSpec((seq, LANES), lambda h, b: (b, kb + h)),
            pl.BlockSpec((seq, LANES), lambda h, b: (b, vb + h)),
            pl.BlockSpec((1, 3, LANES, LANES), lambda h, b: (h, 0, 0, 0)),
            pl.BlockSpec((1, LANES), lambda h, b: (0, 0)),
        ],
        out_specs=pl.BlockSpec((seq, LANES), lambda h, b: (b, h)),
        out_shape=jax.ShapeDtypeStruct((batch * seq, HEADS * LANES), BF16),
        scratch_shapes=[
            pltpu.VMEM((S_SLOTS, 2 * tq, tk), F32),
            pltpu.VMEM((P_SLOTS, 2 * tq, tk), BF16),
            pltpu.VMEM((S_SLOTS, 2 * tq, LANES), F32),
            pltpu.VMEM((P_SLOTS, 2 * tq, LANES), F32),
            pltpu.VMEM((2 * tq, LANES), F32),
            pltpu.VMEM((2 * tq, 2 * LANES), F32),
            pltpu.VMEM((n_bias, tq, tk), F32),
        ],
        compiler_params=_cparams(2, VMEM_LIMIT_ATTN),
        name="diff_attn",
    )(lam, rel_bias, qk, qk, proj, bias_small, subln_w)


def _tri(upper):
    r = lax.broadcasted_iota(jnp.int32, (CHUNK, CHUNK), 0)
    c = lax.broadcasted_iota(jnp.int32, (CHUNK, CHUNK), 1)
    return (c >= r) if upper else (c <= r)


def _tri_dot(mask, x):
    m = jnp.where(mask, 1.0, 0.0).astype(BF16)
    hi, lo = _split_bf16(x)
    return jnp.dot(m, hi, preferred_element_type=F32) + jnp.dot(m, lo, preferred_element_type=F32)


PREP_CHUNKS = 16


def _ssd_prep_kernel(dt_ref, dtb_ref, alog_ref, zt_ref, cs_ref, dd_ref, ww_ref):
    nh = SSM_HEADS
    lane = lax.broadcasted_iota(jnp.int32, (CHUNK, LANES), 1)
    tril = _tri(False)
    triu = _tri(True)
    for k in range(PREP_CHUNKS):
        rows = slice(k * CHUNK, (k + 1) * CHUNK)
        raw = dt_ref[rows, :] + dtb_ref[...]
        dt = jnp.maximum(raw, 0.0) + jnp.log1p(jnp.exp(-jnp.abs(raw)))
        dt = jnp.where(lane < 2 * nh, dt, 0.0)
        da = dt * -jnp.exp(alog_ref[...])
        acs = _tri_dot(tril, da)
        rcs = _tri_dot(triu, da)
        cs = jnp.where(lane < nh, acs, rcs)
        end = jnp.where(lane < nh, acs[CHUNK - 1:CHUNK, :], rcs[0:1, :])
        cs_ref[rows, :] = cs
        dd_ref[rows, :] = jnp.exp(cs)
        ww_ref[rows, :] = jnp.exp(end - cs) * dt
        zt_ref[rows, :] = jnp.where(lane < 2 * nh, dt, pltpu.roll(cs, 2 * nh, 1)).T


def _ssd_prep(dt_raw, dt_bias, a_log):
    m = dt_raw.shape[0]
    rows = PREP_CHUNKS * CHUNK
    blk = pl.BlockSpec((rows, LANES), lambda i: (i, 0))
    vec = pl.BlockSpec((1, LANES), lambda i: (0, 0))
    return pl.pallas_call(
        _ssd_prep_kernel,
        grid=(m // rows,),
        in_specs=[blk, vec, vec],
        out_specs=[blk] * 4,
        out_shape=[jax.ShapeDtypeStruct((m, LANES), F32)] * 4,
        compiler_params=_cparams(1),
        name="ssd_prep",
    )(dt_raw, dt_bias, a_log)


def _expand_operand(x):
    hi, lo = _split_bf16(x)
    return jnp.concatenate([hi, lo], axis=1)


def _expand(x_hl, e2_ref, gs):
    return jnp.dot(x_hl, e2_ref[:, gs], preferred_element_type=F32)


def _shift_rows(u, halo, offset):
    n = u.shape[0]
    if offset == 0:
        return u
    rolled = pltpu.roll(u, (-offset) % n, 0)
    row8 = lax.broadcasted_iota(jnp.int32, (8, u.shape[1]), 0)
    if offset < 0:
        hfix = pltpu.roll(halo, (-offset) % 8, 0)
        first = jnp.where(row8 < -offset, hfix, rolled[0:8])
        return jnp.concatenate([first, rolled[8:n]], axis=0)
    hfix = pltpu.roll(halo, (8 - offset) % 8, 0)
    last = jnp.where(row8 >= 8 - offset, hfix, rolled[n - 8:n])
    return jnp.concatenate([rolled[0:n - 8], last], axis=0)


def _shift_select(n, shifts):
    t = jnp.arange(len(shifts) * n)
    src = t % n + HALO + jnp.repeat(jnp.asarray(shifts), n)
    return (src[:, None] == jnp.arange(n + 2 * HALO)[None, :]).astype(BF16)


SCAN_CHUNKS = 4


def _ssd_bwd_kernel(xc_ref, xp_ref, xn_ref, dd_ref, ww_ref, cw_ref, cb_ref, e2b_ref, sel_ref,
                    xact_ref, ybi_ref, state_scr, *, n_blocks):
    c = pl.program_id(1)
    cr = n_blocks - 1 - c

    @pl.when(c == 0)
    def _():
        state_scr[...] = jnp.zeros(state_scr.shape, F32)

    keep_prev = jnp.where(cr > 0, 1.0, 0.0).astype(BF16)
    keep_next = jnp.where(cr < n_blocks - 1, 1.0, 0.0).astype(BF16)
    slab = 512
    for sl in range(CONV_DIM // slab):
        cs = slice(sl * slab, (sl + 1) * slab)
        ext = jnp.concatenate([xp_ref[:, cs] * keep_prev, xc_ref[:, cs], xn_ref[:, cs] * keep_next], axis=0)
        for k in range(SCAN_CHUNKS):
            rows = slice(k * CHUNK, (k + 1) * CHUNK)
            sh = jnp.dot(sel_ref[...], ext[k * CHUNK:(k + 1) * CHUNK + 2 * HALO], preferred_element_type=F32)
            acc = cb_ref[:, cs] + xc_ref[rows, cs].astype(F32) * cw_ref[2:3, cs]
            for j, off in enumerate(SSM_SHIFTS):
                acc = acc + sh[j * CHUNK:(j + 1) * CHUNK] * cw_ref[off + 2:off + 3, cs]
            xact_ref[rows, cs] = _silu(acc).astype(BF16)

    for k in reversed(range(SCAN_CHUNKS)):
        rows = slice(k * CHUNK, (k + 1) * CHUNK)
        everything = slice(0, SSM_INNER)
        decay = _expand(_expand_operand(dd_ref[rows, :]), e2b_ref, everything)
        chunk_decay = decay[0:1, :]
        xs = xact_ref[rows, 0:SSM_INNER].astype(F32)
        xw = (xs * _expand(_expand_operand(ww_ref[rows, :]), e2b_ref, everything)).astype(BF16)
        for g in range(SSM_GROUPS):
            gs = slice(g * GROUP_W, (g + 1) * GROUP_W)
            bg = xact_ref[rows, SSM_INNER + g * SSM_STATE:SSM_INNER + (g + 1) * SSM_STATE]
            cg = xact_ref[rows, SSM_INNER + (SSM_GROUPS + g) * SSM_STATE:SSM_INNER + (SSM_GROUPS + g + 1) * SSM_STATE]
            st = state_scr[:, gs]
            yb = jnp.dot(cg, st.astype(BF16), preferred_element_type=F32) * decay[:, gs]
            ybi_ref[rows, gs] = yb.astype(BF16)
            upd = lax.dot_general(bg, xw[:, gs], (((0,), (0,)), ((), ())), preferred_element_type=F32)
            state_scr[:, gs] = st * chunk_decay[:, gs] + upd


def _ssd_bwd(proj, dd, ww, conv_w, conv_b, e2b, batch, seq):
    rows = SCAN_CHUNKS * CHUNK
    n_blocks = seq // rows
    sub = rows // HALO
    last16 = batch * seq // HALO - 1
    kern = functools.partial(_ssd_bwd_kernel, n_blocks=n_blocks)
    select = _shift_select(CHUNK, SSM_SHIFTS)

    def cur(b, c):
        return b * n_blocks + (n_blocks - 1 - c)

    return pl.pallas_call(
        kern,
        grid=(batch, n_blocks),
        in_specs=[
            pl.BlockSpec((rows, CONV_DIM), lambda b, c: (cur(b, c), 0)),
            pl.BlockSpec((HALO, CONV_DIM), lambda b, c: (jnp.maximum(cur(b, c) * sub - 1, 0), 0)),
            pl.BlockSpec((HALO, CONV_DIM), lambda b, c: (jnp.minimum((cur(b, c) + 1) * sub, last16), 0)),
            pl.BlockSpec((rows, LANES), lambda b, c: (cur(b, c), 0)),
            pl.BlockSpec((rows, LANES), lambda b, c: (cur(b, c), 0)),
            pl.BlockSpec((4, CONV_DIM), lambda b, c: (0, 0)),
            pl.BlockSpec((1, CONV_DIM), lambda b, c: (0, 0)),
            pl.BlockSpec((2 * LANES, SSM_INNER), lambda b, c: (0, 0)),
            pl.BlockSpec(select.shape, lambda b, c: (0, 0)),
        ],
        out_specs=[
            pl.BlockSpec((rows, CONV_DIM), lambda b, c: (cur(b, c), 0)),
            pl.BlockSpec((rows, SSM_INNER), lambda b, c: (cur(b, c), 0)),
        ],
        out_shape=[
            jax.ShapeDtypeStruct((batch * seq, CONV_DIM), BF16),
            jax.ShapeDtypeStruct((batch * seq, SSM_INNER), BF16),
        ],
        scratch_shapes=[pltpu.VMEM((SSM_STATE, SSM_INNER), F32)],
        compiler_params=_cparams(2),
        name="ssd_bwd",
    )(proj, proj, proj, dd, ww, conv_w, conv_b, e2b, select)


def _ssd_fwd_kernel(xact_ref, ybi_ref, zt_ref, cs_ref, dd_ref, ww_ref, dsk_ref, e2f_ref, out_ref, state_scr):
    @pl.when(pl.program_id(1) == 0)
    def _():
        state_scr[...] = jnp.zeros(state_scr.shape, F32)

    nh = SSM_HEADS
    lane_w = lax.broadcasted_iota(jnp.int32, (CHUNK, LANES), 1)
    row_i = lax.broadcasted_iota(jnp.int32, (CHUNK, CHUNK), 0)
    col_i = lax.broadcasted_iota(jnp.int32, (CHUNK, CHUNK), 1)
    lower = col_i < row_i
    upper = col_i > row_i

    for k in range(SCAN_CHUNKS):
        rows = slice(k * CHUNK, (k + 1) * CHUNK)
        zt = zt_ref.at[rows]
        cs = cs_ref[rows, :]
        dd_hl = _expand_operand(dd_ref[rows, :])
        ww_hl = _expand_operand(ww_ref[rows, :])
        for g in range(SSM_GROUPS):
            gs = slice(g * GROUP_W, (g + 1) * GROUP_W)
            bg = xact_ref[rows, SSM_INNER + g * SSM_STATE:SSM_INNER + (g + 1) * SSM_STATE]
            cg = xact_ref[rows, SSM_INNER + (SSM_GROUPS + g) * SSM_STATE:SSM_INNER + (SSM_GROUPS + g + 1) * SSM_STATE]
            xs_bf = xact_ref[rows, gs]
            xs = xs_bf.astype(F32)
            decay = _expand(dd_hl, e2f_ref, gs)
            xw = (xs * _expand(ww_hl, e2f_ref, gs)).astype(BF16)
            cb = lax.dot_general(cg, bg, (((1,), (1,)), ((), ())), preferred_element_type=F32)
            ys = []
            for r in range(GROUP_W // SSM_HEAD_DIM):
                h = g * (GROUP_W // SSM_HEAD_DIM) + r
                seg_f = cs[:, h:h + 1] - zt[2 * nh + h:2 * nh + h + 1, :]
                seg_b = cs[:, nh + h:nh + h + 1] - zt[3 * nh + h:3 * nh + h + 1, :]
                dt_f = zt[h:h + 1, :]
                dt_b = zt[nh + h:nh + h + 1, :]
                dt_sel = jnp.where(lower, dt_f, jnp.where(upper, dt_b, dt_f + dt_b))
                mat = (cb * (jnp.exp(jnp.where(upper, seg_b, seg_f)) * dt_sel)).astype(BF16)
                pair = xs_bf[:, (r // 2) * LANES:(r // 2 + 1) * LANES]
                ys.append(jnp.dot(mat, pair, preferred_element_type=F32))
            y = jnp.concatenate([jnp.where(lane_w < SSM_HEAD_DIM, ys[0], ys[1]),
                                 jnp.where(lane_w < SSM_HEAD_DIM, ys[2], ys[3])], axis=1)
            st = state_scr[:, gs]
            y = y + jnp.dot(cg, st.astype(BF16), preferred_element_type=F32) * decay
            upd = lax.dot_general(bg, xw, (((0,), (0,)), ((), ())), preferred_element_type=F32)
            state_scr[:, gs] = st * decay[CHUNK - 1:CHUNK, :] + upd
            y = y + ybi_ref[rows, gs].astype(F32) + dsk_ref[:, gs] * xs
            out_ref[rows, gs] = y.astype(BF16)


def _ssd_fwd(xact, ybi, zt, cs, dd, ww, d_skip, e2f, batch, seq):
    rows = SCAN_CHUNKS * CHUNK
    n_blocks = seq // rows
    small = pl.BlockSpec((rows, LANES), lambda b, c: (b * n_blocks + c, 0))
    return pl.pallas_call(
        _ssd_fwd_kernel,
        grid=(batch, n_blocks),
        in_specs=[
            pl.BlockSpec((rows, CONV_DIM), lambda b, c: (b * n_blocks + c, 0)),
            pl.BlockSpec((rows, SSM_INNER), lambda b, c: (b * n_blocks + c, 0)),
            small, small, small, small,
            pl.BlockSpec((1, SSM_INNER), lambda b, c: (0, 0)),
            pl.BlockSpec((2 * LANES, SSM_INNER), lambda b, c: (0, 0)),
        ],
        out_specs=pl.BlockSpec((rows, SSM_INNER), lambda b, c: (b * n_blocks + c, 0)),
        out_shape=jax.ShapeDtypeStruct((batch * seq, SSM_INNER), BF16),
        scratch_shapes=[pltpu.VMEM((SSM_STATE, SSM_INNER), F32)],
        compiler_params=_cparams(2),
        name="ssd_fwd",
    )(xact, ybi, zt, cs, dd, ww, d_skip, e2f)


def _mix_out_kernel(x_ref, ao_ref, so_ref, z_ref, nw_ref, ga_ref, gs_ref, wa_ref, ws_ref, wo_ref, n2_ref, x1_ref,
                    h2_ref):
    attn = jnp.dot(ao_ref[...], wa_ref[...], preferred_element_type=F32)
    ssd = None
    for g in range(SSM_GROUPS):
        gs = slice(g * GROUP_W, (g + 1) * GROUP_W)
        y = so_ref[:, gs].astype(F32) * _silu(z_ref[:, gs].astype(F32))
        part = jnp.dot(_rms(y, nw_ref[:, gs]).astype(BF16), ws_ref[gs, :], preferred_element_type=F32)
        ssd = part if ssd is None else ssd + part
    mixed = (jax.nn.sigmoid(ga_ref[...].astype(F32)) * attn
             + jax.nn.sigmoid(gs_ref[...].astype(F32)) * ssd)
    x1 = x_ref[...] + jnp.dot(mixed.astype(BF16), wo_ref[...], preferred_element_type=F32)
    x1_ref[...] = x1
    h2_ref[...] = _rms(x1, n2_ref[...]).astype(BF16)


def _mix_out(x2d, attn_o, ssd_o, proj, ssm_norm_w, w_attn, w_ssm, w_out, norm2_w, tm):
    m, d = x2d.shape
    gb = COL_GATE // d
    zb = COL_Z // SSM_INNER
    return pl.pallas_call(
        _mix_out_kernel,
        grid=(m // tm,),
        in_specs=[
            pl.BlockSpec((tm, d), lambda i: (i, 0)),
            pl.BlockSpec((tm, attn_o.shape[1]), lambda i: (i, 0)),
            pl.BlockSpec((tm, ssd_o.shape[1]), lambda i: (i, 0)),
            pl.BlockSpec((tm, SSM_INNER), lambda i: (i, zb)),
            pl.BlockSpec((1, SSM_INNER), lambda i: (0, 0)),
            pl.BlockSpec((tm, d), lambda i: (i, gb)),
            pl.BlockSpec((tm, d), lambda i: (i, gb + 1)),
            pl.BlockSpec(w_attn.shape, lambda i: (0, 0)),
            pl.BlockSpec(w_ssm.shape, lambda i: (0, 0)),
            pl.BlockSpec(w_out.shape, lambda i: (0, 0)),
            pl.BlockSpec((1, d), lambda i: (0, 0)),
        ],
        out_specs=[pl.BlockSpec((tm, d), lambda i: (i, 0)), pl.BlockSpec((tm, d), lambda i: (i, 0))],
        out_shape=[jax.ShapeDtypeStruct((m, d), F32), jax.ShapeDtypeStruct((m, d), BF16)],
        compiler_params=_cparams(1),
        name="mix_out",
    )(x2d, attn_o, ssd_o, proj, ssm_norm_w, proj, proj, w_attn, w_ssm, w_out, norm2_w)


FFN_CW = 256


def _ffn_kernel(x1_ref, h_ref, hp_ref, hn_ref, wu_ref, cw_ref, cb_ref, wd_ref, o_ref, act_scr, *, tiles_per_seq):
    i = pl.program_id(0)
    tm = h_ref.shape[0]
    f = wd_ref.shape[0]
    pos = i % tiles_per_seq
    keep_prev = jnp.where(pos > 0, 1.0, 0.0)
    keep_next = jnp.where(pos < tiles_per_seq - 1, 1.0, 0.0)
    h_ext = jnp.concatenate([hp_ref[...], h_ref[...], hn_ref[...]], axis=0)

    def conv(col):
        cs = slice(col, col + FFN_CW)
        u_ext = jnp.dot(h_ext, wu_ref[:, cs], preferred_element_type=F32)
        u = u_ext[HALO:HALO + tm]
        up = u_ext[HALO - 8:HALO] * keep_prev
        un = u_ext[HALO + tm:HALO + tm + 8] * keep_next
        return (cb_ref[:, cs] + _shift_rows(u, up, -1) * cw_ref[0:1, cs] + u * cw_ref[1:2, cs]
                + _shift_rows(u, un, 1) * cw_ref[2:3, cs])

    for c in range(f // FFN_CW):
        act = _silu(conv(c * FFN_CW)) * conv(f + c * FFN_CW)
        act_scr[:, c * FFN_CW:(c + 1) * FFN_CW] = act.astype(BF16)
    o_ref[...] = x1_ref[...] + jnp.dot(act_scr[...], wd_ref[...], preferred_element_type=F32)


def _ffn(x1, h2, w_up, conv_w, conv_b, w_down, seq, tm):
    m, d = x1.shape
    f = w_down.shape[0]
    sub = tm // HALO
    last = m // HALO - 1
    kern = functools.partial(_ffn_kernel, tiles_per_seq=seq // tm)
    resident = pl.Buffered(1)
    return pl.pallas_call(
        kern,
        grid=(m // tm,),
        in_specs=[
            pl.BlockSpec((tm, d), lambda i: (i, 0)),
            pl.BlockSpec((tm, d), lambda i: (i, 0)),
            pl.BlockSpec((HALO, d), lambda i: (jnp.maximum(i * sub - 1, 0), 0)),
            pl.BlockSpec((HALO, d), lambda i: (jnp.minimum((i + 1) * sub, last), 0)),
            pl.BlockSpec(w_up.shape, lambda i: (0, 0), pipeline_mode=resident),
            pl.BlockSpec(conv_w.shape, lambda i: (0, 0)),
            pl.BlockSpec(conv_b.shape, lambda i: (0, 0)),
            pl.BlockSpec(w_down.shape, lambda i: (0, 0), pipeline_mode=resident),
        ],
        out_specs=pl.BlockSpec((tm, d), lambda i: (i, 0)),
        out_shape=jax.ShapeDtypeStruct((m, d), F32),
        scratch_shapes=[pltpu.VMEM((tm, f), BF16)],
        compiler_params=_cparams(1),
        name="ffn",
    )(x1, h2, h2, h2, w_up, conv_w, conv_b, w_down)


def _head_expand(first_row):
    rows = jnp.arange(2 * LANES)[:, None] % LANES
    cols = jnp.arange(SSM_INNER)[None, :] // SSM_HEAD_DIM
    return (rows == cols + first_row).astype(BF16)


def _layer(x2d, batch, seq, lambda_init, p):
    d = x2d.shape[1]
    w_in = p["w_in"].astype(BF16)
    sizes = (HEADS * 2 * HEAD_DIM,) * 3 + (SSM_INNER, CONV_DIM, 2 * SSM_HEADS, 2 * d)
    offs = [0]
    for s in sizes:
        offs.append(offs[-1] + s)
    assert all(o % PROJ_TILE == 0 for o in offs[:6])
    w_dt = jnp.pad(w_in[:, offs[5]:offs[6]], ((0, 0), (0, LANES - sizes[5])))
    w_gate = w_in[:, offs[6]:offs[7]]
    qkw = jnp.stack([jnp.tile(p["q_norm_w"], 2 * HEADS) * (HEAD_DIM ** -0.5 * LOG2E),
                     jnp.tile(p["k_norm_w"], 2 * HEADS)])[:, None, :]
    gi = jnp.arange(MXU_DIM) // HEAD_DIM
    gsum = (gi[:, None] == gi[None, :]).astype(BF16)

    proj, dt_raw, h1 = _in_proj(x2d, p["norm1_w"][None, :], w_in, w_gate, w_dt,
                                src_tiles=(offs[4] // PROJ_TILE, offs[3] // PROJ_TILE, offs[2] // PROJ_TILE),
                                tm=2048)
    qk = _qk_proj(h1, w_in, qkw, gsum, tm=1024)

    lam = (jnp.exp(jnp.sum(p["lambda_q1"] * p["lambda_k1"])) - jnp.exp(jnp.sum(p["lambda_q2"] * p["lambda_k2"]))
           + lambda_init).reshape(1).astype(F32)
    bias_small = _bias_tiles(p["rel_bias"])
    attn_o = _diff_attn(qk, proj, bias_small, p["rel_bias"], lam, p["subln_w"][None, :], batch, seq,
                        tq=512, tk=1024, out_scale=1.0 - lambda_init)

    pad = LANES - 2 * SSM_HEADS
    dt_bias = jnp.pad(jnp.concatenate([p["dt_bias_f"], p["dt_bias_b"]]), (0, pad))[None, :]
    a_log = jnp.pad(jnp.concatenate([p["a_log_f"], p["a_log_b"]]), (0, pad))[None, :]
    zt, cs, dd, ww = _ssd_prep(dt_raw, dt_bias, a_log)
    xact, ybi = _ssd_bwd(proj, dd, ww, p["ssm_conv_w"], p["ssm_conv_b"][None, :],
                         _head_expand(SSM_HEADS), batch, seq)
    ssd_y = _ssd_fwd(xact, ybi, zt, cs, dd, ww, jnp.repeat(p["d_skip"], SSM_HEAD_DIM)[None, :],
                     _head_expand(0), batch, seq)

    x1, h2 = _mix_out(x2d, attn_o, ssd_y, proj, p["ssm_norm_w"][None, :], p["w_attn_out"].astype(BF16),
                      p["w_ssm_out"].astype(BF16), p["w_out"].astype(BF16), p["norm2_w"][None, :], tm=512)

    return _ffn(x1, h2, p["w_ffn_up"].astype(BF16), p["ffn_conv_w"], p["ffn_conv_b"][None, :],
                p["w_ffn_down"].astype(BF16), seq, tm=1024)


def kernel(x, norm1_w, w_in, q_norm_w, k_norm_w, rel_bias, lambda_q1, lambda_k1, lambda_q2, lambda_k2, subln_w, w_attn_out, ssm_conv_w, ssm_conv_b, dt_bias_f, a_log_f, dt_bias_b, a_log_b, d_skip, ssm_norm_w, w_ssm_out, w_out, norm2_w, w_ffn_up, ffn_conv_w, ffn_conv_b, w_ffn_down):
    batch, seq, d = x.shape
    layered = dict(norm1_w=norm1_w, w_in=w_in, q_norm_w=q_norm_w, k_norm_w=k_norm_w, lambda_q1=lambda_q1,
                   lambda_k1=lambda_k1, lambda_q2=lambda_q2, lambda_k2=lambda_k2, subln_w=subln_w,
                   w_attn_out=w_attn_out, ssm_conv_w=ssm_conv_w, ssm_conv_b=ssm_conv_b, dt_bias_f=dt_bias_f,
                   a_log_f=a_log_f, dt_bias_b=dt_bias_b, a_log_b=a_log_b, d_skip=d_skip, ssm_norm_w=ssm_norm_w,
                   w_ssm_out=w_ssm_out, w_out=w_out, norm2_w=norm2_w, w_ffn_up=w_ffn_up, ffn_conv_w=ffn_conv_w,
                   ffn_conv_b=ffn_conv_b, w_ffn_down=w_ffn_down)
    x2d = x.reshape(batch * seq, d)
    for layer in range(norm1_w.shape[0]):
        p = {k: v[layer] for k, v in layered.items()}
        p["rel_bias"] = rel_bias
        lambda_init = 0.8 - 0.6 * math.exp(-0.3 * layer)
        x2d = _layer(x2d, batch, seq, lambda_init, p)
    return x2d.reshape(batch, seq, d)
```

```python
import functools
import math

import jax
import jax.numpy as jnp
from jax import lax
from jax.experimental import pallas as pl
from jax.experimental.pallas import tpu as pltpu

F32 = jnp.float32
BF16 = jnp.bfloat16

HEADS = 8
HEAD_DIM = 64
SSM_HEADS = 32
SSM_HEAD_DIM = 64
SSM_GROUPS = 8
SSM_STATE = 128
SSM_INNER = SSM_HEADS * SSM_HEAD_DIM
GROUP_W = SSM_INNER // SSM_GROUPS
CONV_DIM = SSM_INNER + 2 * SSM_GROUPS * SSM_STATE
CHUNK = 128
SSM_SHIFTS = (-2, -1, 1)
HALO = 16
RMS_EPS = 1e-6
LOG2E = 1.4426950408889634

REL_THRESHOLDS = (12, 16, 23, 32, 46, 64, 91)
REL_FAR = 129

LANES = 128
VMEM_LIMIT = 52 * 1024 * 1024
VMEM_LIMIT_ATTN = 60 * 1024 * 1024

MXU_DIM = 256

COL_XBC = 0
COL_Z = 4096
COL_GATE = 6144
COL_V = 8192
PROJ_W = 9216
PROJ_TILE = 1024


def _cparams(n_axes, vmem_limit=VMEM_LIMIT):
    return pltpu.CompilerParams(dimension_semantics=("arbitrary",) * n_axes, vmem_limit_bytes=vmem_limit)


def _rms(x, w):
    return x * lax.rsqrt(jnp.mean(x * x, axis=-1, keepdims=True) + RMS_EPS) * w


def _split_bf16(x):
    hi = x.astype(BF16)
    lo = (x - hi.astype(F32)).astype(BF16)
    return hi, lo


def _silu(x):
    return x / (1.0 + jnp.exp(-x))


def _in_proj_kernel(x_ref, nw_ref, w_ref, wdt_ref, proj_ref, dt_ref, h_ref):
    @pl.when(pl.program_id(1) == 0)
    def _():
        h = _rms(x_ref[...], nw_ref[...]).astype(BF16)
        h_ref[...] = h
        dt_ref[...] = jnp.dot(h, wdt_ref[...], preferred_element_type=F32)

    proj_ref[...] = jnp.dot(h_ref[...], w_ref[...], preferred_element_type=F32).astype(BF16)


def _in_proj(x2d, norm_w, w_main, w_dt, tm):
    m, d = x2d.shape
    return pl.pallas_call(
        _in_proj_kernel,
        grid=(m // tm, PROJ_W // PROJ_TILE),
        in_specs=[
            pl.BlockSpec((tm, d), lambda i, j: (i, 0)),
            pl.BlockSpec((1, d), lambda i, j: (0, 0)),
            pl.BlockSpec((d, PROJ_TILE), lambda i, j: (0, j)),
            pl.BlockSpec((d, LANES), lambda i, j: (0, 0)),
        ],
        out_specs=[
            pl.BlockSpec((tm, PROJ_TILE), lambda i, j: (i, j)),
            pl.BlockSpec((tm, LANES), lambda i, j: (i, 0)),
            pl.BlockSpec((tm, d), lambda i, j: (i, 0)),
        ],
        out_shape=[
            jax.ShapeDtypeStruct((m, PROJ_W), BF16),
            jax.ShapeDtypeStruct((m, LANES), F32),
            jax.ShapeDtypeStruct((m, d), BF16),
        ],
        compiler_params=_cparams(2),
        name="in_proj",
    )(x2d, norm_w, w_main, w_dt)


def _qk_proj_kernel(h_ref, w_ref, qkw_ref, gsum_ref, o_ref):
    acc = jnp.dot(h_ref[...], w_ref[...], preferred_element_type=F32)
    for c in range(PROJ_TILE // MXU_DIM):
        cs = slice(c * MXU_DIM, (c + 1) * MXU_DIM)
        a = acc[:, cs]
        ss = jnp.dot((a * a).astype(BF16), gsum_ref[...], preferred_element_type=F32)
        o_ref[:, cs] = (a * lax.rsqrt(ss * (1.0 / HEAD_DIM) + RMS_EPS) * qkw_ref[0, :, cs]).astype(BF16)


def _qk_proj(h, w_qk, qkw, gsum, tm):
    m, d = h.shape
    return pl.pallas_call(
        _qk_proj_kernel,
        grid=(m // tm, 2),
        in_specs=[
            pl.BlockSpec((tm, d), lambda i, j: (i, 0)),
            pl.BlockSpec((d, PROJ_TILE), lambda i, j: (0, j)),
            pl.BlockSpec((1, 1, PROJ_TILE), lambda i, j: (j, 0, 0)),
            pl.BlockSpec((MXU_DIM, MXU_DIM), lambda i, j: (0, 0)),
        ],
        out_specs=pl.BlockSpec((tm, PROJ_TILE), lambda i, j: (i, j)),
        out_shape=jax.ShapeDtypeStruct((m, 2 * PROJ_TILE), BF16),
        compiler_params=_cparams(2),
        name="qk_proj",
    )(h, w_qk, qkw, gsum)


def _bias_tiles_kernel(table_ref, out_ref):
    h = pl.program_id(0)
    r = lax.broadcasted_iota(jnp.int32, (LANES, LANES), 0)
    c = lax.broadcasted_iota(jnp.int32, (LANES, LANES), 1)
    for d in range(3):
        rel = (d - 1) * LANES + c - r
        n = jnp.abs(rel)
        large = jnp.full((LANES, LANES), 8, jnp.int32)
        for t in REL_THRESHOLDS:
            large = large + jnp.where(n >= t, 1, 0)
        bucket = jnp.where(rel > 0, 16, 0) + jnp.where(n < 8, n, large)
        val = jnp.zeros((LANES, LANES), F32)
        for bkt in range(32):
            val = jnp.where(bucket == bkt, table_ref[bkt, h], val)
        out_ref[0, d] = val * LOG2E


def _bias_tiles(rel_bias):
    return pl.pallas_call(
        _bias_tiles_kernel,
        grid=(HEADS,),
        in_specs=[pl.BlockSpec(memory_space=pltpu.SMEM)],
        out_specs=pl.BlockSpec((1, 3, LANES, LANES), lambda h: (h, 0, 0, 0)),
        out_shape=jax.ShapeDtypeStruct((HEADS, 3, LANES, LANES), F32),
        compiler_params=_cparams(1),
        name="bias_tiles",
    )(rel_bias)


def _near_bias_tiles(tq, tk):
    nbq, nbk = tq // LANES, tk // LANES
    table = {}
    for a in range(tk // tq):
        for d in (-1, 0, 1):
            ob = d * nbk - a * nbq
            if ob + nbk - 1 >= -1 and ob - (nbq - 1) <= 1:
                table[(a, d)] = (2 + len(table), ob)
    return table


SOFTMAX_LEAD = 1
S_SLOTS = 2
P_SLOTS = 2


def _attn_kernel(lam_ref, table_ref, q_ref, k_ref, v_ref, bsm_ref, subln_ref, o_ref,
                 s_scr, p_scr, mcur_scr, alpha_scr, m_scr, acc_scr, bias_scr, *, tq, tk, seq, out_scale):
    nk = seq // tk
    ratio = tk // tq
    near = _near_bias_tiles(tq, tk)
    h = pl.program_id(0)
    b = pl.program_id(1)
    c_lo = table_ref[15, h] * LOG2E
    c_hi = table_ref[31, h] * LOG2E

    @pl.when(b == 0)
    def _():
        ones = jnp.ones((tq, tk), F32)
        bias_scr[0] = ones * c_lo
        bias_scr[1] = ones * c_hi
        for idx, ob in near.values():
            for rb in range(tq // LANES):
                for cb in range(tk // LANES):
                    off = ob + cb - rb
                    if abs(off) <= 1:
                        blk = bsm_ref[0, off + 1]
                    else:
                        blk = jnp.ones((LANES, LANES), F32) * (c_lo if off < 0 else c_hi)
                    bias_scr[idx, rb * LANES:(rb + 1) * LANES, cb * LANES:(cb + 1) * LANES] = blk

    lane = lax.broadcasted_iota(jnp.int32, (tq, LANES), 1)
    zero = jnp.zeros((tq, LANES), BF16)
    ones_blk = jnp.ones((tk, LANES), BF16)

    first_near = {a: min(d for (a0, d) in near if a0 == a) for a in range(ratio)}
    n_e = max(sum(1 for (a0, _) in near if a0 == a) for a in range(ratio))

    def explicit_start(i):
        c = i // ratio
        a = i - c * ratio
        lo = c + first_near[0]
        for a0 in range(1, ratio):
            lo = jnp.where(a == a0, c + first_near[a0], lo)
        return jnp.clip(lo, 0, nk - n_e)

    def key_tile(i, pos):
        e0 = explicit_start(i)
        if pos < n_e:
            return e0 + pos, None
        k = pos - n_e
        j = k + jnp.where(k >= e0, n_e, 0)
        return j, jnp.where(j < e0, c_lo, c_hi)

    def bias_kind(i, j):
        c = i // ratio
        a = i - c * ratio
        d = j - c
        kind = jnp.where(d < 0, 0, 1)
        for (a0, d0), (idx, _) in near.items():
            kind = jnp.where((a == a0) & (d == d0), idx, kind)
        return kind

    def scores(i, pos):
        j, const = key_tile(i, pos)
        q = q_ref[pl.ds(pl.multiple_of(i * tq, tq), tq), :]
        qs = jnp.concatenate([jnp.where(lane < HEAD_DIM, q, zero), jnp.where(lane >= HEAD_DIM, q, zero)], axis=0)
        k = k_ref[pl.ds(pl.multiple_of(j * tk, tk), tk), :]
        s = lax.dot_general(qs, k, (((1,), (1,)), ((), ())), preferred_element_type=F32)
        if const is None:
            kind = bias_kind(i, j)
            s = jnp.concatenate([s[0:tq] + bias_scr[kind], s[tq:2 * tq] + bias_scr[kind]], axis=0)
            row_max = jnp.max(s, axis=1, keepdims=True)
        else:
            row_max = jnp.max(s, axis=1, keepdims=True) + const
        s_scr[pos % S_SLOTS] = s
        mcur_scr[pos % S_SLOTS] = jnp.broadcast_to(row_max, (2 * tq, LANES))

    def softmax(i, pos):
        _, const = key_tile(i, pos)
        slot, pslot = pos % S_SLOTS, pos % P_SLOTS
        for rs in (slice(0, tq), slice(tq, 2 * tq)):
            if pos == 0:
                m_next = mcur_scr[slot, rs]
            else:
                m_prev = m_scr[rs]
                m_next = jnp.maximum(m_prev, mcur_scr[slot, rs])
                alpha_scr[pslot, rs] = jnp.exp2(m_prev - m_next)
            shift = m_next if const is None else m_next - const
            p_scr[pslot, rs] = jnp.exp2(s_scr[slot, rs]
                                        - jnp.concatenate([shift] * (tk // LANES), axis=1)).astype(BF16)
            m_scr[rs] = m_next

    def values(i, pos):
        j, _ = key_tile(i, pos)
        pslot = pos % P_SLOTS
        vaug = jnp.concatenate([v_ref[pl.ds(pl.multiple_of(j * tk, tk), tk), :], ones_blk], axis=1)
        pv = jnp.dot(p_scr[pslot], vaug, preferred_element_type=F32)
        if pos == 0:
            acc_scr[...] = pv
        else:
            acc_scr[...] = acc_scr[...] * jnp.concatenate([alpha_scr[pslot]] * 2, axis=1) + pv

    def finalize(i):
        acc = acc_scr[...]
        o12 = acc[:, 0:LANES] / acc[:, LANES:2 * LANES]
        o = o12[0:tq] - lam_ref[0] * o12[tq:2 * tq]
        o_ref[pl.ds(pl.multiple_of(i * tq, tq), tq), :] = (_rms(o, subln_ref[...]) * out_scale).astype(BF16)

    nq = seq // tq
    assert nk % P_SLOTS == 0 and P_SLOTS > SOFTMAX_LEAD
    for pos in range(SOFTMAX_LEAD):
        scores(0, pos)
        softmax(0, pos)
    scores(0, SOFTMAX_LEAD)

    def body(g, carry):
        nxt = jnp.minimum(g + 1, nq - 1)
        for r in range(nk):
            values(g, r)
            ahead = r + SOFTMAX_LEAD
            softmax(g if ahead < nk else nxt, ahead % nk)
            scores(g if ahead + 1 < nk else nxt, (ahead + 1) % nk)
        finalize(g)
        return carry

    lax.fori_loop(0, nq, body, 0)


def _diff_attn(qk, proj, bias_small, rel_bias, lam, subln_w, batch, seq, tq, tk, out_scale):
    assert seq % tk == 0 and tk % tq == 0 and (seq // tk) % 2 == 0
    qb, kb, vb = 0, HEADS, COL_V // LANES
    kern = functools.partial(_attn_kernel, tq=tq, tk=tk, seq=seq, out_scale=out_scale)
    n_bias = 2 + len(_near_bias_tiles(tq, tk))
    return pl.pallas_call(
        kern,
        grid=(HEADS, batch),
        in_specs=[
            pl.BlockSpec(memory_space=pltpu.SMEM),
            pl.BlockSpec(memory_space=pltpu.SMEM),
            pl.BlockSpec((seq, LANES), lambda h, b: (b, qb + h)),
            pl.BlockSpec((seq, LANES), lambda h, b: (b, kb + h)),
            pl.BlockSpec((seq, LANES), lambda h, b: (b, vb + h)),
            pl.BlockSpec((1, 3, LANES, LANES), lambda h, b: (h, 0, 0, 0)),
            pl.BlockSpec((1, LANES), lambda h, b: (0, 0)),
        ],
        out_specs=pl.BlockSpec((seq, LANES), lambda h, b: (b, h)),
        out_shape=jax.ShapeDtypeStruct((batch * seq, HEADS * LANES), BF16),
        scratch_shapes=[
            pltpu.VMEM((S_SLOTS, 2 * tq, tk), F32),
            pltpu.VMEM((P_SLOTS, 2 * tq, tk), BF16),
            pltpu.VMEM((S_SLOTS, 2 * tq, LANES), F32),
            pltpu.VMEM((P_SLOTS, 2 * tq, LANES), F32),
            pltpu.VMEM((2 * tq, LANES), F32),
            pltpu.VMEM((2 * tq, 2 * LANES), F32),
            pltpu.VMEM((n_bias, tq, tk), F32),
        ],
        compiler_params=_cparams(2, VMEM_LIMIT_ATTN),
        name="diff_attn",
    )(lam, rel_bias, qk, qk, proj, bias_small, subln_w)


def _tri(upper):
    r = lax.broadcasted_iota(jnp.int32, (CHUNK, CHUNK), 0)
    c = lax.broadcasted_iota(jnp.int32, (CHUNK, CHUNK), 1)
    return (c >= r) if upper else (c <= r)


def _tri_dot(mask, x):
    m = jnp.where(mask, 1.0, 0.0).astype(BF16)
    hi, lo = _split_bf16(x)
    return jnp.dot(m, hi, preferred_element_type=F32) + jnp.dot(m, lo, preferred_element_type=F32)


PREP_CHUNKS = 16


def _ssd_prep_kernel(dt_ref, dtb_ref, alog_ref, zt_ref, cs_ref, dd_ref, ww_ref):
    nh = SSM_HEADS
    lane = lax.broadcasted_iota(jnp.int32, (CHUNK, LANES), 1)
    tril = _tri(False)
    triu = _tri(True)
    for k in range(PREP_CHUNKS):
        rows = slice(k * CHUNK, (k + 1) * CHUNK)
        raw = dt_ref[rows, :] + dtb_ref[...]
        dt = jnp.maximum(raw, 0.0) + jnp.log1p(jnp.exp(-jnp.abs(raw)))
        dt = jnp.where(lane < 2 * nh, dt, 0.0)
        da = dt * -jnp.exp(alog_ref[...])
        acs = _tri_dot(tril, da)
        rcs = _tri_dot(triu, da)
        cs = jnp.where(lane < nh, acs, rcs)
        end = jnp.where(lane < nh, acs[CHUNK - 1:CHUNK, :], rcs[0:1, :])
        cs_ref[rows, :] = cs
        dd_ref[rows, :] = jnp.exp(cs)
        ww_ref[rows, :] = jnp.exp(end - cs) * dt
        zt_ref[rows, :] = jnp.where(lane < 2 * nh, dt, pltpu.roll(cs, 2 * nh, 1)).T


def _ssd_prep(dt_raw, dt_bias, a_log):
    m = dt_raw.shape[0]
    rows = PREP_CHUNKS * CHUNK
    blk = pl.BlockSpec((rows, LANES), lambda i: (i, 0))
    vec = pl.BlockSpec((1, LANES), lambda i: (0, 0))
    return pl.pallas_call(
        _ssd_prep_kernel,
        grid=(m // rows,),
        in_specs=[blk, vec, vec],
        out_specs=[blk] * 4,
        out_shape=[jax.ShapeDtypeStruct((m, LANES), F32)] * 4,
        compiler_params=_cparams(1),
        name="ssd_prep",
    )(dt_raw, dt_bias, a_log)


def _expand_operand(x):
    hi, lo = _split_bf16(x)
    return jnp.concatenate([hi, lo], axis=1)


def _expand(x_hl, e2_ref, gs):
    return jnp.dot(x_hl, e2_ref[:, gs], preferred_element_type=F32)


def _shift_rows(u, halo, offset):
    n = u.shape[0]
    if offset == 0:
        return u
    rolled = pltpu.roll(u, (-offset) % n, 0)
    row8 = lax.broadcasted_iota(jnp.int32, (8, u.shape[1]), 0)
    if offset < 0:
        hfix = pltpu.roll(halo, (-offset) % 8, 0)
        first = jnp.where(row8 < -offset, hfix, rolled[0:8])
        return jnp.concatenate([first, rolled[8:n]], axis=0)
    hfix = pltpu.roll(halo, (8 - offset) % 8, 0)
    last = jnp.where(row8 >= 8 - offset, hfix, rolled[n - 8:n])
    return jnp.concatenate([rolled[0:n - 8], last], axis=0)


def _shift_select(n, shifts):
    t = jnp.arange(len(shifts) * n)
    src = t % n + HALO + jnp.repeat(jnp.asarray(shifts), n)
    return (src[:, None] == jnp.arange(n + 2 * HALO)[None, :]).astype(BF16)


SCAN_CHUNKS = 4


def _ssd_bwd_kernel(xc_ref, xp_ref, xn_ref, dd_ref, ww_ref, cw_ref, cb_ref, e2b_ref, sel_ref,
                    xact_ref, ybi_ref, state_scr, *, n_blocks):
    c = pl.program_id(1)
    cr = n_blocks - 1 - c

    @pl.when(c == 0)
    def _():
        state_scr[...] = jnp.zeros(state_scr.shape, F32)

    keep_prev = jnp.where(cr > 0, 1.0, 0.0).astype(BF16)
    keep_next = jnp.where(cr < n_blocks - 1, 1.0, 0.0).astype(BF16)
    slab = 512
    for sl in range(CONV_DIM // slab):
        cs = slice(sl * slab, (sl + 1) * slab)
        ext = jnp.concatenate([xp_ref[:, cs] * keep_prev, xc_ref[:, cs], xn_ref[:, cs] * keep_next], axis=0)
        for k in range(SCAN_CHUNKS):
            rows = slice(k * CHUNK, (k + 1) * CHUNK)
            sh = jnp.dot(sel_ref[...], ext[k * CHUNK:(k + 1) * CHUNK + 2 * HALO], preferred_element_type=F32)
            acc = cb_ref[:, cs] + xc_ref[rows, cs].astype(F32) * cw_ref[2:3, cs]
            for j, off in enumerate(SSM_SHIFTS):
                acc = acc + sh[j * CHUNK:(j + 1) * CHUNK] * cw_ref[off + 2:off + 3, cs]
            xact_ref[rows, cs] = _silu(acc).astype(BF16)

    for k in reversed(range(SCAN_CHUNKS)):
        rows = slice(k * CHUNK, (k + 1) * CHUNK)
        everything = slice(0, SSM_INNER)
        decay = _expand(_expand_operand(dd_ref[rows, :]), e2b_ref, everything)
        chunk_decay = decay[0:1, :]
        xs = xact_ref[rows, 0:SSM_INNER].astype(F32)
        xw = (xs * _expand(_expand_operand(ww_ref[rows, :]), e2b_ref, everything)).astype(BF16)
        for g in range(SSM_GROUPS):
            gs = slice(g * GROUP_W, (g + 1) * GROUP_W)
            bg = xact_ref[rows, SSM_INNER + g * SSM_STATE:SSM_INNER + (g + 1) * SSM_STATE]
            cg = xact_ref[rows, SSM_INNER + (SSM_GROUPS + g) * SSM_STATE:SSM_INNER + (SSM_GROUPS + g + 1) * SSM_STATE]
            st = state_scr[:, gs]
            yb = jnp.dot(cg, st.astype(BF16), preferred_element_type=F32) * decay[:, gs]
            ybi_ref[rows, gs] = yb.astype(BF16)
            upd = lax.dot_general(bg, xw[:, gs], (((0,), (0,)), ((), ())), preferred_element_type=F32)
            state_scr[:, gs] = st * chunk_decay[:, gs] + upd


def _ssd_bwd(proj, dd, ww, conv_w, conv_b, e2b, batch, seq):
    rows = SCAN_CHUNKS * CHUNK
    n_blocks = seq // rows
    sub = rows // HALO
    last16 = batch * seq // HALO - 1
    kern = functools.partial(_ssd_bwd_kernel, n_blocks=n_blocks)
    select = _shift_select(CHUNK, SSM_SHIFTS)

    def cur(b, c):
        return b * n_blocks + (n_blocks - 1 - c)

    return pl.pallas_call(
        kern,
        grid=(batch, n_blocks),
        in_specs=[
            pl.BlockSpec((rows, CONV_DIM), lambda b, c: (cur(b, c), 0)),
            pl.BlockSpec((HALO, CONV_DIM), lambda b, c: (jnp.maximum(cur(b, c) * sub - 1, 0), 0)),
            pl.BlockSpec((HALO, CONV_DIM), lambda b, c: (jnp.minimum((cur(b, c) + 1) * sub, last16), 0)),
            pl.BlockSpec((rows, LANES), lambda b, c: (cur(b, c), 0)),
            pl.BlockSpec((rows, LANES), lambda b, c: (cur(b, c), 0)),
            pl.BlockSpec((4, CONV_DIM), lambda b, c: (0, 0)),
            pl.BlockSpec((1, CONV_DIM), lambda b, c: (0, 0)),
            pl.BlockSpec((2 * LANES, SSM_INNER), lambda b, c: (0, 0)),
            pl.BlockSpec(select.shape, lambda b, c: (0, 0)),
        ],
        out_specs=[
            pl.BlockSpec((rows, CONV_DIM), lambda b, c: (cur(b, c), 0)),
            pl.BlockSpec((rows, SSM_INNER), lambda b, c: (cur(b, c), 0)),
        ],
        out_shape=[
            jax.ShapeDtypeStruct((batch * seq, CONV_DIM), BF16),
            jax.ShapeDtypeStruct((batch * seq, SSM_INNER), BF16),
        ],
        scratch_shapes=[pltpu.VMEM((SSM_STATE, SSM_INNER), F32)],
        compiler_params=_cparams(2),
        name="ssd_bwd",
    )(proj, proj, proj, dd, ww, conv_w, conv_b, e2b, select)


def _ssd_fwd_kernel(xact_ref, ybi_ref, zt_ref, cs_ref, dd_ref, ww_ref, dsk_ref, e2f_ref, out_ref, state_scr):
    @pl.when(pl.program_id(1) == 0)
    def _():
        state_scr[...] = jnp.zeros(state_scr.shape, F32)

    nh = SSM_HEADS
    lane_w = lax.broadcasted_iota(jnp.int32, (CHUNK, LANES), 1)
    row_i = lax.broadcasted_iota(jnp.int32, (CHUNK, CHUNK), 0)
    col_i = lax.broadcasted_iota(jnp.int32, (CHUNK, CHUNK), 1)
    lower = col_i < row_i
    upper = col_i > row_i

    for k in range(SCAN_CHUNKS):
        rows = slice(k * CHUNK, (k + 1) * CHUNK)
        zt = zt_ref.at[rows]
        cs = cs_ref[rows, :]
        dd_hl = _expand_operand(dd_ref[rows, :])
        ww_hl = _expand_operand(ww_ref[rows, :])
        for g in range(SSM_GROUPS):
            gs = slice(g * GROUP_W, (g + 1) * GROUP_W)
            bg = xact_ref[rows, SSM_INNER + g * SSM_STATE:SSM_INNER + (g + 1) * SSM_STATE]
            cg = xact_ref[rows, SSM_INNER + (SSM_GROUPS + g) * SSM_STATE:SSM_INNER + (SSM_GROUPS + g + 1) * SSM_STATE]
            xs_bf = xact_ref[rows, gs]
            xs = xs_bf.astype(F32)
            decay = _expand(dd_hl, e2f_ref, gs)
            xw = (xs * _expand(ww_hl, e2f_ref, gs)).astype(BF16)
            cb = lax.dot_general(cg, bg, (((1,), (1,)), ((), ())), preferred_element_type=F32)
            ys = []
            for r in range(GROUP_W // SSM_HEAD_DIM):
                h = g * (GROUP_W // SSM_HEAD_DIM) + r
                seg_f = cs[:, h:h + 1] - zt[2 * nh + h:2 * nh + h + 1, :]
                seg_b = cs[:, nh + h:nh + h + 1] - zt[3 * nh + h:3 * nh + h + 1, :]
                dt_f = zt[h:h + 1, :]
                dt_b = zt[nh + h:nh + h + 1, :]
                dt_sel = jnp.where(lower, dt_f, jnp.where(upper, dt_b, dt_f + dt_b))
                mat = (cb * (jnp.exp(jnp.where(upper, seg_b, seg_f)) * dt_sel)).astype(BF16)
                pair = xs_bf[:, (r // 2) * LANES:(r // 2 + 1) * LANES]
                ys.append(jnp.dot(mat, pair, preferred_element_type=F32))
            y = jnp.concatenate([jnp.where(lane_w < SSM_HEAD_DIM, ys[0], ys[1]),
                                 jnp.where(lane_w < SSM_HEAD_DIM, ys[2], ys[3])], axis=1)
            st = state_scr[:, gs]
            y = y + jnp.dot(cg, st.astype(BF16), preferred_element_type=F32) * decay
            upd = lax.dot_general(bg, xw, (((0,), (0,)), ((), ())), preferred_element_type=F32)
            state_scr[:, gs] = st * decay[CHUNK - 1:CHUNK, :] + upd
            y = y + ybi_ref[rows, gs].astype(F32) + dsk_ref[:, gs] * xs
            out_ref[rows, gs] = y.astype(BF16)


def _ssd_fwd(xact, ybi, zt, cs, dd, ww, d_skip, e2f, batch, seq):
    rows = SCAN_CHUNKS * CHUNK
    n_blocks = seq // rows
    small = pl.BlockSpec((rows, LANES), lambda b, c: (b * n_blocks + c, 0))
    return pl.pallas_call(
        _ssd_fwd_kernel,
        grid=(batch, n_blocks),
        in_specs=[
            pl.BlockSpec((rows, CONV_DIM), lambda b, c: (b * n_blocks + c, 0)),
            pl.BlockSpec((rows, SSM_INNER), lambda b, c: (b * n_blocks + c, 0)),
            small, small, small, small,
            pl.BlockSpec((1, SSM_INNER), lambda b, c: (0, 0)),
            pl.BlockSpec((2 * LANES, SSM_INNER), lambda b, c: (0, 0)),
        ],
        out_specs=pl.BlockSpec((rows, SSM_INNER), lambda b, c: (b * n_blocks + c, 0)),
        out_shape=jax.ShapeDtypeStruct((batch * seq, SSM_INNER), BF16),
        scratch_shapes=[pltpu.VMEM((SSM_STATE, SSM_INNER), F32)],
        compiler_params=_cparams(2),
        name="ssd_fwd",
    )(xact, ybi, zt, cs, dd, ww, d_skip, e2f)


def _mix_out_kernel(x_ref, ao_ref, so_ref, z_ref, nw_ref, ga_ref, gs_ref, wa_ref, ws_ref, wo_ref, n2_ref, x1_ref,
                    h2_ref):
    attn = jnp.dot(ao_ref[...], wa_ref[...], preferred_element_type=F32)
    ssd = None
    for g in range(SSM_GROUPS):
        gs = slice(g * GROUP_W, (g + 1) * GROUP_W)
        y = so_ref[:, gs].astype(F32) * _silu(z_ref[:, gs].astype(F32))
        part = jnp.dot(_rms(y, nw_ref[:, gs]).astype(BF16), ws_ref[gs, :], preferred_element_type=F32)
        ssd = part if ssd is None else ssd + part
    mixed = (jax.nn.sigmoid(ga_ref[...].astype(F32)) * attn
             + jax.nn.sigmoid(gs_ref[...].astype(F32)) * ssd)
    x1 = x_ref[...] + jnp.dot(mixed.astype(BF16), wo_ref[...], preferred_element_type=F32)
    x1_ref[...] = x1
    h2_ref[...] = _rms(x1, n2_ref[...]).astype(BF16)


def _mix_out(x2d, attn_o, ssd_o, proj, ssm_norm_w, w_attn, w_ssm, w_out, norm2_w, tm):
    m, d = x2d.shape
    gb = COL_GATE // d
    zb = COL_Z // SSM_INNER
    return pl.pallas_call(
        _mix_out_kernel,
        grid=(m // tm,),
        in_specs=[
            pl.BlockSpec((tm, d), lambda i: (i, 0)),
            pl.BlockSpec((tm, attn_o.shape[1]), lambda i: (i, 0)),
            pl.BlockSpec((tm, ssd_o.shape[1]), lambda i: (i, 0)),
            pl.BlockSpec((tm, SSM_INNER), lambda i: (i, zb)),
            pl.BlockSpec((1, SSM_INNER), lambda i: (0, 0)),
            pl.BlockSpec((tm, d), lambda i: (i, gb)),
            pl.BlockSpec((tm, d), lambda i: (i, gb + 1)),
            pl.BlockSpec(w_attn.shape, lambda i: (0, 0)),
            pl.BlockSpec(w_ssm.shape, lambda i: (0, 0)),
            pl.BlockSpec(w_out.shape, lambda i: (0, 0)),
            pl.BlockSpec((1, d), lambda i: (0, 0)),
        ],
        out_specs=[pl.BlockSpec((tm, d), lambda i: (i, 0)), pl.BlockSpec((tm, d), lambda i: (i, 0))],
        out_shape=[jax.ShapeDtypeStruct((m, d), F32), jax.ShapeDtypeStruct((m, d), BF16)],
        compiler_params=_cparams(1),
        name="mix_out",
    )(x2d, attn_o, ssd_o, proj, ssm_norm_w, proj, proj, w_attn, w_ssm, w_out, norm2_w)


FFN_CW = 256


def _ffn_kernel(x1_ref, h_ref, hp_ref, hn_ref, wu_ref, cw_ref, cb_ref, wd_ref, o_ref, act_scr, *, tiles_per_seq):
    i = pl.program_id(0)
    tm = h_ref.shape[0]
    f = wd_ref.shape[0]
    pos = i % tiles_per_seq
    keep_prev = jnp.where(pos > 0, 1.0, 0.0)
    keep_next = jnp.where(pos < tiles_per_seq - 1, 1.0, 0.0)
    h_ext = jnp.concatenate([hp_ref[...], h_ref[...], hn_ref[...]], axis=0)

    def conv(col):
        cs = slice(col, col + FFN_CW)
        u_ext = jnp.dot(h_ext, wu_ref[:, cs], preferred_element_type=F32)
        u = u_ext[HALO:HALO + tm]
        up = u_ext[HALO - 8:HALO] * keep_prev
        un = u_ext[HALO + tm:HALO + tm + 8] * keep_next
        return (cb_ref[:, cs] + _shift_rows(u, up, -1) * cw_ref[0:1, cs] + u * cw_ref[1:2, cs]
                + _shift_rows(u, un, 1) * cw_ref[2:3, cs])

    for c in range(f // FFN_CW):
        act = _silu(conv(c * FFN_CW)) * conv(f + c * FFN_CW)
        act_scr[:, c * FFN_CW:(c + 1) * FFN_CW] = act.astype(BF16)
    o_ref[...] = x1_ref[...] + jnp.dot(act_scr[...], wd_ref[...], preferred_element_type=F32)


def _ffn(x1, h2, w_up, conv_w, conv_b, w_down, seq, tm):
    m, d = x1.shape
    f = w_down.shape[0]
    sub = tm // HALO
    last = m // HALO - 1
    kern = functools.partial(_ffn_kernel, tiles_per_seq=seq // tm)
    resident = pl.Buffered(1)
    return pl.pallas_call(
        kern,
        grid=(m // tm,),
        in_specs=[
            pl.BlockSpec((tm, d), lambda i: (i, 0)),
            pl.BlockSpec((tm, d), lambda i: (i, 0)),
            pl.BlockSpec((HALO, d), lambda i: (jnp.maximum(i * sub - 1, 0), 0)),
            pl.BlockSpec((HALO, d), lambda i: (jnp.minimum((i + 1) * sub, last), 0)),
            pl.BlockSpec(w_up.shape, lambda i: (0, 0), pipeline_mode=resident),
            pl.BlockSpec(conv_w.shape, lambda i: (0, 0)),
            pl.BlockSpec(conv_b.shape, lambda i: (0, 0)),
            pl.BlockSpec(w_down.shape, lambda i: (0, 0), pipeline_mode=resident),
        ],
        out_specs=pl.BlockSpec((tm, d), lambda i: (i, 0)),
        out_shape=jax.ShapeDtypeStruct((m, d), F32),
        scratch_shapes=[pltpu.VMEM((tm, f), BF16)],
        compiler_params=_cparams(1),
        name="ffn",
    )(x1, h2, h2, h2, w_up, conv_w, conv_b, w_down)


def _head_expand(first_row):
    rows = jnp.arange(2 * LANES)[:, None] % LANES
    cols = jnp.arange(SSM_INNER)[None, :] // SSM_HEAD_DIM
    return (rows == cols + first_row).astype(BF16)


def _layer(x2d, batch, seq, lambda_init, p):
    d = x2d.shape[1]
    w_in = p["w_in"].astype(BF16)
    sizes = (HEADS * 2 * HEAD_DIM,) * 3 + (SSM_INNER, CONV_DIM, 2 * SSM_HEADS, 2 * d)
    offs = [0]
    for s in sizes:
        offs.append(offs[-1] + s)
    wq, wk, wv, wz, wxbc, wdt, wg = (w_in[:, offs[n]:offs[n + 1]] for n in range(7))
    w_main = jnp.concatenate([wxbc, wz, wg, wv], axis=1)
    w_qk = jnp.concatenate([wq, wk], axis=1)
    w_dt = jnp.pad(wdt, ((0, 0), (0, LANES - wdt.shape[1])))
    qkw = jnp.stack([jnp.tile(p["q_norm_w"], 2 * HEADS) * (HEAD_DIM ** -0.5 * LOG2E),
                     jnp.tile(p["k_norm_w"], 2 * HEADS)])[:, None, :]
    gi = jnp.arange(MXU_DIM) // HEAD_DIM
    gsum = (gi[:, None] == gi[None, :]).astype(BF16)

    proj, dt_raw, h1 = _in_proj(x2d, p["norm1_w"][None, :], w_main, w_dt, tm=2048)
    qk = _qk_proj(h1, w_qk, qkw, gsum, tm=1024)

    lam = (jnp.exp(jnp.sum(p["lambda_q1"] * p["lambda_k1"])) - jnp.exp(jnp.sum(p["lambda_q2"] * p["lambda_k2"]))
           + lambda_init).reshape(1).astype(F32)
    bias_small = _bias_tiles(p["rel_bias"])
    attn_o = _diff_attn(qk, proj, bias_small, p["rel_bias"], lam, p["subln_w"][None, :], batch, seq,
                        tq=512, tk=1024, out_scale=1.0 - lambda_init)

    pad = LANES - 2 * SSM_HEADS
    dt_bias = jnp.pad(jnp.concatenate([p["dt_bias_f"], p["dt_bias_b"]]), (0, pad))[None, :]
    a_log = jnp.pad(jnp.concatenate([p["a_log_f"], p["a_log_b"]]), (0, pad))[None, :]
    zt, cs, dd, ww = _ssd_prep(dt_raw, dt_bias, a_log)
    xact, ybi = _ssd_bwd(proj, dd, ww, p["ssm_conv_w"], p["ssm_conv_b"][None, :],
                         _head_expand(SSM_HEADS), batch, seq)
    ssd_y = _ssd_fwd(xact, ybi, zt, cs, dd, ww, jnp.repeat(p["d_skip"], SSM_HEAD_DIM)[None, :],
                     _head_expand(0), batch, seq)

    x1, h2 = _mix_out(x2d, attn_o, ssd_y, proj, p["ssm_norm_w"][None, :], p["w_attn_out"].astype(BF16),
                      p["w_ssm_out"].astype(BF16), p["w_out"].astype(BF16), p["norm2_w"][None, :], tm=512)

    return _ffn(x1, h2, p["w_ffn_up"].astype(BF16), p["ffn_conv_w"], p["ffn_conv_b"][None, :],
                p["w_ffn_down"].astype(BF16), seq, tm=1024)


def kernel(x, norm1_w, w_in, q_norm_w, k_norm_w, rel_bias, lambda_q1, lambda_k1, lambda_q2, lambda_k2, subln_w, w_attn_out, ssm_conv_w, ssm_conv_b, dt_bias_f, a_log_f, dt_bias_b, a_log_b, d_skip, ssm_norm_w, w_ssm_out, w_out, norm2_w, w_ffn_up, ffn_conv_w, ffn_conv_b, w_ffn_down):
    batch, seq, d = x.shape
    layered = dict(norm1_w=norm1_w, w_in=w_in, q_norm_w=q_norm_w, k_norm_w=k_norm_w, lambda_q1=lambda_q1,
                   lambda_k1=lambda_k1, lambda_q2=lambda_q2, lambda_k2=lambda_k2, subln_w=subln_w,
                   w_attn_out=w_attn_out, ssm_conv_w=ssm_conv_w, ssm_conv_b=ssm_conv_b, dt_bias_f=dt_bias_f,
                   a_log_f=a_log_f, dt_bias_b=dt_bias_b, a_log_b=a_log_b, d_skip=d_skip, ssm_norm_w=ssm_norm_w,
                   w_ssm_out=w_ssm_out, w_out=w_out, norm2_w=norm2_w, w_ffn_up=w_ffn_up, ffn_conv_w=ffn_conv_w,
                   ffn_conv_b=ffn_conv_b, w_ffn_down=w_ffn_down)
    x2d = x.reshape(batch * seq, d)
    for layer in range(norm1_w.shape[0]):
        p = {k: v[layer] for k, v in layered.items()}
        p["rel_bias"] = rel_bias
        lambda_init = 0.8 - 0.6 * math.exp(-0.3 * layer)
        x2d = _layer(x2d, batch, seq, lambda_init, p)
    return x2d.reshape(batch, seq, d)
```
